```python
import jax, jax.numpy as jnp
from jax import lax
import numpy as np

D_MODEL = 1024
BATCH = 16
SEQ = 4096
DEPTH = 4

N_A_LAYERS = DEPTH // 2
N_B_LAYERS = DEPTH - N_A_LAYERS

A_HEADS = 8
A_QK_DIM = D_MODEL // 2
A_V_DIM = D_MODEL
A_QK_HEAD = A_QK_DIM // A_HEADS
A_V_HEAD = A_V_DIM // A_HEADS
A_IN_DIM = 2 * A_QK_DIM + 2 * A_V_DIM + 2 * A_HEADS
A_CHUNK = 64
GATE_SOFTCAP = 15.0

B_HEADS = 16
B_HEAD_DIM = D_MODEL // B_HEADS
Q_BLOCK = 128
KV_OUT_DIM = 2 * D_MODEL + B_HEADS

N_EXPERTS = 16
N_GROUPS = 4
EXPERTS_PER_GROUP = N_EXPERTS // N_GROUPS
TOP_K = 2
D_EXPERT = D_MODEL // 2
MOE_BLOCK = 256

N_MOD = 6
EPS = 1e-6
F32 = jnp.float32

kernel_name = "mlstm_fox_yoco_grouped_moe_adaln"


def rmsnorm(x, g):
    xf = x.astype(F32)
    y = xf * lax.rsqrt(jnp.mean(xf * xf, axis=-1, keepdims=True) + EPS)
    return (y * g.astype(F32)).astype(x.dtype)


def modulate(h, shift, scale):
    return h * (1 + scale[:, None, :]) + shift[:, None, :]


def softcap(x, cap):
    return cap * jnp.tanh(x / cap)


def mlstm_cell(q, k, v, ig, lf):
    B, H, S, dk = q.shape
    dv = v.shape[-1]
    nc = S // A_CHUNK

    def chunks(t):
        return jnp.moveaxis(t.reshape(t.shape[:2] + (nc, A_CHUNK) + t.shape[3:]), 2, 0)

    tri = jnp.tril(jnp.ones((A_CHUNK, A_CHUNK), dtype=bool))

    def step(carry, xs):
        C, n, m = carry
        qc, kc, vc, igc, lfc = xs
        b = jnp.cumsum(lfc, axis=-1)
        d_intra = jnp.where(tri, b[..., :, None] - b[..., None, :] + igc[..., None, :], -jnp.inf)
        d_inter = b + m[..., None]
        m_t = jnp.maximum(d_inter, jnp.max(d_intra, axis=-1))
        w = jnp.exp(d_intra - m_t[..., None]) * jnp.einsum('bhtd,bhsd->bhts', qc, kc)
        a = jnp.exp(d_inter - m_t)
        num = jnp.einsum('bhts,bhsv->bhtv', w, vc) + a[..., None] * jnp.einsum('bhtd,bhdv->bhtv', qc, C)
        den = jnp.sum(w, axis=-1) + a * jnp.einsum('bhtd,bhd->bht', qc, n)
        h = num / jnp.maximum(jnp.abs(den), jnp.exp(-m_t))[..., None]
        b_last = b[..., -1]
        g = b_last[..., None] - b + igc
        m_new = jnp.maximum(b_last + m, jnp.max(g, axis=-1))
        wk = jnp.exp(g - m_new[..., None])
        decay = jnp.exp(b_last + m - m_new)
        C = decay[..., None, None] * C + jnp.einsum('bhs,bhsd,bhsv->bhdv', wk, kc, vc)
        n = decay[..., None] * n + jnp.einsum('bhs,bhsd->bhd', wk, kc)
        return (C, n, m_new), h

    init = (jnp.zeros((B, H, dk, dv), F32), jnp.zeros((B, H, dk), F32), jnp.zeros((B, H), F32))
    _, hs = lax.scan(step, init, (chunks(q), chunks(k), chunks(v), chunks(ig), chunks(lf)))
    return jnp.moveaxis(hs, 0, 2).reshape(B, H, S, dv)


def mlstm_layer(h, w_in, b_gates, mh_norm, w_out):
    B, S, _ = h.shape
    proj = jnp.einsum('bsd,de->bse', h, w_in)
    q, k, v, o, gates = jnp.split(
        proj, [A_QK_DIM, 2 * A_QK_DIM, 2 * A_QK_DIM + A_V_DIM, 2 * A_QK_DIM + 2 * A_V_DIM], axis=-1)
    gates = softcap(gates.astype(F32) + b_gates.astype(F32), GATE_SOFTCAP)
    ig = jnp.transpose(gates[..., :A_HEADS], (0, 2, 1))
    lf = jax.nn.log_sigmoid(jnp.transpose(gates[..., A_HEADS:], (0, 2, 1)))

    def heads(t, dh):
        return t.astype(F32).reshape(B, S, A_HEADS, dh).transpose(0, 2, 1, 3)

    hs = mlstm_cell(heads(q, A_QK_HEAD) * (A_QK_HEAD ** -0.5), heads(k, A_QK_HEAD), heads(v, A_V_HEAD), ig, lf)
    hs = hs * lax.rsqrt(jnp.mean(hs * hs, axis=-1, keepdims=True) + EPS)
    hs = hs.transpose(0, 2, 1, 3) * mh_norm.astype(F32).reshape(A_HEADS, A_V_HEAD)
    out = hs.reshape(B, S, A_V_DIM) * jax.nn.sigmoid(o.astype(F32))
    return jnp.einsum('bse,ed->bsd', out.astype(h.dtype), w_out)


def shared_kv(x, kv_norm, shift, scale, w_kv, b_fgate):
    B, S, _ = x.shape
    h = modulate(rmsnorm(x, kv_norm), shift, scale)
    kvf = jnp.einsum('bsd,de->bse', h, w_kv)
    k = kvf[..., :D_MODEL].reshape(B, S, B_HEADS, B_HEAD_DIM).transpose(0, 2, 1, 3)
    v = kvf[..., D_MODEL:2 * D_MODEL].reshape(B, S, B_HEADS, B_HEAD_DIM).transpose(0, 2, 1, 3)
    log_f = jax.nn.log_sigmoid(kvf[..., 2 * D_MODEL:].astype(F32) + b_fgate.astype(F32))
    fcum = jnp.cumsum(log_f, axis=1).transpose(0, 2, 1)
    return k, v, fcum


def fox_layer(h, w_q, w_o, k, v, fcum):
    B, S, _ = h.shape
    nb = S // Q_BLOCK
    q = jnp.einsum('bsd,de->bse', h, w_q).reshape(B, S, B_HEADS, B_HEAD_DIM).transpose(0, 2, 1, 3)
    q_blocks = jnp.moveaxis(q.reshape(B, B_HEADS, nb, Q_BLOCK, B_HEAD_DIM), 2, 0)
    f_blocks = jnp.moveaxis(fcum.reshape(B, B_HEADS, nb, Q_BLOCK), 2, 0)
    offs = jnp.arange(Q_BLOCK)
    scale = B_HEAD_DIM ** -0.5

    def query_block(args):
        i, qi, fi = args
        qi = qi.astype(F32) * scale
        q_pos = i * Q_BLOCK + offs

        def key_block(j, carry):
            m, l, acc = carry
            kj = lax.dynamic_slice_in_dim(k, j * Q_BLOCK, Q_BLOCK, axis=2).astype(F32)
            vj = lax.dynamic_slice_in_dim(v, j * Q_BLOCK, Q_BLOCK, axis=2).astype(F32)
            fj = lax.dynamic_slice_in_dim(fcum, j * Q_BLOCK, Q_BLOCK, axis=2)
            s = jnp.einsum('bhtd,bhsd->bhts', qi, kj) + fi[..., :, None] - fj[..., None, :]
            s = jnp.where(q_pos[:, None] >= (j * Q_BLOCK + offs)[None, :], s, -jnp.inf)
            m_new = jnp.maximum(m, jnp.max(s, axis=-1))
            p = jnp.exp(s - m_new[..., None])
            corr = jnp.exp(m - m_new)
            return (m_new, corr * l + jnp.sum(p, axis=-1),
                    corr[..., None] * acc + jnp.einsum('bhts,bhsd->bhtd', p, vj))

        init = (jnp.full((B, B_HEADS, Q_BLOCK), -jnp.inf, F32), jnp.zeros((B, B_HEADS, Q_BLOCK), F32),
                jnp.zeros((B, B_HEADS, Q_BLOCK, B_HEAD_DIM), F32))
        _, l, acc = lax.fori_loop(0, i + 1, key_block, init)
        return acc / l[..., None]

    o = lax.map(query_block, (jnp.arange(nb), q_blocks, f_blocks))
    o = jnp.moveaxis(o, 0, 2).reshape(B, B_HEADS, S, B_HEAD_DIM).transpose(0, 2, 1, 3).reshape(B, S, D_MODEL)
    return jnp.einsum('bse,ed->bsd', o.astype(h.dtype), w_o)


def moe_ffn(h, w_router, router_bias, w_gate, w_up, w_down):
    B, S, D = h.shape
    T = B * S
    A = T * TOP_K
    P = A + N_EXPERTS * MOE_BLOCK
    nblk = P // MOE_BLOCK
    ht = h.reshape(T, D)
    aff = jax.nn.sigmoid(jnp.einsum('td,de->te', ht, w_router).astype(F32))
    sel = (aff + router_bias.astype(F32)).reshape(T, N_GROUPS, EXPERTS_PER_GROUP)
    grp_score = jnp.sum(lax.top_k(sel, TOP_K)[0], axis=-1)
    grp = jnp.argmax(grp_score, axis=-1)
    in_grp = jnp.take_along_axis(sel, grp[:, None, None], axis=1)[:, 0]
    local = lax.top_k(in_grp, TOP_K)[1]
    e_idx = grp[:, None] * EXPERTS_PER_GROUP + local
    w_sel = jnp.take_along_axis(aff, e_idx, axis=1)
    gates = w_sel / jnp.sum(w_sel, axis=-1, keepdims=True)
    flat_e = e_idx.reshape(-1).astype(jnp.int32)
    flat_tok = jnp.repeat(jnp.arange(T, dtype=jnp.int32), TOP_K)
    flat_w = gates.reshape(-1)
    order = jnp.argsort(flat_e)
    sorted_e = flat_e[order]
    counts = jnp.bincount(flat_e, length=N_EXPERTS)
    padded = (counts + MOE_BLOCK - 1) // MOE_BLOCK * MOE_BLOCK
    pad_end = jnp.cumsum(padded)
    pad_start = pad_end - padded
    start = jnp.cumsum(counts) - counts
    dest = pad_start[sorted_e] + jnp.arange(A, dtype=jnp.int32) - start[sorted_e]
    buf_tok = jnp.full((P,), T, jnp.int32).at[dest].set(flat_tok[order])
    buf_w = jnp.zeros((P,), F32).at[dest].set(flat_w[order])
    blk_expert = jnp.clip(jnp.searchsorted(pad_end, jnp.arange(nblk, dtype=pad_end.dtype) * MOE_BLOCK,
                                           side='right'), 0, N_EXPERTS - 1)
    h_pad = jnp.concatenate([ht, jnp.zeros((1, D), ht.dtype)], axis=0)

    def expert_block(args):
        tok, e = args
        xb = h_pad[tok]
        return jnp.einsum('tf,fd->td', jax.nn.silu(xb @ w_gate[e]) * (xb @ w_up[e]), w_down[e])

    ys = lax.map(expert_block, (buf_tok.reshape(nblk, MOE_BLOCK), blk_expert))
    ys = ys.reshape(P, D).astype(F32) * buf_w[:, None]
    out = jax.ops.segment_sum(ys, buf_tok, num_segments=T + 1)[:T]
    return out.reshape(B, S, D).astype(h.dtype)


def setup_inputs(seed: int = 0) -> dict:
    key = jax.random.key(seed)
    ks = jax.random.split(key, 26)

    def nrm(k, shape, scale):
        return jax.random.normal(k, shape, F32) * scale

    d = D_MODEL
    ada = 0.5 * d ** -0.5
    return {
        "x": nrm(ks[0], (BATCH, SEQ, d), 1.0),
        "c": nrm(ks[1], (BATCH, d), 1.0),
        "a_w_in": nrm(ks[2], (N_A_LAYERS, d, A_IN_DIM), d ** -0.5),
        "a_b_gates": jnp.concatenate([
            nrm(ks[3], (N_A_LAYERS, A_HEADS), 0.1),
            jnp.linspace(3.0, 6.0, A_HEADS, dtype=F32)[None, :] + nrm(ks[4], (N_A_LAYERS, A_HEADS), 0.1)], axis=-1),
        "a_mh_norm": 1.0 + nrm(ks[5], (N_A_LAYERS, A_V_DIM), 0.02),
        "a_w_out": nrm(ks[6], (N_A_LAYERS, A_V_DIM, d), A_V_DIM ** -0.5),
        "kv_norm": 1.0 + nrm(ks[7], (d,), 0.02),
        "w_ada_kv": nrm(ks[8], (d, 2 * d), ada),
        "b_ada_kv": nrm(ks[9], (2 * d,), 0.01),
        "w_kv": nrm(ks[10], (d, KV_OUT_DIM), d ** -0.5),
        "b_fgate": jnp.linspace(2.0, 6.0, B_HEADS, dtype=F32) + nrm(ks[11], (B_HEADS,), 0.1),
        "b_w_q": nrm(ks[12], (N_B_LAYERS, d, d), d ** -0.5),
        "b_w_o": nrm(ks[13], (N_B_LAYERS, d, d), d ** -0.5),
        "norm_mix": 1.0 + nrm(ks[14], (DEPTH, d), 0.02),
        "norm_ffn": 1.0 + nrm(ks[15], (DEPTH, d), 0.02),
        "w_ada": nrm(ks[16], (DEPTH, d, N_MOD * d), ada),
        "b_ada": nrm(ks[17], (DEPTH, N_MOD * d), 0.01),
        "w_router": nrm(ks[18], (d, N_EXPERTS), d ** -0.5),
        "router_bias": nrm(ks[19], (N_EXPERTS,), 0.01),
        "w_gate": nrm(ks[20], (DEPTH, N_EXPERTS, d, D_EXPERT), d ** -0.5),
        "w_up": nrm(ks[21], (DEPTH, N_EXPERTS, d, D_EXPERT), d ** -0.5),
        "w_down": nrm(ks[22], (DEPTH, N_EXPERTS, D_EXPERT, d), D_EXPERT ** -0.5),
        "norm_final": 1.0 + nrm(ks[23], (d,), 0.02),
        "w_ada_final": nrm(ks[24], (d, 2 * d), ada),
        "b_ada_final": nrm(ks[25], (2 * d,), 0.01),
    }


def reference(x, c, a_w_in, a_b_gates, a_mh_norm, a_w_out, kv_norm, w_ada_kv, b_ada_kv, w_kv, b_fgate,
              b_w_q, b_w_o, norm_mix, norm_ffn, w_ada, b_ada, w_router, router_bias, w_gate, w_up, w_down,
              norm_final, w_ada_final, b_ada_final):
    c_act = jax.nn.silu(c)
    mods = jnp.einsum('bd,lde->lbe', c_act, w_ada) + b_ada[:, None, :]
    k_sh = v_sh = fcum = None
    for layer in range(DEPTH):
        sh_m, sc_m, g_m, sh_f, sc_f, g_f = jnp.split(mods[layer], N_MOD, axis=-1)
        h = modulate(rmsnorm(x, norm_mix[layer]), sh_m, sc_m)
        if layer < N_A_LAYERS:
            mix = mlstm_layer(h, a_w_in[layer], a_b_gates[layer], a_mh_norm[layer], a_w_out[layer])
        else:
            if layer == N_A_LAYERS:
                kv_mod = c_act @ w_ada_kv + b_ada_kv
                k_sh, v_sh, fcum = shared_kv(x, kv_norm, kv_mod[:, :D_MODEL], kv_mod[:, D_MODEL:], w_kv, b_fgate)
            j = layer - N_A_LAYERS
            mix = fox_layer(h, b_w_q[j], b_w_o[j], k_sh, v_sh, fcum)
        x = x + g_m[:, None, :] * mix
        h = modulate(rmsnorm(x, norm_ffn[layer]), sh_f, sc_f)
        x = x + g_f[:, None, :] * moe_ffn(h, w_router, router_bias, w_gate[layer], w_up[layer], w_down[layer])
    fin = c_act @ w_ada_final + b_ada_final
    return modulate(rmsnorm(x, norm_final), fin[:, :D_MODEL], fin[:, D_MODEL:])
```

```python
import functools

import jax
import jax.numpy as jnp
from jax import lax
from jax.experimental import pallas as pl
from jax.experimental.pallas import tpu as pltpu

F32 = jnp.float32
BF16 = jnp.bfloat16
EPS = 1e-6
GATE_SOFTCAP = 15.0

A_HEADS = 8
B_HEADS = 16
N_EXPERTS = 16
N_GROUPS = 4
EPG = N_EXPERTS // N_GROUPS
N_PAIRS = 6
N_CLASSES = N_GROUPS * N_PAIRS
CLS_PAD = 32

VMEM_LIMIT = 56 * 1024 * 1024

TM = 512
A_L = 128
TQ = 256
MB = 256
RB = 1024

HIGHEST = lax.Precision.HIGHEST


def _cp(sem):
    return pltpu.CompilerParams(dimension_semantics=sem, vmem_limit_bytes=VMEM_LIMIT)


def _dot(a, b, precision=None):
    return jnp.dot(a, b, preferred_element_type=F32, precision=precision)


def _dot_nt(a, b, precision=None):
    return lax.dot_general(a, b, (((1,), (1,)), ((), ())), preferred_element_type=F32, precision=precision)


def _dot_tn(a, b, precision=None):
    return lax.dot_general(a, b, (((0,), (0,)), ((), ())), preferred_element_type=F32, precision=precision)


def _norm_mod(xin, nw, shift, scale):
    ms = jnp.mean(xin * xin, axis=-1, keepdims=True)
    y = xin * lax.rsqrt(ms + EPS)
    return (y * nw) * (1.0 + scale) + shift


def _log_sigmoid(x):
    return jnp.minimum(x, 0.0) - jnp.log1p(jnp.exp(-jnp.abs(x)))


def _sigmoid(x):
    return 1.0 / (1.0 + jnp.exp(-x))


def _ada_kernel(c_ref, w_ref, b_ref, o_ref):
    c = c_ref[...]
    ca = c * _sigmoid(c)
    o_ref[...] = _dot(ca, w_ref[...], HIGHEST) + b_ref[...]


def ada(c, w, b):
    nl, d, n = w.shape
    bsz = c.shape[0]
    tn = 512
    return pl.pallas_call(
        _ada_kernel,
        grid=(nl, n // tn),
        in_specs=[
            pl.BlockSpec((bsz, d), lambda l, j: (0, 0)),
            pl.BlockSpec((None, d, tn), lambda l, j: (l, 0, j)),
            pl.BlockSpec((None, 1, tn), lambda l, j: (l, 0, j)),
        ],
        out_specs=pl.BlockSpec((None, bsz, tn), lambda l, j: (l, 0, j)),
        out_shape=jax.ShapeDtypeStruct((nl, bsz, n), F32),
        compiler_params=_cp(("arbitrary", "arbitrary")),
        name="ada",
    )(c, w, b.reshape(nl, 1, n))


def _prologue(has_moe, refs):
    if has_moe:
        x_ref, moe_ref, gf_ref, nw_ref, sh_ref, sc_ref = refs[:6]
        rest = refs[6:]
        xin = x_ref[...] + gf_ref[...] * moe_ref[...]
    else:
        x_ref, nw_ref, sh_ref, sc_ref = refs[:4]
        rest = refs[4:]
        xin = x_ref[...]
    h = _norm_mod(xin, nw_ref[...], sh_ref[...], sc_ref[...])
    return xin, h, rest


def _prologue_specs(has_moe, ns, d, tm):
    row = pl.BlockSpec((tm, d), lambda b, s: (b * ns + s, 0))
    vec = pl.BlockSpec((None, 1, d), lambda b, s: (b, 0, 0))
    one = pl.BlockSpec((1, d), lambda b, s: (0, 0))
    if has_moe:
        return [row, row, vec, one, vec, vec]
    return [row, one, vec, vec]


def _full(shape):
    nd = len(shape)
    return pl.BlockSpec(shape, lambda b, s: (0,) * nd)


def _mlstm_in_kernel(has_moe, dqk, dv, *refs):
    xin, h, rest = _prologue(has_moe, refs)
    w_ref, wgt_ref, wg_ref, bcol_ref, brow_ref = rest[:5]
    outs = rest[5:]
    if has_moe:
        xnew_ref = outs[0]
        outs = outs[1:]
        xnew_ref[...] = xin
    q_ref, k_ref, v_ref, o_ref, grow_ref, gcol_ref = outs
    hb = h.astype(BF16)
    scale = float(dqk // A_HEADS) ** -0.5
    q_ref[...] = (_dot(hb, w_ref[:, 0:dqk]) * scale).astype(BF16)
    k_ref[...] = _dot(hb, w_ref[:, dqk:2 * dqk]).astype(BF16)
    v_ref[...] = _dot(hb, w_ref[:, 2 * dqk:2 * dqk + dv]).astype(BF16)
    o_ref[...] = _dot(hb, w_ref[:, 2 * dqk + dv:2 * dqk + 2 * dv]).astype(BF16)
    gr = _dot_nt(wgt_ref[...], hb) + bcol_ref[...]
    gr = GATE_SOFTCAP * jnp.tanh(gr / GATE_SOFTCAP)
    ridx = lax.broadcasted_iota(jnp.int32, gr.shape, 0)
    grow_ref[...] = jnp.where(ridx < A_HEADS, gr, _log_sigmoid(gr))
    gc = _dot(hb, wg_ref[...]) + brow_ref[...]
    gc = GATE_SOFTCAP * jnp.tanh(gc / GATE_SOFTCAP)
    cidx = lax.broadcasted_iota(jnp.int32, gc.shape, 1)
    gcol_ref[...] = jnp.where(cidx < A_HEADS, gc, _log_sigmoid(gc))


def mlstm_in(x, moe, gf, nw, shift, scale, w_in, b_gates, bsz, seq):
    t, d = x.shape
    dqk, dv = d // 2, d
    has_moe = moe is not None
    tm = min(TM, seq)
    ns = seq // tm
    nmain = 2 * dqk + 2 * dv
    w_main = w_in[:, :nmain].astype(BF16)
    w_g = w_in[:, nmain:].astype(BF16)
    w_gt = w_g.T
    ng = 2 * A_HEADS
    args = [x] + ([moe, gf] if has_moe else []) + [nw, shift, scale, w_main, w_gt, w_g,
                                                    b_gates.reshape(ng, 1), b_gates.reshape(1, ng)]
    in_specs = _prologue_specs(has_moe, ns, d, tm) + [
        _full((d, nmain)), _full((ng, d)), _full((d, ng)), _full((ng, 1)), _full((1, ng))]
    row = lambda n: pl.BlockSpec((tm, n), lambda b, s: (b * ns + s, 0))
    out_specs = [row(dqk), row(dqk), row(dv), row(dv),
                 pl.BlockSpec((None, ng, tm), lambda b, s: (b, 0, s)),
                 pl.BlockSpec((None, tm, ng), lambda b, s: (b, s, 0))]
    out_shape = [jax.ShapeDtypeStruct((t, dqk), BF16), jax.ShapeDtypeStruct((t, dqk), BF16),
                 jax.ShapeDtypeStruct((t, dv), BF16), jax.ShapeDtypeStruct((t, dv), BF16),
                 jax.ShapeDtypeStruct((bsz, ng, seq), F32), jax.ShapeDtypeStruct((bsz, seq, ng), F32)]
    if has_moe:
        out_specs = [row(d)] + out_specs
        out_shape = [jax.ShapeDtypeStruct((t, d), F32)] + out_shape
    res = pl.pallas_call(
        functools.partial(_mlstm_in_kernel, has_moe, dqk, dv),
        grid=(bsz, ns), in_specs=in_specs, out_specs=out_specs, out_shape=out_shape,
        compiler_params=_cp(("arbitrary", "arbitrary")), name="mlstm_in",
    )(*args)
    if has_moe:
        return res[0], res[1:]
    return x, res


def _mlstm_cell_kernel(q_ref, k_ref, v_ref, grow_ref, gcol_ref, hs_ref, cn_ref, m_ref):
    c = pl.program_id(1)
    L = q_ref.shape[0]
    dk = q_ref.shape[1] // A_HEADS
    dvh = v_ref.shape[1] // A_HEADS

    @pl.when(c == 0)
    def _():
        cn_ref[...] = jnp.zeros_like(cn_ref)
        m_ref[...] = jnp.zeros_like(m_ref)

    r = lax.broadcasted_iota(jnp.int32, (L, L), 0)
    cc = lax.broadcasted_iota(jnp.int32, (L, L), 1)
    causal = cc <= r
    lower = causal.astype(F32)
    upper = (r <= cc).astype(F32)
    grow = grow_ref[...]
    gcol = gcol_ref[...]
    b_row_all = _dot(grow[A_HEADS:, :], upper, HIGHEST)
    b_col_all = _dot(lower, gcol[:, A_HEADS:], HIGHEST)
    ones_col = (lax.broadcasted_iota(jnp.int32, (L, dvh), 1) == 0).astype(BF16)

    for h in range(A_HEADS):
        qh = q_ref[:, h * dk:(h + 1) * dk]
        kh = k_ref[:, h * dk:(h + 1) * dk]
        v_aug = jnp.concatenate([v_ref[:, h * dvh:(h + 1) * dvh], ones_col], axis=1)
        bc = b_col_all[:, h:h + 1]
        br = b_row_all[h:h + 1, :]
        igr = grow[h:h + 1, :]
        igc = gcol[:, h:h + 1]
        m_prev = m_ref[h:h + 1, 0:1]
        cn = cn_ref[h]

        d_intra = jnp.where(causal, bc - br + igr, -jnp.inf)
        d_inter = bc + m_prev
        m_t = jnp.maximum(d_inter, jnp.max(d_intra, axis=-1, keepdims=True))
        w = jnp.exp(d_intra - m_t) * _dot_nt(qh, kh)
        a = jnp.exp(d_inter - m_t)
        tot = _dot(w.astype(BF16), v_aug) + a * _dot(qh, cn.astype(BF16))
        num = tot[:, :dvh]
        den = tot[:, dvh:dvh + 1]
        hh = num / jnp.maximum(jnp.abs(den), jnp.exp(-m_t))
        hh = hh * lax.rsqrt(jnp.mean(hh * hh, axis=-1, keepdims=True) + EPS)
        hs_ref[:, h * dvh:(h + 1) * dvh] = hh.astype(hs_ref.dtype)

        b_last = bc[L - 1:L, :]
        g = b_last - bc + igc
        m_new = jnp.maximum(b_last + m_prev, jnp.max(g, axis=0, keepdims=True))
        wk = jnp.exp(g - m_new)
        decay = jnp.exp(b_last + m_prev - m_new)
        upd = _dot_tn(kh, (wk * v_aug.astype(F32)).astype(BF16))
        cn_ref[h] = decay * cn + upd
        m_ref[h:h + 1, :] = jnp.broadcast_to(m_new, (1, m_ref.shape[1]))


def mlstm_cell(q, k, v, grow, gcol, bsz, seq):
    t, dqk = q.shape
    dv = v.shape[1]
    L = min(A_L, seq)
    nc = seq // L
    ng = grow.shape[1]
    dk = dqk // A_HEADS
    dvh = dv // A_HEADS
    row = lambda n: pl.BlockSpec((L, n), lambda b, c: (b * nc + c, 0))
    return pl.pallas_call(
        _mlstm_cell_kernel,
        grid=(bsz, nc),
        in_specs=[row(dqk), row(dqk), row(dv),
                  pl.BlockSpec((None, ng, L), lambda b, c: (b, 0, c)),
                  pl.BlockSpec((None, L, ng), lambda b, c: (b, c, 0))],
        out_specs=row(dv),
        out_shape=jax.ShapeDtypeStruct((t, dv), BF16),
        scratch_shapes=[pltpu.VMEM((A_HEADS, dk, 2 * dvh), F32), pltpu.VMEM((A_HEADS, 128), F32)],
        compiler_params=_cp(("arbitrary", "arbitrary")), name="mlstm_cell",
    )(q, k, v, grow, gcol)


def _fox_q_kernel(has_moe, qscale, *refs):
    xin, h, rest = _prologue(has_moe, refs)
    w_ref = rest[0]
    outs = rest[1:]
    if has_moe:
        outs[0][...] = xin
        outs = outs[1:]
    outs[0][...] = (_dot(h.astype(BF16), w_ref[...]) * qscale).astype(BF16)


def fox_q(x, moe, gf, nw, shift, scale, w_q, bsz, seq):
    t, d = x.shape
    has_moe = moe is not None
    tm = min(TM, seq)
    ns = seq // tm
    args = [x] + ([moe, gf] if has_moe else []) + [nw, shift, scale, w_q.astype(BF16)]
    in_specs = _prologue_specs(has_moe, ns, d, tm) + [_full((d, d))]
    row = pl.BlockSpec((tm, d), lambda b, s: (b * ns + s, 0))
    out_specs = [row]
    out_shape = [jax.ShapeDtypeStruct((t, d), BF16)]
    if has_moe:
        out_specs = [row] + out_specs
        out_shape = [jax.ShapeDtypeStruct((t, d), F32)] + out_shape
    qscale = float(d // B_HEADS) ** -0.5
    res = pl.pallas_call(
        functools.partial(_fox_q_kernel, has_moe, qscale),
        grid=(bsz, ns), in_specs=in_specs, out_specs=out_specs, out_shape=out_shape,
        compiler_params=_cp(("arbitrary", "arbitrary")), name="fox_q",
    )(*args)
    if has_moe:
        return res[0], res[1]
    return x, res[0]


def _shared_kv_kernel(x_ref, nw_ref, sh_ref, sc_ref, w_ref, wft_ref, wf_ref, bcol_ref, brow_ref, up_ref, lo_ref,
                      k_ref, v_ref, frow_ref, fcol_ref, cr_ref, cc_ref):
    s = pl.program_id(1)
    d = x_ref.shape[1]

    @pl.when(s == 0)
    def _():
        cr_ref[...] = jnp.zeros_like(cr_ref)
        cc_ref[...] = jnp.zeros_like(cc_ref)

    h = _norm_mod(x_ref[...], nw_ref[...], sh_ref[...], sc_ref[...])
    hb = h.astype(BF16)
    k_ref[...] = _dot(hb, w_ref[:, 0:d]).astype(BF16)
    v_ref[...] = _dot(hb, w_ref[:, d:2 * d]).astype(BF16)
    lr = _log_sigmoid(_dot_nt(wft_ref[...], hb) + bcol_ref[...])
    frow_ref[...] = _dot(lr, up_ref[...], HIGHEST) + cr_ref[...]
    cr_ref[...] += jnp.sum(lr, axis=1, keepdims=True)
    lc = _log_sigmoid(_dot(hb, wf_ref[...]) + brow_ref[...])
    fcol_ref[...] = _dot(lo_ref[...], lc, HIGHEST) + cc_ref[...]
    cc_ref[...] += jnp.sum(lc, axis=0, keepdims=True)


def shared_kv(x, nw, shift, scale, w_kv, b_fgate, bsz, seq):
    t, d = x.shape
    tm = min(TM, seq)
    ns = seq // tm
    nh = B_HEADS
    w_main = w_kv[:, :2 * d].astype(BF16)
    w_f = w_kv[:, 2 * d:].astype(BF16)
    idx = jnp.arange(tm)
    upper = (idx[:, None] <= idx[None, :]).astype(F32)
    lower = upper.T
    row = pl.BlockSpec((tm, d), lambda b, s: (b * ns + s, 0))
    vec = pl.BlockSpec((None, 1, d), lambda b, s: (b, 0, 0))
    return pl.pallas_call(
        _shared_kv_kernel,
        grid=(bsz, ns),
        in_specs=[row, _full((1, d)), vec, vec, _full((d, 2 * d)), _full((nh, d)), _full((d, nh)),
                  _full((nh, 1)), _full((1, nh)), _full((tm, tm)), _full((tm, tm))],
        out_specs=[row, row, pl.BlockSpec((None, nh, tm), lambda b, s: (b, 0, s)),
                   pl.BlockSpec((None, tm, nh), lambda b, s: (b, s, 0))],
        out_shape=[jax.ShapeDtypeStruct((t, d), BF16), jax.ShapeDtypeStruct((t, d), BF16),
                   jax.ShapeDtypeStruct((bsz, nh, seq), F32), jax.ShapeDtypeStruct((bsz, seq, nh), F32)],
        scratch_shapes=[pltpu.VMEM((nh, 1), F32), pltpu.VMEM((1, nh), F32)],
        compiler_params=_cp(("arbitrary", "arbitrary")), name="shared_kv",
    )(x, nw, shift, scale, w_main, w_f.T, w_f, b_fgate.reshape(nh, 1), b_fgate.reshape(1, nh), upper, lower)


def _fox_attn_kernel(q_ref, k_ref, v_ref, fcol_ref, frow_ref, o_ref):
    hp = pl.program_id(1)
    i = pl.program_id(2)
    tq = q_ref.shape[0]
    dh2 = q_ref.shape[1]
    dh = dh2 // 2
    lane = lax.broadcasted_iota(jnp.int32, (1, dh2), 1)
    lo_mask = lane < dh
    q = q_ref[...]
    zero = jnp.zeros_like(q)
    qs = (jnp.where(lo_mask, q, zero), jnp.where(lo_mask, zero, q))
    fcol = fcol_ref[...]
    hidx = lax.broadcasted_iota(jnp.int32, fcol.shape, 1)
    fis = tuple(jnp.sum(jnp.where(hidx == 2 * hp + e, fcol, 0.0), axis=1, keepdims=True) for e in range(2))
    r = lax.broadcasted_iota(jnp.int32, (tq, tq), 0)
    cc = lax.broadcasted_iota(jnp.int32, (tq, tq), 1)
    causal = cc <= r

    def step(j, carry, masked):
        off = pl.multiple_of(j * tq, tq)
        kj = k_ref[pl.ds(off, tq), :]
        vj = v_ref[pl.ds(off, tq), :]
        zv = jnp.zeros_like(vj)
        vs = (jnp.where(lo_mask, vj, zv), jnp.where(lo_mask, zv, vj))
        out = []
        for e in range(2):
            m, l, acc = carry[e]
            fj = frow_ref[pl.ds(2 * hp + e, 1), pl.ds(off, tq)]
            s = _dot_nt(qs[e], kj) + fis[e] - fj
            if masked:
                s = jnp.where(causal, s, -jnp.inf)
            m_new = jnp.maximum(m, jnp.max(s, axis=-1, keepdims=True))
            p = jnp.exp(s - m_new)
            corr = jnp.exp(m - m_new)
            l = corr * l + jnp.sum(p, axis=-1, keepdims=True)
            acc = corr * acc + _dot(p.astype(BF16), vs[e])
            out.append((m_new, l, acc))
        return tuple(out)

    init = tuple((jnp.full((tq, 1), -1e30, F32), jnp.zeros((tq, 1), F32), jnp.zeros((tq, dh2), F32))
                 for _ in range(2))
    carry = lax.fori_loop(0, i, lambda j, c: step(j, c, False), init)
    carry = step(i, carry, True)
    o0 = carry[0][2] / carry[0][1]
    o1 = carry[1][2] / carry[1][1]
    o_ref[...] = jnp.where(lo_mask, o0, o1).astype(o_ref.dtype)


def fox_attn(q, k, v, fcol, frow, bsz, seq):
    t, d = q.shape
    nh = B_HEADS
    dh2 = 2 * (d // nh)
    tq = min(TQ, seq)
    nq = seq // tq
    return pl.pallas_call(
        _fox_attn_kernel,
        grid=(bsz, nh // 2, nq),
        in_specs=[pl.BlockSpec((tq, dh2), lambda b, hp, i: (b * nq + i, hp)),
                  pl.BlockSpec((seq, dh2), lambda b, hp, i: (b, hp)),
                  pl.BlockSpec((seq, dh2), lambda b, hp, i: (b, hp)),
                  pl.BlockSpec((None, tq, nh), lambda b, hp, i: (b, i, 0)),
                  pl.BlockSpec((None, nh, seq), lambda b, hp, i: (b, 0, 0))],
        out_specs=pl.BlockSpec((tq, dh2), lambda b, hp, i: (b * nq + i, hp)),
        out_shape=jax.ShapeDtypeStruct((t, d), BF16),
        compiler_params=_cp(("arbitrary", "arbitrary", "arbitrary")), name="fox_attn",
    )(q, k, v, fcol, frow)


def _route(logits, bias_col):
    aff = _sigmoid(logits)
    sel = aff + bias_col
    s = [sel[e:e + 1, :] for e in range(N_EXPERTS)]
    a = [aff[e:e + 1, :] for e in range(N_EXPERTS)]
    gs = []
    for g in range(N_GROUPS):
        v = s[EPG * g:EPG * (g + 1)]
        best = v[0] + v[1]
        for i in range(EPG):
            for j in range(i + 1, EPG):
                if (i, j) != (0, 1):
                    best = jnp.maximum(best, v[i] + v[j])
        gs.append(best)
    grp = jnp.zeros_like(gs[0], dtype=jnp.int32)
    best = gs[0]
    for g in range(1, N_GROUPS):
        upd = gs[g] > best
        grp = jnp.where(upd, g, grp)
        best = jnp.where(upd, gs[g], best)

    def pick(arrs, j):
        out = arrs[(N_GROUPS - 1) * EPG + j]
        for g in range(N_GROUPS - 2, -1, -1):
            out = jnp.where(grp == g, arrs[g * EPG + j], out)
        return out

    v = [pick(s, j) for j in range(EPG)]
    w = [pick(a, j) for j in range(EPG)]
    i1 = jnp.zeros_like(grp)
    b1 = v[0]
    for j in range(1, EPG):
        upd = v[j] > b1
        i1 = jnp.where(upd, j, i1)
        b1 = jnp.where(upd, v[j], b1)
    i2 = jnp.full_like(grp, -1)
    b2 = jnp.full_like(b1, -jnp.inf)
    for j in range(EPG):
        upd = (i1 != j) & ((i2 < 0) | (v[j] > b2))
        i2 = jnp.where(upd, j, i2)
        b2 = jnp.where(upd, v[j], b2)

    def take(arrs, idx):
        out = arrs[EPG - 1]
        for j in range(EPG - 2, -1, -1):
            out = jnp.where(idx == j, arrs[j], out)
        return out

    w1 = take(w, i1)
    w2 = take(w, i2)
    tot = w1 + w2
    g1 = w1 / tot
    g2 = w2 / tot
    first_low = i1 < i2
    lo = jnp.where(first_low, i1, i2)
    hi = jnp.where(first_low, i2, i1)
    g_lo = jnp.where(first_low, g1, g2)
    g_hi = jnp.where(first_low, g2, g1)
    base = jnp.where(lo == 0, 0, jnp.where(lo == 1, 3, 5))
    cls = grp * N_PAIRS + base + (hi - lo - 1)
    return cls, g_lo, g_hi


def _post_mix_kernel(is_mlstm, *refs):
    if is_mlstm:
        hs_ref, og_ref, mh_ref = refs[:3]
        refs = refs[3:]
        og = og_ref[...].astype(F32)
        mix = hs_ref[...].astype(F32) * mh_ref[...] * _sigmoid(og)
    else:
        mix = refs[0][...]
        refs = refs[1:]
    (w_ref, x_ref, gm_ref, nw_ref, sh_ref, sc_ref, wrt_ref, rb_ref, us_ref,
     xnew_ref, h2_ref, cls_ref, glo_ref, ghi_ref, rank_ref, cnt_ref) = refs
    first = (pl.program_id(0) == 0) & (pl.program_id(1) == 0)

    @pl.when(first)
    def _():
        cnt_ref[...] = jnp.zeros_like(cnt_ref)

    y = _dot(mix.astype(BF16), w_ref[...])
    xnew = x_ref[...] + gm_ref[...] * y
    xnew_ref[...] = xnew
    h2 = _norm_mod(xnew, nw_ref[...], sh_ref[...], sc_ref[...])
    h2_ref[...] = h2
    logits = _dot_nt(wrt_ref[...], h2, HIGHEST)
    cls, g_lo, g_hi = _route(logits, rb_ref[...])
    cls_ref[...] = cls
    glo_ref[...] = g_lo
    ghi_ref[...] = g_hi
    tm = cls.shape[1]
    onehot = (lax.broadcasted_iota(jnp.int32, (CLS_PAD, tm), 0) == cls).astype(F32)
    prefix = _dot(onehot.astype(BF16), us_ref[...])
    carry = cnt_ref[:, 0:1]
    rank = jnp.sum(onehot * (prefix + carry), axis=0, keepdims=True)
    rank_ref[...] = rank.astype(jnp.int32)
    cnt_ref[...] += jnp.sum(onehot, axis=1, keepdims=True)


def post_mix(mix_args, is_mlstm, w_o, x, gm, nw, shift, scale, w_router, router_bias, bsz, seq):
    t, d = x.shape
    tm = min(TM, seq)
    ns = seq // tm
    ne = N_EXPERTS
    idx = jnp.arange(tm)
    upper_strict = (idx[:, None] < idx[None, :]).astype(BF16)
    row = pl.BlockSpec((tm, d), lambda b, s: (b * ns + s, 0))
    vec = pl.BlockSpec((None, 1, d), lambda b, s: (b, 0, 0))
    lane_row = pl.BlockSpec((None, 1, tm), lambda b, s: (b, 0, s))
    if is_mlstm:
        mix_specs = [row, row, _full((1, d))]
    else:
        mix_specs = [row]
    in_specs = mix_specs + [_full((d, d)), row, vec, _full((1, d)), vec, vec, _full((ne, d)), _full((ne, 1)),
                            _full((tm, tm))]
    out_specs = [row, row, lane_row, lane_row, lane_row, lane_row, _full((CLS_PAD, 128))]
    out_shape = [jax.ShapeDtypeStruct((t, d), F32), jax.ShapeDtypeStruct((t, d), F32),
                 jax.ShapeDtypeStruct((bsz, 1, seq), jnp.int32), jax.ShapeDtypeStruct((bsz, 1, seq), F32),
                 jax.ShapeDtypeStruct((bsz, 1, seq), F32), jax.ShapeDtypeStruct((bsz, 1, seq), jnp.int32),
                 jax.ShapeDtypeStruct((CLS_PAD, 128), F32)]
    return pl.pallas_call(
        functools.partial(_post_mix_kernel, is_mlstm),
        grid=(bsz, ns), in_specs=in_specs, out_specs=out_specs, out_shape=out_shape,
        compiler_params=_cp(("arbitrary", "arbitrary")), name="post_mix",
    )(*mix_args, w_o.astype(BF16), x, gm, nw, shift, scale, w_router.T, router_bias.reshape(ne, 1), upper_strict)


def _row_scatter_kernel(rb, dest_ref, src_ref, dst_in_ref, dst_ref, sem):
    del dst_in_ref
    base = pl.program_id(0) * rb

    def start(rr, carry):
        tok = base + rr
        pltpu.make_async_copy(src_ref.at[pl.ds(tok, 1)], dst_ref.at[pl.ds(dest_ref[tok], 1)], sem).start()
        return carry

    def wait(rr, carry):
        pltpu.make_async_copy(src_ref.at[pl.ds(0, 1)], dst_ref.at[pl.ds(0, 1)], sem).wait()
        return carry

    lax.fori_loop(0, rb, start, 0, unroll=8)
    lax.fori_loop(0, rb, wait, 0, unroll=8)


def row_scatter(dest, src, n_rows):
    t, d = src.shape
    rb = min(RB, t)
    dst0 = jnp.zeros((n_rows, d), src.dtype)
    return pl.pallas_call(
        functools.partial(_row_scatter_kernel, rb),
        grid_spec=pltpu.PrefetchScalarGridSpec(
            num_scalar_prefetch=1, grid=(t // rb,),
            in_specs=[pl.BlockSpec(memory_space=pl.ANY), pl.BlockSpec(memory_space=pl.ANY)],
            out_specs=pl.BlockSpec(memory_space=pl.ANY),
            scratch_shapes=[pltpu.SemaphoreType.DMA]),
        out_shape=jax.ShapeDtypeStruct((n_rows, d), src.dtype),
        input_output_aliases={2: 0},
        compiler_params=pltpu.CompilerParams(dimension_semantics=("arbitrary",)), name="row_scatter",
    )(dest, src, dst0)


def _row_gather_kernel(rb, dest_ref, src_ref, dst_ref, sem):
    base = pl.program_id(0) * rb

    def start(rr, carry):
        tok = base + rr
        pltpu.make_async_copy(src_ref.at[pl.ds(dest_ref[tok], 1)], dst_ref.at[pl.ds(tok, 1)], sem).start()
        return carry

    def wait(rr, carry):
        pltpu.make_async_copy(src_ref.at[pl.ds(0, 1)], dst_ref.at[pl.ds(0, 1)], sem).wait()
        return carry

    lax.fori_loop(0, rb, start, 0, unroll=8)
    lax.fori_loop(0, rb, wait, 0, unroll=8)


def row_gather(dest, src, t):
    d = src.shape[1]
    rb = min(RB, t)
    return pl.pallas_call(
        functools.partial(_row_gather_kernel, rb),
        grid_spec=pltpu.PrefetchScalarGridSpec(
            num_scalar_prefetch=1, grid=(t // rb,),
            in_specs=[pl.BlockSpec(memory_space=pl.ANY)],
            out_specs=pl.BlockSpec(memory_space=pl.ANY),
            scratch_shapes=[pltpu.SemaphoreType.DMA]),
        out_shape=jax.ShapeDtypeStruct((t, d), src.dtype),
        compiler_params=pltpu.CompilerParams(dimension_semantics=("arbitrary",)), name="row_gather",
    )(dest, src)


def _lane_to_col(g_row):
    n = g_row.shape[1]
    eye = lax.broadcasted_iota(jnp.int32, (n, n), 0) == lax.broadcasted_iota(jnp.int32, (n, n), 1)
    return jnp.sum(jnp.where(eye, g_row, 0.0), axis=1, keepdims=True)


def _experts_kernel(elo_ref, ehi_ref, nused_ref, x_ref, glo_ref, ghi_ref,
                    wg_lo, wu_lo, wd_lo, wg_hi, wu_hi, wd_hi, y_ref):
    del elo_ref, ehi_ref
    i = pl.program_id(0)

    @pl.when(i < nused_ref[0])
    def _():
        xb = x_ref[...].astype(BF16)
        acc = None
        for g_ref, wg, wu, wd in ((glo_ref, wg_lo, wu_lo, wd_lo), (ghi_ref, wg_hi, wu_hi, wd_hi)):
            hg = _dot(xb, wg[...])
            hu = _dot(xb, wu[...])
            act = hg * _sigmoid(hg) * hu * _lane_to_col(g_ref[...])
            part = _dot(act.astype(BF16), wd[...])
            acc = part if acc is None else acc + part
        y_ref[...] = acc

    @pl.when(i >= nused_ref[0])
    def _():
        y_ref[...] = jnp.zeros_like(y_ref)


def experts(xs, glo_s, ghi_s, blk_lo, blk_hi, nused, w_gate, w_up, w_down):
    p, d = xs.shape
    nblk = p // MB
    de = w_gate.shape[2]
    xrow = pl.BlockSpec((MB, d), lambda i, lo, hi, nu: (i, 0))
    grow = pl.BlockSpec((None, 1, MB), lambda i, lo, hi, nu: (i, 0, 0))
    w_in_lo = pl.BlockSpec((None, d, de), lambda i, lo, hi, nu: (lo[i], 0, 0))
    w_in_hi = pl.BlockSpec((None, d, de), lambda i, lo, hi, nu: (hi[i], 0, 0))
    w_out_lo = pl.BlockSpec((None, de, d), lambda i, lo, hi, nu: (lo[i], 0, 0))
    w_out_hi = pl.BlockSpec((None, de, d), lambda i, lo, hi, nu: (hi[i], 0, 0))
    return pl.pallas_call(
        _experts_kernel,
        grid_spec=pltpu.PrefetchScalarGridSpec(
            num_scalar_prefetch=3, grid=(nblk,),
            in_specs=[xrow, grow, grow, w_in_lo, w_in_lo, w_out_lo, w_in_hi, w_in_hi, w_out_hi],
            out_specs=xrow),
        out_shape=jax.ShapeDtypeStruct((p, d), F32),
        compiler_params=_cp(("arbitrary",)), name="experts",
    )(blk_lo, blk_hi, nused, xs, glo_s.reshape(nblk, 1, MB), ghi_s.reshape(nblk, 1, MB),
      w_gate, w_up, w_down, w_gate, w_up, w_down)


_PAIR_LO = (0, 0, 0, 1, 1, 2)
_PAIR_HI = (1, 2, 3, 2, 3, 3)


def moe(h2, cls, g_lo, g_hi, rank, counts, w_gate, w_up, w_down):
    t, d = h2.shape
    p = t + N_CLASSES * MB
    nblk = p // MB
    cls = cls.reshape(t)
    counts = counts[:N_CLASSES, 0].astype(jnp.int32)
    padded = (counts + MB - 1) // MB * MB
    pad_end = jnp.cumsum(padded)
    pad_start = pad_end - padded
    dest = (pad_start[cls] + rank.reshape(t)).astype(jnp.int32)
    blk_cls = jnp.clip(jnp.searchsorted(pad_end, jnp.arange(nblk, dtype=jnp.int32) * MB, side='right'),
                       0, N_CLASSES - 1).astype(jnp.int32)
    grp = blk_cls // N_PAIRS
    pr = blk_cls % N_PAIRS
    blk_lo = grp * EPG + jnp.asarray(_PAIR_LO, jnp.int32)[pr]
    blk_hi = grp * EPG + jnp.asarray(_PAIR_HI, jnp.int32)[pr]
    nused = (pad_end[-1:] // MB).astype(jnp.int32)
    glo_s = jnp.zeros((p,), F32).at[dest].set(g_lo.reshape(t))
    ghi_s = jnp.zeros((p,), F32).at[dest].set(g_hi.reshape(t))
    xs = row_scatter(dest, h2, p)
    ys = experts(xs, glo_s, ghi_s, blk_lo, blk_hi, nused, w_gate, w_up, w_down)
    return row_gather(dest, ys, t)


def _final_kernel(*refs):
    _, h, rest = _prologue(True, refs)
    rest[0][...] = h


def final_norm(x, moe_out, gf, nw, shift, scale, bsz, seq):
    t, d = x.shape
    tm = min(TM, seq)
    ns = seq // tm
    row = pl.BlockSpec((tm, d), lambda b, s: (b * ns + s, 0))
    return pl.pallas_call(
        _final_kernel, grid=(bsz, ns), in_specs=_prologue_specs(True, ns, d, tm), out_specs=row,
        out_shape=jax.ShapeDtypeStruct((t, d), F32),
        compiler_params=_cp(("arbitrary", "arbitrary")), name="final_norm",
    )(x, moe_out, gf, nw, shift, scale)


def kernel(x, c, a_w_in, a_b_gates, a_mh_norm, a_w_out, kv_norm, w_ada_kv, b_ada_kv, w_kv, b_fgate, b_w_q, b_w_o,
           norm_mix, norm_ffn, w_ada, b_ada, w_router, router_bias, w_gate, w_up, w_down, norm_final, w_ada_final,
           b_ada_final):
    bsz, seq, d = x.shape
    depth = w_ada.shape[0]
    n_a = a_w_in.shape[0]
    t = bsz * seq
    xf = x.reshape(t, d)

    mods = ada(c, w_ada, b_ada)
    kv_mod = ada(c, w_ada_kv[None], b_ada_kv[None])[0]
    fin = ada(c, w_ada_final[None], b_ada_final[None])[0]

    def vecs(m, n):
        return [m[:, None, i * d:(i + 1) * d] for i in range(n)]

    wg_b = w_gate.astype(BF16)
    wu_b = w_up.astype(BF16)
    wd_b = w_down.astype(BF16)

    moe_out = None
    gf_prev = None
    k_sh = v_sh = frow = fcol = None
    for layer in range(depth):
        sh_m, sc_m, g_m, sh_f, sc_f, g_f = vecs(mods[layer], 6)
        nm = norm_mix[layer].reshape(1, d)
        if layer < n_a:
            xf, (q, k, v, o, grow, gcol) = mlstm_in(xf, moe_out, gf_prev, nm, sh_m, sc_m, a_w_in[layer],
                                                     a_b_gates[layer], bsz, seq)
            hs = mlstm_cell(q, k, v, grow, gcol, bsz, seq)
            mix_args = (hs, o, a_mh_norm[layer].reshape(1, d))
            w_o = a_w_out[layer]
        else:
            j = layer - n_a
            xf, q = fox_q(xf, moe_out, gf_prev, nm, sh_m, sc_m, b_w_q[j], bsz, seq)
            if layer == n_a:
                kv_sh, kv_sc = vecs(kv_mod, 2)
                k_sh, v_sh, frow, fcol = shared_kv(xf, kv_norm.reshape(1, d), kv_sh, kv_sc, w_kv, b_fgate, bsz, seq)
            att = fox_attn(q, k_sh, v_sh, fcol, frow, bsz, seq)
            mix_args = (att,)
            w_o = b_w_o[j]
        xf, h2, cls, g_lo, g_hi, rank, counts = post_mix(
            mix_args, layer < n_a, w_o, xf, g_m, norm_ffn[layer].reshape(1, d), sh_f, sc_f, w_router, router_bias,
            bsz, seq)
        moe_out = moe(h2, cls, g_lo, g_hi, rank, counts, wg_b[layer], wu_b[layer], wd_b[layer])
        gf_prev = g_f
    fin_sh, fin_sc = vecs(fin, 2)
    out = final_norm(xf, moe_out, gf_prev, norm_final.reshape(1, d), fin_sh, fin_sc, bsz, seq)
    return out.reshape(bsz, seq, d)
```

```python
import functools

import jax
import jax.numpy as jnp
from jax import lax
from jax.experimental import pallas as pl
from jax.experimental.pallas import tpu as pltpu

F32 = jnp.float32
BF16 = jnp.bfloat16
EPS = 1e-6
GATE_SOFTCAP = 15.0

A_HEADS = 8
B_HEADS = 16
N_EXPERTS = 16
N_GROUPS = 4
EPG = N_EXPERTS // N_GROUPS
N_PAIRS = 6
N_CLASSES = N_GROUPS * N_PAIRS
CLS_PAD = 32

VMEM_LIMIT = 56 * 1024 * 1024

TM = 512
A_L = 128
TQ = 1024
KT_PER_Q = 1
MB = 256
RB = 1024
LANES = 128
LOG2E = 1.4426950408889634

HIGHEST = lax.Precision.HIGHEST


def _cp(sem):
    return pltpu.CompilerParams(dimension_semantics=sem, vmem_limit_bytes=VMEM_LIMIT)


def _dot(a, b, precision=None):
    return jnp.dot(a, b, preferred_element_type=F32, precision=precision)


def _dot_nt(a, b, precision=None):
    return lax.dot_general(a, b, (((1,), (1,)), ((), ())), preferred_element_type=F32, precision=precision)


def _dot_tn(a, b, precision=None):
    return lax.dot_general(a, b, (((0,), (0,)), ((), ())), preferred_element_type=F32, precision=precision)


def _norm_mod(xin, nw, shift, scale):
    ms = jnp.mean(xin * xin, axis=-1, keepdims=True)
    y = xin * lax.rsqrt(ms + EPS)
    return (y * nw) * (1.0 + scale) + shift


def _log_sigmoid(x):
    return jnp.minimum(x, 0.0) - jnp.log1p(jnp.exp(-jnp.abs(x)))


def _sigmoid(x):
    return 1.0 / (1.0 + jnp.exp(-x))


def _ada_kernel(c_ref, w_ref, b_ref, o_ref):
    c = c_ref[...]
    ca = c * _sigmoid(c)
    o_ref[...] = _dot(ca, w_ref[...], HIGHEST) + b_ref[...]


def ada(c, w, b):
    nl, d, n = w.shape
    bsz = c.shape[0]
    tn = 512
    return pl.pallas_call(
        _ada_kernel,
        grid=(nl, n // tn),
        in_specs=[
            pl.BlockSpec((bsz, d), lambda l, j: (0, 0)),
            pl.BlockSpec((None, d, tn), lambda l, j: (l, 0, j)),
            pl.BlockSpec((None, 1, tn), lambda l, j: (l, 0, j)),
        ],
        out_specs=pl.BlockSpec((None, bsz, tn), lambda l, j: (l, 0, j)),
        out_shape=jax.ShapeDtypeStruct((nl, bsz, n), F32),
        compiler_params=_cp(("arbitrary", "arbitrary")),
        name="ada",
    )(c, w, b.reshape(nl, 1, n))


def _prologue(has_moe, refs):
    if has_moe:
        x_ref, moe_ref, gf_ref, nw_ref, sh_ref, sc_ref = refs[:6]
        rest = refs[6:]
        xin = x_ref[...] + gf_ref[...] * moe_ref[...].reshape(x_ref.shape)
    else:
        x_ref, nw_ref, sh_ref, sc_ref = refs[:4]
        rest = refs[4:]
        xin = x_ref[...]
    h = _norm_mod(xin, nw_ref[...], sh_ref[...], sc_ref[...])
    return xin, h, rest


def _prologue_specs(has_moe, ns, d, tm):
    row = pl.BlockSpec((tm, d), lambda b, s: (b * ns + s, 0))
    vec = pl.BlockSpec((None, 1, d), lambda b, s: (b, 0, 0))
    one = pl.BlockSpec((1, d), lambda b, s: (0, 0))
    if has_moe:
        row3 = pl.BlockSpec((tm, d // LANES, LANES), lambda b, s: (b * ns + s, 0, 0))
        return [row, row3, vec, one, vec, vec]
    return [row, one, vec, vec]


def _full(shape):
    nd = len(shape)
    return pl.BlockSpec(shape, lambda b, s: (0,) * nd)


def _mlstm_in_kernel(has_moe, dqk, dv, *refs):
    xin, h, rest = _prologue(has_moe, refs)
    w_ref, wgt_ref, wg_ref, bcol_ref, brow_ref = rest[:5]
    outs = rest[5:]
    if has_moe:
        xnew_ref = outs[0]
        outs = outs[1:]
        xnew_ref[...] = xin
    q_ref, k_ref, v_ref, o_ref, grow_ref, gcol_ref = outs
    hb = h.astype(BF16)
    scale = float(dqk // A_HEADS) ** -0.5
    q_ref[...] = (_dot(hb, w_ref[:, 0:dqk]) * scale).astype(BF16)
    k_ref[...] = _dot(hb, w_ref[:, dqk:2 * dqk]).astype(BF16)
    v_ref[...] = _dot(hb, w_ref[:, 2 * dqk:2 * dqk + dv]).astype(BF16)
    o_ref[...] = _dot(hb, w_ref[:, 2 * dqk + dv:2 * dqk + 2 * dv]).astype(BF16)
    gr = _dot_nt(wgt_ref[...], hb) + bcol_ref[...]
    gr = GATE_SOFTCAP * jnp.tanh(gr / GATE_SOFTCAP)
    ridx = lax.broadcasted_iota(jnp.int32, gr.shape, 0)
    grow_ref[...] = jnp.where(ridx < A_HEADS, gr, _log_sigmoid(gr))
    gc = _dot(hb, wg_ref[...]) + brow_ref[...]
    gc = GATE_SOFTCAP * jnp.tanh(gc / GATE_SOFTCAP)
    cidx = lax.broadcasted_iota(jnp.int32, gc.shape, 1)
    gcol_ref[...] = jnp.where(cidx < A_HEADS, gc, _log_sigmoid(gc))


def mlstm_in(x, moe, gf, nw, shift, scale, w_in, b_gates, bsz, seq):
    t, d = x.shape
    dqk, dv = d // 2, d
    has_moe = moe is not None
    tm = min(TM, seq)
    ns = seq // tm
    nmain = 2 * dqk + 2 * dv
    w_main = w_in[:, :nmain].astype(BF16)
    w_g = w_in[:, nmain:].astype(BF16)
    w_gt = w_g.T
    ng = 2 * A_HEADS
    args = [x] + ([moe, gf] if has_moe else []) + [nw, shift, scale, w_main, w_gt, w_g,
                                                    b_gates.reshape(ng, 1), b_gates.reshape(1, ng)]
    in_specs = _prologue_specs(has_moe, ns, d, tm) + [
        _full((d, nmain)), _full((ng, d)), _full((d, ng)), _full((ng, 1)), _full((1, ng))]
    row = lambda n: pl.BlockSpec((tm, n), lambda b, s: (b * ns + s, 0))
    out_specs = [row(dqk), row(dqk), row(dv), row(dv),
                 pl.BlockSpec((None, ng, tm), lambda b, s: (b, 0, s)),
                 pl.BlockSpec((None, tm, ng), lambda b, s: (b, s, 0))]
    out_shape = [jax.ShapeDtypeStruct((t, dqk), BF16), jax.ShapeDtypeStruct((t, dqk), BF16),
                 jax.ShapeDtypeStruct((t, dv), BF16), jax.ShapeDtypeStruct((t, dv), BF16),
                 jax.ShapeDtypeStruct((bsz, ng, seq), F32), jax.ShapeDtypeStruct((bsz, seq, ng), F32)]
    if has_moe:
        out_specs = [row(d)] + out_specs
        out_shape = [jax.ShapeDtypeStruct((t, d), F32)] + out_shape
    res = pl.pallas_call(
        functools.partial(_mlstm_in_kernel, has_moe, dqk, dv),
        grid=(bsz, ns), in_specs=in_specs, out_specs=out_specs, out_shape=out_shape,
        compiler_params=_cp(("arbitrary", "arbitrary")), name="mlstm_in",
    )(*args)
    if has_moe:
        return res[0], res[1:]
    return x, res


def _mlstm_cell_kernel(q_ref, k_ref, v_ref, grow_ref, gcol_ref, hs_ref, cn_ref, m_ref):
    c = pl.program_id(1)
    L = q_ref.shape[0]
    dk = q_ref.shape[1] // A_HEADS
    dvh = v_ref.shape[1] // A_HEADS

    @pl.when(c == 0)
    def _():
        cn_ref[...] = jnp.zeros_like(cn_ref)
        m_ref[...] = jnp.zeros_like(m_ref)

    r = lax.broadcasted_iota(jnp.int32, (L, L), 0)
    cc = lax.broadcasted_iota(jnp.int32, (L, L), 1)
    causal = cc <= r
    lower = causal.astype(F32)
    upper = (r <= cc).astype(F32)
    grow = grow_ref[...]
    gcol = gcol_ref[...]
    b_row_all = _dot(grow[A_HEADS:, :], upper, HIGHEST)
    b_col_all = _dot(lower, gcol[:, A_HEADS:], HIGHEST)
    ones_col = (lax.broadcasted_iota(jnp.int32, (L, dvh), 1) == 0).astype(BF16)

    for h in range(A_HEADS):
        qh = q_ref[:, h * dk:(h + 1) * dk]
        kh = k_ref[:, h * dk:(h + 1) * dk]
        v_aug = jnp.concatenate([v_ref[:, h * dvh:(h + 1) * dvh], ones_col], axis=1)
        bc = b_col_all[:, h:h + 1]
        br = b_row_all[h:h + 1, :]
        igr = grow[h:h + 1, :]
        igc = gcol[:, h:h + 1]
        m_prev = m_ref[h:h + 1, 0:1]
        cn = cn_ref[h]

        d_intra = jnp.where(causal, bc - br + igr, -jnp.inf)
        d_inter = bc + m_prev
        m_t = jnp.maximum(d_inter, jnp.max(d_intra, axis=-1, keepdims=True))
        w = jnp.exp(d_intra - m_t) * _dot_nt(qh, kh)
        a = jnp.exp(d_inter - m_t)
        tot = _dot(w.astype(BF16), v_aug) + a * _dot(qh, cn.astype(BF16))
        num = tot[:, :dvh]
        den = tot[:, dvh:dvh + 1]
        hh = num / jnp.maximum(jnp.abs(den), jnp.exp(-m_t))
        hh = hh * lax.rsqrt(jnp.mean(hh * hh, axis=-1, keepdims=True) + EPS)
        hs_ref[:, h * dvh:(h + 1) * dvh] = hh.astype(hs_ref.dtype)

        b_last = bc[L - 1:L, :]
        g = b_last - bc + igc
        m_new = jnp.maximum(b_last + m_prev, jnp.max(g, axis=0, keepdims=True))
        wk = jnp.exp(g - m_new)
        decay = jnp.exp(b_last + m_prev - m_new)
        upd = _dot_tn(kh, (wk * v_aug.astype(F32)).astype(BF16))
        cn_ref[h] = decay * cn + upd
        m_ref[h:h + 1, :] = jnp.broadcast_to(m_new, (1, m_ref.shape[1]))


def mlstm_cell(q, k, v, grow, gcol, bsz, seq):
    t, dqk = q.shape
    dv = v.shape[1]
    L = min(A_L, seq)
    nc = seq // L
    ng = grow.shape[1]
    dk = dqk // A_HEADS
    dvh = dv // A_HEADS
    row = lambda n: pl.BlockSpec((L, n), lambda b, c: (b * nc + c, 0))
    return pl.pallas_call(
        _mlstm_cell_kernel,
        grid=(bsz, nc),
        in_specs=[row(dqk), row(dqk), row(dv),
                  pl.BlockSpec((None, ng, L), lambda b, c: (b, 0, c)),
                  pl.BlockSpec((None, L, ng), lambda b, c: (b, c, 0))],
        out_specs=row(dv),
        out_shape=jax.ShapeDtypeStruct((t, dv), BF16),
        scratch_shapes=[pltpu.VMEM((A_HEADS, dk, 2 * dvh), F32), pltpu.VMEM((A_HEADS, 128), F32)],
        compiler_params=_cp(("arbitrary", "arbitrary")), name="mlstm_cell",
    )(q, k, v, grow, gcol)


def _fox_q_kernel(has_moe, qscale, *refs):
    xin, h, rest = _prologue(has_moe, refs)
    w_ref = rest[0]
    outs = rest[1:]
    if has_moe:
        outs[0][...] = xin
        outs = outs[1:]
    outs[0][...] = (_dot(h.astype(BF16), w_ref[...]) * qscale).astype(BF16)


def fox_q(x, moe, gf, nw, shift, scale, w_q, bsz, seq):
    t, d = x.shape
    has_moe = moe is not None
    tm = min(TM, seq)
    ns = seq // tm
    args = [x] + ([moe, gf] if has_moe else []) + [nw, shift, scale, w_q.astype(BF16)]
    in_specs = _prologue_specs(has_moe, ns, d, tm) + [_full((d, d))]
    row = pl.BlockSpec((tm, d), lambda b, s: (b * ns + s, 0))
    out_specs = [row]
    out_shape = [jax.ShapeDtypeStruct((t, d), BF16)]
    if has_moe:
        out_specs = [row] + out_specs
        out_shape = [jax.ShapeDtypeStruct((t, d), F32)] + out_shape
    qscale = float(d // B_HEADS) ** -0.5 * LOG2E
    res = pl.pallas_call(
        functools.partial(_fox_q_kernel, has_moe, qscale),
        grid=(bsz, ns), in_specs=in_specs, out_specs=out_specs, out_shape=out_shape,
        compiler_params=_cp(("arbitrary", "arbitrary")), name="fox_q",
    )(*args)
    if has_moe:
        return res[0], res[1]
    return x, res[0]


N_BIAS_PIECES = 3


def _bias_select_matrices(d, n_heads):
    import numpy as np
    dh = d // n_heads
    pq = np.zeros((N_BIAS_PIECES, n_heads, d), np.float32)
    pk = np.zeros((N_BIAS_PIECES, n_heads, d), np.float32)
    oq = np.zeros((1, d), np.float32)
    ok = np.zeros((1, d), np.float32)
    for h in range(n_heads):
        base = (h // 2) * 2 * dh + (dh if h % 2 == 0 else 0)
        for p in range(N_BIAS_PIECES):
            pq[p, h, base + p] = 1.0
            ok[0, base + p] = 1.0
            oq[0, base + N_BIAS_PIECES + p] = 1.0
            pk[p, h, base + N_BIAS_PIECES + p] = -1.0
    return pq, pk, oq, ok


def _split3(f):
    a = f.astype(BF16)
    r1 = f - a.astype(F32)
    b = r1.astype(BF16)
    c = (r1 - b.astype(F32)).astype(BF16)
    return a, b, c


def _shared_kv_kernel(x_ref, nw_ref, sh_ref, sc_ref, w_ref, wvt_ref, wf_ref, brow_ref, lo_ref, pq_ref, pk_ref, oq_ref,
                      ok_ref, ke_ref, ko_ref, vt_ref, eq_ref, cc_ref):
    s = pl.program_id(1)
    d = x_ref.shape[1]
    dh = d // B_HEADS

    @pl.when(s == 0)
    def _():
        cc_ref[...] = jnp.zeros_like(cc_ref)

    h = _norm_mod(x_ref[...], nw_ref[...], sh_ref[...], sc_ref[...])
    hb = h.astype(BF16)
    lc = _log_sigmoid(_dot(hb, wf_ref[...]) + brow_ref[...])
    fcum = _dot(lo_ref[...], lc, HIGHEST) + cc_ref[...]
    cc_ref[...] += jnp.sum(lc, axis=0, keepdims=True)
    pieces = _split3(fcum * LOG2E)
    eq = oq_ref[...]
    ek = ok_ref[...]
    for p in range(N_BIAS_PIECES):
        eq = eq + _dot(pieces[p], pq_ref[p])
        ek = ek + _dot(pieces[p], pk_ref[p])
    eq_ref[...] = eq.astype(BF16)
    even = (lax.broadcasted_iota(jnp.int32, (1, d), 1) % (2 * dh)) < dh
    k = _dot(hb, w_ref[...])
    ke_ref[...] = jnp.where(even, k, ek).astype(BF16)
    ko_ref[...] = jnp.where(even, ek, k).astype(BF16)
    vt_ref[...] = _dot_nt(wvt_ref[...], hb).astype(BF16)


def shared_kv(x, nw, shift, scale, w_kv, b_fgate, bsz, seq):
    t, d = x.shape
    tm = min(TM, seq)
    ns = seq // tm
    nh = B_HEADS
    w_k = w_kv[:, :d].astype(BF16)
    w_vt = w_kv[:, d:2 * d].astype(BF16).T
    w_f = w_kv[:, 2 * d:].astype(BF16)
    idx = jnp.arange(tm)
    lower = (idx[None, :] <= idx[:, None]).astype(F32)
    pq, pk, oq, ok = _bias_select_matrices(d, nh)
    row = pl.BlockSpec((tm, d), lambda b, s: (b * ns + s, 0))
    vec = pl.BlockSpec((None, 1, d), lambda b, s: (b, 0, 0))
    sel = _full((N_BIAS_PIECES, nh, d))
    return pl.pallas_call(
        _shared_kv_kernel,
        grid=(bsz, ns),
        in_specs=[row, _full((1, d)), vec, vec, _full((d, d)), _full((d, d)), _full((d, nh)), _full((1, nh)),
                  _full((tm, tm)), sel, sel, _full((1, d)), _full((1, d))],
        out_specs=[row, row, pl.BlockSpec((d, tm), lambda b, s: (0, b * ns + s)), row],
        out_shape=[jax.ShapeDtypeStruct((t, d), BF16), jax.ShapeDtypeStruct((t, d), BF16),
                   jax.ShapeDtypeStruct((d, t), BF16), jax.ShapeDtypeStruct((t, d), BF16)],
        scratch_shapes=[pltpu.VMEM((1, nh), F32)],
        compiler_params=_cp(("arbitrary", "arbitrary")), name="shared_kv",
    )(x, nw, shift, scale, w_k, w_vt, w_f, b_fgate.reshape(1, nh), lower,
      jnp.asarray(pq, BF16), jnp.asarray(pk, BF16), jnp.asarray(oq), jnp.asarray(ok))


def _fox_attn_kernel(q_ref, eq_ref, ke_ref, ko_ref, vt_ref, o_ref):
    i = pl.program_id(2)
    tq = q_ref.shape[0]
    tk = tq // KT_PER_Q
    dh2 = q_ref.shape[1]
    dh = dh2 // 2
    lo_mask = lax.broadcasted_iota(jnp.int32, (1, dh2), 1) < dh
    q = q_ref[...]
    eq = eq_ref[...]
    qs = (jnp.where(lo_mask, q, eq), jnp.where(lo_mask, eq, q))
    k_refs = (ke_ref, ko_ref)
    key_pos = lax.broadcasted_iota(jnp.int32, (tk, tq), 0)
    qry_pos = lax.broadcasted_iota(jnp.int32, (tk, tq), 1)

    def tile(kt, carry, diag):
        off = pl.multiple_of(kt * tk, tk)
        out = []
        for e in range(2):
            m, l, acc = carry[e]
            st = _dot_nt(k_refs[e][pl.ds(off, tk), :], qs[e])
            if diag is not None:
                st = jnp.where(key_pos + diag * tk <= qry_pos, st, -jnp.inf)
            m_new = jnp.maximum(m, jnp.max(st, axis=0, keepdims=True))
            pt = jnp.exp2(st - m_new)
            corr = jnp.exp2(m - m_new)
            l = corr * l + jnp.sum(pt, axis=0, keepdims=True)
            vt = vt_ref[e * dh:(e + 1) * dh, pl.ds(off, tk)]
            acc = corr * acc + _dot(vt, pt.astype(BF16))
            out.append((m_new, l, acc))
        return tuple(out)

    def trip(j, carry):
        for u in range(KT_PER_Q):
            carry = tile(j * KT_PER_Q + u, carry, None)
        return carry

    init = tuple((jnp.full((1, tq), -1e30, F32), jnp.zeros((1, tq), F32), jnp.zeros((dh, tq), F32))
                 for _ in range(2))
    carry = lax.fori_loop(0, i, trip, init)
    for u in range(KT_PER_Q):
        carry = tile(i * KT_PER_Q + u, carry, u)
    ot = jnp.concatenate([carry[0][2] / carry[0][1], carry[1][2] / carry[1][1]], axis=0)
    o_ref[...] = ot.T.astype(o_ref.dtype)


def fox_attn(q, eq, ke, ko, vt, bsz, seq):
    t, d = q.shape
    nh = B_HEADS
    dh2 = 2 * (d // nh)
    tq = min(TQ, seq)
    nq = seq // tq
    qspec = pl.BlockSpec((tq, dh2), lambda b, hp, i: (b * nq + i, hp))
    kspec = pl.BlockSpec((seq, dh2), lambda b, hp, i: (b, hp))
    vspec = pl.BlockSpec((dh2, seq), lambda b, hp, i: (hp, b))
    return pl.pallas_call(
        _fox_attn_kernel,
        grid=(bsz, nh // 2, nq),
        in_specs=[qspec, qspec, kspec, kspec, vspec],
        out_specs=qspec,
        out_shape=jax.ShapeDtypeStruct((t, d), BF16),
        compiler_params=_cp(("arbitrary", "arbitrary", "arbitrary")), name="fox_attn",
    )(q, eq, ke, ko, vt)


def _route(logits, bias_col):
    aff = _sigmoid(logits)
    sel = aff + bias_col
    s = [sel[e:e + 1, :] for e in range(N_EXPERTS)]
    a = [aff[e:e + 1, :] for e in range(N_EXPERTS)]
    gs = []
    for g in range(N_GROUPS):
        v = s[EPG * g:EPG * (g + 1)]
        best = v[0] + v[1]
        for i in range(EPG):
            for j in range(i + 1, EPG):
                if (i, j) != (0, 1):
                    best = jnp.maximum(best, v[i] + v[j])
        gs.append(best)
    grp = jnp.zeros_like(gs[0], dtype=jnp.int32)
    best = gs[0]
    for g in range(1, N_GROUPS):
        upd = gs[g] > best
        grp = jnp.where(upd, g, grp)
        best = jnp.where(upd, gs[g], best)

    def pick(arrs, j):
        out = arrs[(N_GROUPS - 1) * EPG + j]
        for g in range(N_GROUPS - 2, -1, -1):
            out = jnp.where(grp == g, arrs[g * EPG + j], out)
        return out

    v = [pick(s, j) for j in range(EPG)]
    w = [pick(a, j) for j in range(EPG)]
    i1 = jnp.zeros_like(grp)
    b1 = v[0]
    for j in range(1, EPG):
        upd = v[j] > b1
        i1 = jnp.where(upd, j, i1)
        b1 = jnp.where(upd, v[j], b1)
    i2 = jnp.full_like(grp, -1)
    b2 = jnp.full_like(b1, -jnp.inf)
    for j in range(EPG):
        upd = (i1 != j) & ((i2 < 0) | (v[j] > b2))
        i2 = jnp.where(upd, j, i2)
        b2 = jnp.where(upd, v[j], b2)

    def take(arrs, idx):
        out = arrs[EPG - 1]
        for j in range(EPG - 2, -1, -1):
            out = jnp.where(idx == j, arrs[j], out)
        return out

    w1 = take(w, i1)
    w2 = take(w, i2)
    tot = w1 + w2
    g1 = w1 / tot
    g2 = w2 / tot
    first_low = i1 < i2
    lo = jnp.where(first_low, i1, i2)
    hi = jnp.where(first_low, i2, i1)
    g_lo = jnp.where(first_low, g1, g2)
    g_hi = jnp.where(first_low, g2, g1)
    base = jnp.where(lo == 0, 0, jnp.where(lo == 1, 3, 5))
    cls = grp * N_PAIRS + base + (hi - lo - 1)
    return cls, g_lo, g_hi


def _post_mix_kernel(is_mlstm, *refs):
    if is_mlstm:
        hs_ref, og_ref, mh_ref = refs[:3]
        refs = refs[3:]
        og = og_ref[...].astype(F32)
        mix = hs_ref[...].astype(F32) * mh_ref[...] * _sigmoid(og)
    else:
        mix = refs[0][...]
        refs = refs[1:]
    (w_ref, x_ref, gm_ref, nw_ref, sh_ref, sc_ref, wrt_ref, rb_ref, us_ref,
     xnew_ref, h2_ref, cls_ref, glo_ref, ghi_ref, rank_ref, cnt_ref) = refs
    first = (pl.program_id(0) == 0) & (pl.program_id(1) == 0)

    @pl.when(first)
    def _():
        cnt_ref[...] = jnp.zeros_like(cnt_ref)

    y = _dot(mix.astype(BF16), w_ref[...])
    xnew = x_ref[...] + gm_ref[...] * y
    xnew_ref[...] = xnew
    h2 = _norm_mod(xnew, nw_ref[...], sh_ref[...], sc_ref[...])
    h2_ref[...] = h2.reshape(h2_ref.shape)
    logits = _dot_nt(wrt_ref[...], h2, HIGHEST)
    cls, g_lo, g_hi = _route(logits, rb_ref[...])
    cls_ref[...] = cls
    glo_ref[...] = g_lo
    ghi_ref[...] = g_hi
    tm = cls.shape[1]
    onehot = (lax.broadcasted_iota(jnp.int32, (CLS_PAD, tm), 0) == cls).astype(F32)
    prefix = _dot(onehot.astype(BF16), us_ref[...])
    carry = cnt_ref[:, 0:1]
    rank = jnp.sum(onehot * (prefix + carry), axis=0, keepdims=True)
    rank_ref[...] = rank.astype(jnp.int32)
    cnt_ref[...] += jnp.sum(onehot, axis=1, keepdims=True)


def post_mix(mix_args, is_mlstm, w_o, x, gm, nw, shift, scale, w_router, router_bias, bsz, seq):
    t, d = x.shape
    tm = min(TM, seq)
    ns = seq // tm
    ne = N_EXPERTS
    idx = jnp.arange(tm)
    upper_strict = (idx[:, None] < idx[None, :]).astype(BF16)
    row = pl.BlockSpec((tm, d), lambda b, s: (b * ns + s, 0))
    vec = pl.BlockSpec((None, 1, d), lambda b, s: (b, 0, 0))
    lane_row = pl.BlockSpec((None, 1, tm), lambda b, s: (b, 0, s))
    if is_mlstm:
        mix_specs = [row, row, _full((1, d))]
    else:
        mix_specs = [row]
    in_specs = mix_specs + [_full((d, d)), row, vec, _full((1, d)), vec, vec, _full((ne, d)), _full((ne, 1)),
                            _full((tm, tm))]
    row3 = pl.BlockSpec((tm, d // LANES, LANES), lambda b, s: (b * ns + s, 0, 0))
    out_specs = [row, row3, lane_row, lane_row, lane_row, lane_row, _full((CLS_PAD, 128))]
    out_shape = [jax.ShapeDtypeStruct((t, d), F32), jax.ShapeDtypeStruct((t, d // LANES, LANES), F32),
                 jax.ShapeDtypeStruct((bsz, 1, seq), jnp.int32), jax.ShapeDtypeStruct((bsz, 1, seq), F32),
                 jax.ShapeDtypeStruct((bsz, 1, seq), F32), jax.ShapeDtypeStruct((bsz, 1, seq), jnp.int32),
                 jax.ShapeDtypeStruct((CLS_PAD, 128), F32)]
    return pl.pallas_call(
        functools.partial(_post_mix_kernel, is_mlstm),
        grid=(bsz, ns), in_specs=in_specs, out_specs=out_specs, out_shape=out_shape,
        compiler_params=_cp(("arbitrary", "arbitrary")), name="post_mix",
    )(*mix_args, w_o.astype(BF16), x, gm, nw, shift, scale, w_router.T, router_bias.reshape(ne, 1), upper_strict)


def _row_scatter_kernel(rb, dest_ref, src_ref, dst_in_ref, dst_ref, sem):
    del dst_in_ref
    base = pl.program_id(0) * rb

    def start(rr, carry):
        tok = base + rr
        pltpu.make_async_copy(src_ref.at[tok], dst_ref.at[dest_ref[tok]], sem).start()
        return carry

    def wait(rr, carry):
        pltpu.make_async_copy(src_ref.at[0], dst_ref.at[0], sem).wait()
        return carry

    lax.fori_loop(0, rb, start, 0, unroll=8)
    lax.fori_loop(0, rb, wait, 0, unroll=8)


def row_scatter(dest, src, n_rows):
    t = src.shape[0]
    rb = min(RB, t)
    dst0 = jnp.zeros((n_rows,) + src.shape[1:], src.dtype)
    return pl.pallas_call(
        functools.partial(_row_scatter_kernel, rb),
        grid_spec=pltpu.PrefetchScalarGridSpec(
            num_scalar_prefetch=1, grid=(t // rb,),
            in_specs=[pl.BlockSpec(memory_space=pl.ANY), pl.BlockSpec(memory_space=pl.ANY)],
            out_specs=pl.BlockSpec(memory_space=pl.ANY),
            scratch_shapes=[pltpu.SemaphoreType.DMA]),
        out_shape=jax.ShapeDtypeStruct(dst0.shape, src.dtype),
        input_output_aliases={2: 0},
        compiler_params=pltpu.CompilerParams(dimension_semantics=("arbitrary",)), name="row_scatter",
    )(dest, src, dst0)


def _row_gather_kernel(rb, dest_ref, src_ref, dst_ref, sem):
    base = pl.program_id(0) * rb

    def start(rr, carry):
        tok = base + rr
        pltpu.make_async_copy(src_ref.at[dest_ref[tok]], dst_ref.at[tok], sem).start()
        return carry

    def wait(rr, carry):
        pltpu.make_async_copy(src_ref.at[0], dst_ref.at[0], sem).wait()
        return carry

    lax.fori_loop(0, rb, start, 0, unroll=8)
    lax.fori_loop(0, rb, wait, 0, unroll=8)


def row_gather(dest, src, t):
    rb = min(RB, t)
    return pl.pallas_call(
        functools.partial(_row_gather_kernel, rb),
        grid_spec=pltpu.PrefetchScalarGridSpec(
            num_scalar_prefetch=1, grid=(t // rb,),
            in_specs=[pl.BlockSpec(memory_space=pl.ANY)],
            out_specs=pl.BlockSpec(memory_space=pl.ANY),
            scratch_shapes=[pltpu.SemaphoreType.DMA]),
        out_shape=jax.ShapeDtypeStruct((t,) + src.shape[1:], src.dtype),
        compiler_params=pltpu.CompilerParams(dimension_semantics=("arbitrary",)), name="row_gather",
    )(dest, src)


def _lane_to_col(g_row):
    n = g_row.shape[1]
    eye = lax.broadcasted_iota(jnp.int32, (n, n), 0) == lax.broadcasted_iota(jnp.int32, (n, n), 1)
    return jnp.sum(jnp.where(eye, g_row, 0.0), axis=1, keepdims=True)


def _experts_kernel(elo_ref, ehi_ref, nused_ref, x_ref, glo_ref, ghi_ref,
                    wg_lo, wu_lo, wd_lo, wg_hi, wu_hi, wd_hi, y_ref):
    del elo_ref, ehi_ref
    i = pl.program_id(0)

    @pl.when(i < nused_ref[0])
    def _():
        xb = x_ref[...].reshape(x_ref.shape[0], -1).astype(BF16)
        acc = None
        for g_ref, wg, wu, wd in ((glo_ref, wg_lo, wu_lo, wd_lo), (ghi_ref, wg_hi, wu_hi, wd_hi)):
            hg = _dot(xb, wg[...])
            hu = _dot(xb, wu[...])
            act = hg * _sigmoid(hg) * hu * _lane_to_col(g_ref[...])
            part = _dot(act.astype(BF16), wd[...])
            acc = part if acc is None else acc + part
        y_ref[...] = acc.reshape(y_ref.shape)

    @pl.when(i >= nused_ref[0])
    def _():
        y_ref[...] = jnp.zeros_like(y_ref)


def experts(xs, glo_s, ghi_s, blk_lo, blk_hi, nused, w_gate, w_up, w_down):
    p = xs.shape[0]
    d = xs.shape[1] * xs.shape[2]
    nblk = p // MB
    de = w_gate.shape[2]
    xrow = pl.BlockSpec((MB,) + xs.shape[1:], lambda i, lo, hi, nu: (i, 0, 0))
    grow = pl.BlockSpec((None, 1, MB), lambda i, lo, hi, nu: (i, 0, 0))
    w_in_lo = pl.BlockSpec((None, d, de), lambda i, lo, hi, nu: (lo[i], 0, 0))
    w_in_hi = pl.BlockSpec((None, d, de), lambda i, lo, hi, nu: (hi[i], 0, 0))
    w_out_lo = pl.BlockSpec((None, de, d), lambda i, lo, hi, nu: (lo[i], 0, 0))
    w_out_hi = pl.BlockSpec((None, de, d), lambda i, lo, hi, nu: (hi[i], 0, 0))
    return pl.pallas_call(
        _experts_kernel,
        grid_spec=pltpu.PrefetchScalarGridSpec(
            num_scalar_prefetch=3, grid=(nblk,),
            in_specs=[xrow, grow, grow, w_in_lo, w_in_lo, w_out_lo, w_in_hi, w_in_hi, w_out_hi],
            out_specs=xrow),
        out_shape=jax.ShapeDtypeStruct(xs.shape, F32),
        compiler_params=_cp(("arbitrary",)), name="experts",
    )(blk_lo, blk_hi, nused, xs, glo_s.reshape(nblk, 1, MB), ghi_s.reshape(nblk, 1, MB),
      w_gate, w_up, w_down, w_gate, w_up, w_down)


_PAIR_LO = (0, 0, 0, 1, 1, 2)
_PAIR_HI = (1, 2, 3, 2, 3, 3)


def moe(h2, cls, g_lo, g_hi, rank, counts, w_gate, w_up, w_down):
    t = h2.shape[0]
    p = t + N_CLASSES * MB
    nblk = p // MB
    cls = cls.reshape(t)
    counts = counts[:N_CLASSES, 0].astype(jnp.int32)
    padded = (counts + MB - 1) // MB * MB
    pad_end = jnp.cumsum(padded)
    pad_start = pad_end - padded
    dest = (pad_start[cls] + rank.reshape(t)).astype(jnp.int32)
    blk_cls = jnp.clip(jnp.searchsorted(pad_end, jnp.arange(nblk, dtype=jnp.int32) * MB, side='right'),
                       0, N_CLASSES - 1).astype(jnp.int32)
    grp = blk_cls // N_PAIRS
    pr = blk_cls % N_PAIRS
    blk_lo = grp * EPG + jnp.asarray(_PAIR_LO, jnp.int32)[pr]
    blk_hi = grp * EPG + jnp.asarray(_PAIR_HI, jnp.int32)[pr]
    nused = (pad_end[-1:] // MB).astype(jnp.int32)
    glo_s = jnp.zeros((p,), F32).at[dest].set(g_lo.reshape(t))
    ghi_s = jnp.zeros((p,), F32).at[dest].set(g_hi.reshape(t))
    xs = row_scatter(dest, h2, p)
    ys = experts(xs, glo_s, ghi_s, blk_lo, blk_hi, nused, w_gate, w_up, w_down)
    return row_gather(dest, ys, t)


def _final_kernel(*refs):
    _, h, rest = _prologue(True, refs)
    rest[0][...] = h


def final_norm(x, moe_out, gf, nw, shift, scale, bsz, seq):
    t, d = x.shape
    tm = min(TM, seq)
    ns = seq // tm
    row = pl.BlockSpec((tm, d), lambda b, s: (b * ns + s, 0))
    return pl.pallas_call(
        _final_kernel, grid=(bsz, ns), in_specs=_prologue_specs(True, ns, d, tm), out_specs=row,
        out_shape=jax.ShapeDtypeStruct((t, d), F32),
        compiler_params=_cp(("arbitrary", "arbitrary")), name="final_norm",
    )(x, moe_out, gf, nw, shift, scale)


def kernel(x, c, a_w_in, a_b_gates, a_mh_norm, a_w_out, kv_norm, w_ada_kv, b_ada_kv, w_kv, b_fgate, b_w_q, b_w_o,
           norm_mix, norm_ffn, w_ada, b_ada, w_router, router_bias, w_gate, w_up, w_down, norm_final, w_ada_final,
           b_ada_final):
    bsz, seq, d = x.shape
    depth = w_ada.shape[0]
    n_a = a_w_in.shape[0]
    t = bsz * seq
    xf = x.reshape(t, d)

    mods = ada(c, w_ada, b_ada)
    kv_mod = ada(c, w_ada_kv[None], b_ada_kv[None])[0]
    fin = ada(c, w_ada_final[None], b_ada_final[None])[0]

    def vecs(m, n):
        return [m[:, None, i * d:(i + 1) * d] for i in range(n)]

    wg_b = w_gate.astype(BF16)
    wu_b = w_up.astype(BF16)
    wd_b = w_down.astype(BF16)

    moe_out = None
    gf_prev = None
    kv = None
    for layer in range(depth):
        sh_m, sc_m, g_m, sh_f, sc_f, g_f = vecs(mods[layer], 6)
        nm = norm_mix[layer].reshape(1, d)
        if layer < n_a:
            xf, (q, k, v, o, grow, gcol) = mlstm_in(xf, moe_out, gf_prev, nm, sh_m, sc_m, a_w_in[layer],
                                                     a_b_gates[layer], bsz, seq)
            hs = mlstm_cell(q, k, v, grow, gcol, bsz, seq)
            mix_args = (hs, o, a_mh_norm[layer].reshape(1, d))
            w_o = a_w_out[layer]
        else:
            j = layer - n_a
            xf, q = fox_q(xf, moe_out, gf_prev, nm, sh_m, sc_m, b_w_q[j], bsz, seq)
            if layer == n_a:
                kv_sh, kv_sc = vecs(kv_mod, 2)
                kv = shared_kv(xf, kv_norm.reshape(1, d), kv_sh, kv_sc, w_kv, b_fgate, bsz, seq)
            ke, ko, vt, eq = kv
            att = fox_attn(q, eq, ke, ko, vt, bsz, seq)
            mix_args = (att,)
            w_o = b_w_o[j]
        xf, h2, cls, g_lo, g_hi, rank, counts = post_mix(
            mix_args, layer < n_a, w_o, xf, g_m, norm_ffn[layer].reshape(1, d), sh_f, sc_f, w_router, router_bias,
            bsz, seq)
        moe_out = moe(h2, cls, g_lo, g_hi, rank, counts, wg_b[layer], wu_b[layer], wd_b[layer])
        gf_prev = g_f
    fin_sh, fin_sc = vecs(fin, 2)
    out = final_norm(xf, moe_out, gf_prev, norm_final.reshape(1, d), fin_sh, fin_sc, bsz, seq)
    return out.reshape(bsz, seq, d)
```

```python
import functools

import jax
import jax.numpy as jnp
from jax import lax
from jax.experimental import pallas as pl
from jax.experimental.pallas import tpu as pltpu

F32 = jnp.float32
BF16 = jnp.bfloat16
EPS = 1e-6
GATE_SOFTCAP = 15.0

A_HEADS = 8
B_HEADS = 16
N_EXPERTS = 16
N_GROUPS = 4
EPG = N_EXPERTS // N_GROUPS
N_PAIRS = 6
N_CLASSES = N_GROUPS * N_PAIRS
CLS_PAD = 32

VMEM_LIMIT = 56 * 1024 * 1024

TM = 512
A_L = 128
TQ = 1024
KT_PER_Q = 1
MB = 256
RB = 512
LANES = 128
LOG2E = 1.4426950408889634

HIGHEST = lax.Precision.HIGHEST


def _cp(sem):
    return pltpu.CompilerParams(dimension_semantics=sem, vmem_limit_bytes=VMEM_LIMIT)


def _dot(a, b, precision=None):
    return jnp.dot(a, b, preferred_element_type=F32, precision=precision)


def _dot_nt(a, b, precision=None):
    return lax.dot_general(a, b, (((1,), (1,)), ((), ())), preferred_element_type=F32, precision=precision)


def _dot_tn(a, b, precision=None):
    return lax.dot_general(a, b, (((0,), (0,)), ((), ())), preferred_element_type=F32, precision=precision)


def _norm_mod(xin, nw, shift, scale):
    ms = jnp.mean(xin * xin, axis=-1, keepdims=True)
    y = xin * lax.rsqrt(ms + EPS)
    return (y * nw) * (1.0 + scale) + shift


def _log_sigmoid(x):
    return jnp.minimum(x, 0.0) - jnp.log1p(jnp.exp(-jnp.abs(x)))


def _sigmoid(x):
    return 1.0 / (1.0 + jnp.exp(-x))


def _ada_kernel(c_ref, w_ref, b_ref, o_ref):
    c = c_ref[...]
    ca = c * _sigmoid(c)
    o_ref[...] = _dot(ca, w_ref[...], HIGHEST) + b_ref[...]


def ada(c, w, b):
    nl, d, n = w.shape
    bsz = c.shape[0]
    tn = 512
    return pl.pallas_call(
        _ada_kernel,
        grid=(nl, n // tn),
        in_specs=[
            pl.BlockSpec((bsz, d), lambda l, j: (0, 0)),
            pl.BlockSpec((None, d, tn), lambda l, j: (l, 0, j)),
            pl.BlockSpec((None, 1, tn), lambda l, j: (l, 0, j)),
        ],
        out_specs=pl.BlockSpec((None, bsz, tn), lambda l, j: (l, 0, j)),
        out_shape=jax.ShapeDtypeStruct((nl, bsz, n), F32),
        compiler_params=_cp(("arbitrary", "arbitrary")),
        name="ada",
    )(c, w, b.reshape(nl, 1, n))


def _prologue(has_moe, refs):
    if has_moe:
        x_ref, moe_ref, gf_ref, nw_ref, sh_ref, sc_ref = refs[:6]
        rest = refs[6:]
        xin = x_ref[...] + gf_ref[...] * moe_ref[...].reshape(x_ref.shape)
    else:
        x_ref, nw_ref, sh_ref, sc_ref = refs[:4]
        rest = refs[4:]
        xin = x_ref[...]
    h = _norm_mod(xin, nw_ref[...], sh_ref[...], sc_ref[...])
    return xin, h, rest


def _prologue_specs(has_moe, ns, d, tm):
    row = pl.BlockSpec((tm, d), lambda b, s: (b * ns + s, 0))
    vec = pl.BlockSpec((None, 1, d), lambda b, s: (b, 0, 0))
    one = pl.BlockSpec((1, d), lambda b, s: (0, 0))
    if has_moe:
        row3 = pl.BlockSpec((tm, d // LANES, LANES), lambda b, s: (b * ns + s, 0, 0))
        return [row, row3, vec, one, vec, vec]
    return [row, one, vec, vec]


def _full(shape):
    nd = len(shape)
    return pl.BlockSpec(shape, lambda b, s: (0,) * nd)


def _mlstm_in_kernel(has_moe, dqk, dv, *refs):
    xin, h, rest = _prologue(has_moe, refs)
    w_ref, wgt_ref, wg_ref, bcol_ref, brow_ref = rest[:5]
    outs = rest[5:]
    if has_moe:
        xnew_ref = outs[0]
        outs = outs[1:]
        xnew_ref[...] = xin
    q_ref, k_ref, v_ref, o_ref, grow_ref, gcol_ref = outs
    hb = h.astype(BF16)
    scale = float(dqk // A_HEADS) ** -0.5
    q_ref[...] = (_dot(hb, w_ref[:, 0:dqk]) * scale).astype(BF16)
    k_ref[...] = _dot(hb, w_ref[:, dqk:2 * dqk]).astype(BF16)
    v_ref[...] = _dot(hb, w_ref[:, 2 * dqk:2 * dqk + dv]).astype(BF16)
    o_ref[...] = _dot(hb, w_ref[:, 2 * dqk + dv:2 * dqk + 2 * dv]).astype(BF16)
    gr = _dot_nt(wgt_ref[...], hb) + bcol_ref[...]
    gr = GATE_SOFTCAP * jnp.tanh(gr / GATE_SOFTCAP)
    ridx = lax.broadcasted_iota(jnp.int32, gr.shape, 0)
    grow_ref[...] = jnp.where(ridx < A_HEADS, gr, _log_sigmoid(gr))
    gc = _dot(hb, wg_ref[...]) + brow_ref[...]
    gc = GATE_SOFTCAP * jnp.tanh(gc / GATE_SOFTCAP)
    cidx = lax.broadcasted_iota(jnp.int32, gc.shape, 1)
    gcol_ref[...] = jnp.where(cidx < A_HEADS, gc, _log_sigmoid(gc))


def mlstm_in(x, moe, gf, nw, shift, scale, w_in, b_gates, bsz, seq):
    t, d = x.shape
    dqk, dv = d // 2, d
    has_moe = moe is not None
    tm = min(TM, seq)
    ns = seq // tm
    nmain = 2 * dqk + 2 * dv
    w_main = w_in[:, :nmain].astype(BF16)
    w_g = w_in[:, nmain:].astype(BF16)
    w_gt = w_g.T
    ng = 2 * A_HEADS
    args = [x] + ([moe, gf] if has_moe else []) + [nw, shift, scale, w_main, w_gt, w_g,
                                                    b_gates.reshape(ng, 1), b_gates.reshape(1, ng)]
    in_specs = _prologue_specs(has_moe, ns, d, tm) + [
        _full((d, nmain)), _full((ng, d)), _full((d, ng)), _full((ng, 1)), _full((1, ng))]
    row = lambda n: pl.BlockSpec((tm, n), lambda b, s: (b * ns + s, 0))
    out_specs = [row(dqk), row(dqk), row(dv), row(dv),
                 pl.BlockSpec((None, ng, tm), lambda b, s: (b, 0, s)),
                 pl.BlockSpec((None, tm, ng), lambda b, s: (b, s, 0))]
    out_shape = [jax.ShapeDtypeStruct((t, dqk), BF16), jax.ShapeDtypeStruct((t, dqk), BF16),
                 jax.ShapeDtypeStruct((t, dv), BF16), jax.ShapeDtypeStruct((t, dv), BF16),
                 jax.ShapeDtypeStruct((bsz, ng, seq), F32), jax.ShapeDtypeStruct((bsz, seq, ng), F32)]
    if has_moe:
        out_specs = [row(d)] + out_specs
        out_shape = [jax.ShapeDtypeStruct((t, d), F32)] + out_shape
    res = pl.pallas_call(
        functools.partial(_mlstm_in_kernel, has_moe, dqk, dv),
        grid=(bsz, ns), in_specs=in_specs, out_specs=out_specs, out_shape=out_shape,
        compiler_params=_cp(("arbitrary", "arbitrary")), name="mlstm_in",
    )(*args)
    if has_moe:
        return res[0], res[1:]
    return x, res


def _mlstm_cell_kernel(q_ref, k_ref, v_ref, grow_ref, gcol_ref, hs_ref, cn_ref, m_ref):
    c = pl.program_id(1)
    L = q_ref.shape[0]
    dk = q_ref.shape[1] // A_HEADS
    dvh = v_ref.shape[1] // A_HEADS

    @pl.when(c == 0)
    def _():
        cn_ref[...] = jnp.zeros_like(cn_ref)
        m_ref[...] = jnp.zeros_like(m_ref)

    r = lax.broadcasted_iota(jnp.int32, (L, L), 0)
    cc = lax.broadcasted_iota(jnp.int32, (L, L), 1)
    causal = cc <= r
    lower = causal.astype(F32)
    upper = (r <= cc).astype(F32)
    grow = grow_ref[...]
    gcol = gcol_ref[...]
    b_row_all = _dot(grow[A_HEADS:, :], upper, HIGHEST)
    b_col_all = _dot(lower, gcol[:, A_HEADS:], HIGHEST)
    ones_col = (lax.broadcasted_iota(jnp.int32, (L, dvh), 1) == 0).astype(BF16)

    for h in range(A_HEADS):
        qh = q_ref[:, h * dk:(h + 1) * dk]
        kh = k_ref[:, h * dk:(h + 1) * dk]
        v_aug = jnp.concatenate([v_ref[:, h * dvh:(h + 1) * dvh], ones_col], axis=1)
        bc = b_col_all[:, h:h + 1]
        br = b_row_all[h:h + 1, :]
        igr = grow[h:h + 1, :]
        igc = gcol[:, h:h + 1]
        m_prev = m_ref[h:h + 1, 0:1]
        cn = cn_ref[h]

        d_intra = jnp.where(causal, bc - br + igr, -jnp.inf)
        d_inter = bc + m_prev
        m_t = jnp.maximum(d_inter, jnp.max(d_intra, axis=-1, keepdims=True))
        w = jnp.exp(d_intra - m_t) * _dot_nt(qh, kh)
        a = jnp.exp(d_inter - m_t)
        tot = _dot(w.astype(BF16), v_aug) + a * _dot(qh, cn.astype(BF16))
        num = tot[:, :dvh]
        den = tot[:, dvh:dvh + 1]
        hh = num / jnp.maximum(jnp.abs(den), jnp.exp(-m_t))
        hh = hh * lax.rsqrt(jnp.mean(hh * hh, axis=-1, keepdims=True) + EPS)
        hs_ref[:, h * dvh:(h + 1) * dvh] = hh.astype(hs_ref.dtype)

        b_last = bc[L - 1:L, :]
        g = b_last - bc + igc
        m_new = jnp.maximum(b_last + m_prev, jnp.max(g, axis=0, keepdims=True))
        wk = jnp.exp(g - m_new)
        decay = jnp.exp(b_last + m_prev - m_new)
        upd = _dot_tn(kh, (wk * v_aug.astype(F32)).astype(BF16))
        cn_ref[h] = decay * cn + upd
        m_ref[h:h + 1, :] = jnp.broadcast_to(m_new, (1, m_ref.shape[1]))


def mlstm_cell(q, k, v, grow, gcol, bsz, seq):
    t, dqk = q.shape
    dv = v.shape[1]
    L = min(A_L, seq)
    nc = seq // L
    ng = grow.shape[1]
    dk = dqk // A_HEADS
    dvh = dv // A_HEADS
    row = lambda n: pl.BlockSpec((L, n), lambda b, c: (b * nc + c, 0))
    return pl.pallas_call(
        _mlstm_cell_kernel,
        grid=(bsz, nc),
        in_specs=[row(dqk), row(dqk), row(dv),
                  pl.BlockSpec((None, ng, L), lambda b, c: (b, 0, c)),
                  pl.BlockSpec((None, L, ng), lambda b, c: (b, c, 0))],
        out_specs=row(dv),
        out_shape=jax.ShapeDtypeStruct((t, dv), BF16),
        scratch_shapes=[pltpu.VMEM((A_HEADS, dk, 2 * dvh), F32), pltpu.VMEM((A_HEADS, 128), F32)],
        compiler_params=_cp(("arbitrary", "arbitrary")), name="mlstm_cell",
    )(q, k, v, grow, gcol)


def _fox_q_kernel(has_moe, qscale, *refs):
    xin, h, rest = _prologue(has_moe, refs)
    w_ref = rest[0]
    outs = rest[1:]
    if has_moe:
        outs[0][...] = xin
        outs = outs[1:]
    outs[0][...] = (_dot(h.astype(BF16), w_ref[...]) * qscale).astype(BF16)


def fox_q(x, moe, gf, nw, shift, scale, w_q, bsz, seq):
    t, d = x.shape
    has_moe = moe is not None
    tm = min(TM, seq)
    ns = seq // tm
    args = [x] + ([moe, gf] if has_moe else []) + [nw, shift, scale, w_q.astype(BF16)]
    in_specs = _prologue_specs(has_moe, ns, d, tm) + [_full((d, d))]
    row = pl.BlockSpec((tm, d), lambda b, s: (b * ns + s, 0))
    out_specs = [row]
    out_shape = [jax.ShapeDtypeStruct((t, d), BF16)]
    if has_moe:
        out_specs = [row] + out_specs
        out_shape = [jax.ShapeDtypeStruct((t, d), F32)] + out_shape
    qscale = float(d // B_HEADS) ** -0.5 * LOG2E
    res = pl.pallas_call(
        functools.partial(_fox_q_kernel, has_moe, qscale),
        grid=(bsz, ns), in_specs=in_specs, out_specs=out_specs, out_shape=out_shape,
        compiler_params=_cp(("arbitrary", "arbitrary")), name="fox_q",
    )(*args)
    if has_moe:
        return res[0], res[1]
    return x, res[0]


N_BIAS_PIECES = 3


def _bias_select_matrices(d, n_heads):
    import numpy as np
    dh = d // n_heads
    pq = np.zeros((N_BIAS_PIECES, n_heads, d), np.float32)
    pk = np.zeros((N_BIAS_PIECES, n_heads, d), np.float32)
    oq = np.zeros((1, d), np.float32)
    ok = np.zeros((1, d), np.float32)
    for h in range(n_heads):
        base = (h // 2) * 2 * dh + (dh if h % 2 == 0 else 0)
        for p in range(N_BIAS_PIECES):
            pq[p, h, base + p] = 1.0
            ok[0, base + p] = 1.0
            oq[0, base + N_BIAS_PIECES + p] = 1.0
            pk[p, h, base + N_BIAS_PIECES + p] = -1.0
    return pq, pk, oq, ok


def _split3(f):
    a = f.astype(BF16)
    r1 = f - a.astype(F32)
    b = r1.astype(BF16)
    c = (r1 - b.astype(F32)).astype(BF16)
    return a, b, c


def _shared_kv_kernel(x_ref, nw_ref, sh_ref, sc_ref, w_ref, wvt_ref, wf_ref, brow_ref, lo_ref, pq_ref, pk_ref, oq_ref,
                      ok_ref, ke_ref, ko_ref, vt_ref, eq_ref, cc_ref):
    s = pl.program_id(1)
    d = x_ref.shape[1]
    dh = d // B_HEADS

    @pl.when(s == 0)
    def _():
        cc_ref[...] = jnp.zeros_like(cc_ref)

    h = _norm_mod(x_ref[...], nw_ref[...], sh_ref[...], sc_ref[...])
    hb = h.astype(BF16)
    lc = _log_sigmoid(_dot(hb, wf_ref[...]) + brow_ref[...])
    fcum = _dot(lo_ref[...], lc, HIGHEST) + cc_ref[...]
    cc_ref[...] += jnp.sum(lc, axis=0, keepdims=True)
    pieces = _split3(fcum * LOG2E)
    eq = oq_ref[...]
    ek = ok_ref[...]
    for p in range(N_BIAS_PIECES):
        eq = eq + _dot(pieces[p], pq_ref[p])
        ek = ek + _dot(pieces[p], pk_ref[p])
    eq_ref[...] = eq.astype(BF16)
    even = (lax.broadcasted_iota(jnp.int32, (1, d), 1) % (2 * dh)) < dh
    k = _dot(hb, w_ref[...])
    ke_ref[...] = jnp.where(even, k, ek).astype(BF16)
    ko_ref[...] = jnp.where(even, ek, k).astype(BF16)
    vt_ref[...] = _dot_nt(wvt_ref[...], hb).astype(BF16)


def shared_kv(x, nw, shift, scale, w_kv, b_fgate, bsz, seq):
    t, d = x.shape
    tm = min(TM, seq)
    ns = seq // tm
    nh = B_HEADS
    w_k = w_kv[:, :d].astype(BF16)
    w_vt = w_kv[:, d:2 * d].astype(BF16).T
    w_f = w_kv[:, 2 * d:].astype(BF16)
    idx = jnp.arange(tm)
    lower = (idx[None, :] <= idx[:, None]).astype(F32)
    pq, pk, oq, ok = _bias_select_matrices(d, nh)
    row = pl.BlockSpec((tm, d), lambda b, s: (b * ns + s, 0))
    vec = pl.BlockSpec((None, 1, d), lambda b, s: (b, 0, 0))
    sel = _full((N_BIAS_PIECES, nh, d))
    return pl.pallas_call(
        _shared_kv_kernel,
        grid=(bsz, ns),
        in_specs=[row, _full((1, d)), vec, vec, _full((d, d)), _full((d, d)), _full((d, nh)), _full((1, nh)),
                  _full((tm, tm)), sel, sel, _full((1, d)), _full((1, d))],
        out_specs=[row, row, pl.BlockSpec((d, tm), lambda b, s: (0, b * ns + s)), row],
        out_shape=[jax.ShapeDtypeStruct((t, d), BF16), jax.ShapeDtypeStruct((t, d), BF16),
                   jax.ShapeDtypeStruct((d, t), BF16), jax.ShapeDtypeStruct((t, d), BF16)],
        scratch_shapes=[pltpu.VMEM((1, nh), F32)],
        compiler_params=_cp(("arbitrary", "arbitrary")), name="shared_kv",
    )(x, nw, shift, scale, w_k, w_vt, w_f, b_fgate.reshape(1, nh), lower,
      jnp.asarray(pq, BF16), jnp.asarray(pk, BF16), jnp.asarray(oq), jnp.asarray(ok))


def _fox_attn_kernel(q_ref, eq_ref, ke_ref, ko_ref, vt_ref, o_ref):
    i = pl.program_id(2)
    tq = q_ref.shape[0]
    tk = tq // KT_PER_Q
    dh2 = q_ref.shape[1]
    dh = dh2 // 2
    lo_mask = lax.broadcasted_iota(jnp.int32, (1, dh2), 1) < dh
    q = q_ref[...]
    eq = eq_ref[...]
    qs = (jnp.where(lo_mask, q, eq), jnp.where(lo_mask, eq, q))
    k_refs = (ke_ref, ko_ref)
    key_pos = lax.broadcasted_iota(jnp.int32, (tk, tq), 0)
    qry_pos = lax.broadcasted_iota(jnp.int32, (tk, tq), 1)

    def tile(kt, carry, diag):
        off = pl.multiple_of(kt * tk, tk)
        out = []
        for e in range(2):
            m, l, acc = carry[e]
            st = _dot_nt(k_refs[e][pl.ds(off, tk), :], qs[e])
            if diag is not None:
                st = jnp.where(key_pos + diag * tk <= qry_pos, st, -jnp.inf)
            m_new = jnp.maximum(m, jnp.max(st, axis=0, keepdims=True))
            pt = jnp.exp2(st - m_new)
            corr = jnp.exp2(m - m_new)
            l = corr * l + jnp.sum(pt, axis=0, keepdims=True)
            vt = vt_ref[e * dh:(e + 1) * dh, pl.ds(off, tk)]
            acc = corr * acc + _dot(vt, pt.astype(BF16))
            out.append((m_new, l, acc))
        return tuple(out)

    def trip(j, carry):
        for u in range(KT_PER_Q):
            carry = tile(j * KT_PER_Q + u, carry, None)
        return carry

    init = tuple((jnp.full((1, tq), -1e30, F32), jnp.zeros((1, tq), F32), jnp.zeros((dh, tq), F32))
                 for _ in range(2))
    carry = lax.fori_loop(0, i, trip, init)
    for u in range(KT_PER_Q):
        carry = tile(i * KT_PER_Q + u, carry, u)
    ot = jnp.concatenate([carry[0][2] / carry[0][1], carry[1][2] / carry[1][1]], axis=0)
    o_ref[...] = ot.T.astype(o_ref.dtype)


def fox_attn(q, eq, ke, ko, vt, bsz, seq):
    t, d = q.shape
    nh = B_HEADS
    dh2 = 2 * (d // nh)
    tq = min(TQ, seq)
    nq = seq // tq
    qspec = pl.BlockSpec((tq, dh2), lambda b, hp, i: (b * nq + i, hp))
    kspec = pl.BlockSpec((seq, dh2), lambda b, hp, i: (b, hp))
    vspec = pl.BlockSpec((dh2, seq), lambda b, hp, i: (hp, b))
    return pl.pallas_call(
        _fox_attn_kernel,
        grid=(bsz, nh // 2, nq),
        in_specs=[qspec, qspec, kspec, kspec, vspec],
        out_specs=qspec,
        out_shape=jax.ShapeDtypeStruct((t, d), BF16),
        compiler_params=_cp(("arbitrary", "arbitrary", "arbitrary")), name="fox_attn",
    )(q, eq, ke, ko, vt)


def _route(logits, bias_col):
    aff = _sigmoid(logits)
    sel = aff + bias_col
    s = [sel[e:e + 1, :] for e in range(N_EXPERTS)]
    a = [aff[e:e + 1, :] for e in range(N_EXPERTS)]
    gs = []
    for g in range(N_GROUPS):
        v = s[EPG * g:EPG * (g + 1)]
        best = v[0] + v[1]
        for i in range(EPG):
            for j in range(i + 1, EPG):
                if (i, j) != (0, 1):
                    best = jnp.maximum(best, v[i] + v[j])
        gs.append(best)
    grp = jnp.zeros_like(gs[0], dtype=jnp.int32)
    best = gs[0]
    for g in range(1, N_GROUPS):
        upd = gs[g] > best
        grp = jnp.where(upd, g, grp)
        best = jnp.where(upd, gs[g], best)

    def pick(arrs, j):
        out = arrs[(N_GROUPS - 1) * EPG + j]
        for g in range(N_GROUPS - 2, -1, -1):
            out = jnp.where(grp == g, arrs[g * EPG + j], out)
        return out

    v = [pick(s, j) for j in range(EPG)]
    w = [pick(a, j) for j in range(EPG)]
    i1 = jnp.zeros_like(grp)
    b1 = v[0]
    for j in range(1, EPG):
        upd = v[j] > b1
        i1 = jnp.where(upd, j, i1)
        b1 = jnp.where(upd, v[j], b1)
    i2 = jnp.full_like(grp, -1)
    b2 = jnp.full_like(b1, -jnp.inf)
    for j in range(EPG):
        upd = (i1 != j) & ((i2 < 0) | (v[j] > b2))
        i2 = jnp.where(upd, j, i2)
        b2 = jnp.where(upd, v[j], b2)

    def take(arrs, idx):
        out = arrs[EPG - 1]
        for j in range(EPG - 2, -1, -1):
            out = jnp.where(idx == j, arrs[j], out)
        return out

    w1 = take(w, i1)
    w2 = take(w, i2)
    tot = w1 + w2
    g1 = w1 / tot
    g2 = w2 / tot
    first_low = i1 < i2
    lo = jnp.where(first_low, i1, i2)
    hi = jnp.where(first_low, i2, i1)
    g_lo = jnp.where(first_low, g1, g2)
    g_hi = jnp.where(first_low, g2, g1)
    base = jnp.where(lo == 0, 0, jnp.where(lo == 1, 3, 5))
    cls = grp * N_PAIRS + base + (hi - lo - 1)
    return cls, g_lo, g_hi


def _post_mix_kernel(is_mlstm, *refs):
    if is_mlstm:
        hs_ref, og_ref, mh_ref = refs[:3]
        refs = refs[3:]
        og = og_ref[...].astype(F32)
        mix = hs_ref[...].astype(F32) * mh_ref[...] * _sigmoid(og)
    else:
        mix = refs[0][...]
        refs = refs[1:]
    (w_ref, x_ref, gm_ref, nw_ref, sh_ref, sc_ref, wrt_ref, rb_ref, us_ref,
     xnew_ref, h2_ref, cls_ref, glo_ref, ghi_ref, rank_ref, cnt_ref) = refs
    first = (pl.program_id(0) == 0) & (pl.program_id(1) == 0)

    @pl.when(first)
    def _():
        cnt_ref[...] = jnp.zeros_like(cnt_ref)

    y = _dot(mix.astype(BF16), w_ref[...])
    xnew = x_ref[...] + gm_ref[...] * y
    xnew_ref[...] = xnew
    h2 = _norm_mod(xnew, nw_ref[...], sh_ref[...], sc_ref[...])
    h2_ref[...] = h2.reshape(h2_ref.shape)
    logits = _dot_nt(wrt_ref[...], h2, HIGHEST)
    cls, g_lo, g_hi = _route(logits, rb_ref[...])
    cls_ref[...] = cls
    glo_ref[...] = g_lo
    ghi_ref[...] = g_hi
    tm = cls.shape[1]
    onehot = (lax.broadcasted_iota(jnp.int32, (CLS_PAD, tm), 0) == cls).astype(F32)
    prefix = _dot(onehot.astype(BF16), us_ref[...])
    carry = cnt_ref[:, 0:1]
    rank = jnp.sum(onehot * (prefix + carry), axis=0, keepdims=True)
    rank_ref[...] = rank.astype(jnp.int32)
    cnt_ref[...] += jnp.sum(onehot, axis=1, keepdims=True)


def post_mix(mix_args, is_mlstm, w_o, x, gm, nw, shift, scale, w_router, router_bias, bsz, seq):
    t, d = x.shape
    tm = min(TM, seq)
    ns = seq // tm
    ne = N_EXPERTS
    idx = jnp.arange(tm)
    upper_strict = (idx[:, None] < idx[None, :]).astype(BF16)
    row = pl.BlockSpec((tm, d), lambda b, s: (b * ns + s, 0))
    vec = pl.BlockSpec((None, 1, d), lambda b, s: (b, 0, 0))
    lane_row = pl.BlockSpec((None, 1, tm), lambda b, s: (b, 0, s))
    if is_mlstm:
        mix_specs = [row, row, _full((1, d))]
    else:
        mix_specs = [row]
    in_specs = mix_specs + [_full((d, d)), row, vec, _full((1, d)), vec, vec, _full((ne, d)), _full((ne, 1)),
                            _full((tm, tm))]
    row3 = pl.BlockSpec((tm, d // LANES, LANES), lambda b, s: (b * ns + s, 0, 0))
    out_specs = [row, row3, lane_row, lane_row, lane_row, lane_row, _full((CLS_PAD, 128))]
    out_shape = [jax.ShapeDtypeStruct((t, d), F32), jax.ShapeDtypeStruct((t, d // LANES, LANES), F32),
                 jax.ShapeDtypeStruct((bsz, 1, seq), jnp.int32), jax.ShapeDtypeStruct((bsz, 1, seq), F32),
                 jax.ShapeDtypeStruct((bsz, 1, seq), F32), jax.ShapeDtypeStruct((bsz, 1, seq), jnp.int32),
                 jax.ShapeDtypeStruct((CLS_PAD, 128), F32)]
    return pl.pallas_call(
        functools.partial(_post_mix_kernel, is_mlstm),
        grid=(bsz, ns), in_specs=in_specs, out_specs=out_specs, out_shape=out_shape,
        compiler_params=_cp(("arbitrary", "arbitrary")), name="post_mix",
    )(*mix_args, w_o.astype(BF16), x, gm, nw, shift, scale, w_router.T, router_bias.reshape(ne, 1), upper_strict)


def _row_scatter_kernel(rb, dest_ref, src_ref, dst_in_ref, dst_ref, sem):
    del dst_in_ref
    base = pl.program_id(0) * rb

    def start(rr, carry):
        pltpu.make_async_copy(src_ref.at[rr], dst_ref.at[dest_ref[base + rr]], sem).start()
        return carry

    def wait(rr, carry):
        pltpu.make_async_copy(src_ref.at[0], dst_ref.at[0], sem).wait()
        return carry

    lax.fori_loop(0, rb, start, 0, unroll=8)
    lax.fori_loop(0, rb, wait, 0, unroll=8)


def row_scatter(dest, src, n_rows):
    t = src.shape[0]
    rb = min(RB, t)
    dst0 = jnp.zeros((n_rows,) + src.shape[1:], src.dtype)
    return pl.pallas_call(
        functools.partial(_row_scatter_kernel, rb),
        grid_spec=pltpu.PrefetchScalarGridSpec(
            num_scalar_prefetch=1, grid=(t // rb,),
            in_specs=[pl.BlockSpec((rb,) + src.shape[1:], lambda i, dest: (i, 0, 0)),
                      pl.BlockSpec(memory_space=pl.ANY)],
            out_specs=pl.BlockSpec(memory_space=pl.ANY),
            scratch_shapes=[pltpu.SemaphoreType.DMA]),
        out_shape=jax.ShapeDtypeStruct(dst0.shape, src.dtype),
        input_output_aliases={2: 0},
        compiler_params=_cp(("arbitrary",)), name="row_scatter",
    )(dest, src, dst0)


def _row_gather_kernel(rb, dest_ref, src_ref, out_ref, sem):
    base = pl.program_id(0) * rb

    def start(rr, carry):
        pltpu.make_async_copy(src_ref.at[dest_ref[base + rr]], out_ref.at[rr], sem).start()
        return carry

    def wait(rr, carry):
        pltpu.make_async_copy(src_ref.at[0], out_ref.at[0], sem).wait()
        return carry

    lax.fori_loop(0, rb, start, 0, unroll=8)
    lax.fori_loop(0, rb, wait, 0, unroll=8)


def row_gather(dest, src, t):
    rb = min(RB, t)
    return pl.pallas_call(
        functools.partial(_row_gather_kernel, rb),
        grid_spec=pltpu.PrefetchScalarGridSpec(
            num_scalar_prefetch=1, grid=(t // rb,),
            in_specs=[pl.BlockSpec(memory_space=pl.ANY)],
            out_specs=pl.BlockSpec((rb,) + src.shape[1:], lambda i, dest: (i, 0, 0)),
            scratch_shapes=[pltpu.SemaphoreType.DMA]),
        out_shape=jax.ShapeDtypeStruct((t,) + src.shape[1:], src.dtype),
        compiler_params=_cp(("arbitrary",)), name="row_gather",
    )(dest, src)


def _lane_to_col(g_row):
    n = g_row.shape[1]
    eye = lax.broadcasted_iota(jnp.int32, (n, n), 0) == lax.broadcasted_iota(jnp.int32, (n, n), 1)
    return jnp.sum(jnp.where(eye, g_row, 0.0), axis=1, keepdims=True)


def _experts_kernel(elo_ref, ehi_ref, nused_ref, x_ref, glo_ref, ghi_ref,
                    wg_lo, wu_lo, wd_lo, wg_hi, wu_hi, wd_hi, y_ref):
    del elo_ref, ehi_ref
    i = pl.program_id(0)

    @pl.when(i < nused_ref[0])
    def _():
        xb = x_ref[...].reshape(x_ref.shape[0], -1).astype(BF16)
        acc = None
        for g_ref, wg, wu, wd in ((glo_ref, wg_lo, wu_lo, wd_lo), (ghi_ref, wg_hi, wu_hi, wd_hi)):
            hg = _dot(xb, wg[...])
            hu = _dot(xb, wu[...])
            act = hg * _sigmoid(hg) * hu * _lane_to_col(g_ref[...])
            part = _dot(act.astype(BF16), wd[...])
            acc = part if acc is None else acc + part
        y_ref[...] = acc.reshape(y_ref.shape)

    @pl.when(i >= nused_ref[0])
    def _():
        y_ref[...] = jnp.zeros_like(y_ref)


def experts(xs, glo_s, ghi_s, blk_lo, blk_hi, nused, w_gate, w_up, w_down):
    p = xs.shape[0]
    d = xs.shape[1] * xs.shape[2]
    nblk = p // MB
    de = w_gate.shape[2]
    xrow = pl.BlockSpec((MB,) + xs.shape[1:], lambda i, lo, hi, nu: (i, 0, 0))
    grow = pl.BlockSpec((None, 1, MB), lambda i, lo, hi, nu: (i, 0, 0))
    w_in_lo = pl.BlockSpec((None, d, de), lambda i, lo, hi, nu: (lo[i], 0, 0))
    w_in_hi = pl.BlockSpec((None, d, de), lambda i, lo, hi, nu: (hi[i], 0, 0))
    w_out_lo = pl.BlockSpec((None, de, d), lambda i, lo, hi, nu: (lo[i], 0, 0))
    w_out_hi = pl.BlockSpec((None, de, d), lambda i, lo, hi, nu: (hi[i], 0, 0))
    return pl.pallas_call(
        _experts_kernel,
        grid_spec=pltpu.PrefetchScalarGridSpec(
            num_scalar_prefetch=3, grid=(nblk,),
            in_specs=[xrow, grow, grow, w_in_lo, w_in_lo, w_out_lo, w_in_hi, w_in_hi, w_out_hi],
            out_specs=xrow),
        out_shape=jax.ShapeDtypeStruct(xs.shape, F32),
        compiler_params=_cp(("arbitrary",)), name="experts",
    )(blk_lo, blk_hi, nused, xs, glo_s.reshape(nblk, 1, MB), ghi_s.reshape(nblk, 1, MB),
      w_gate, w_up, w_down, w_gate, w_up, w_down)


_PAIR_LO = (0, 0, 0, 1, 1, 2)
_PAIR_HI = (1, 2, 3, 2, 3, 3)


def moe(h2, cls, g_lo, g_hi, rank, counts, w_gate, w_up, w_down):
    t = h2.shape[0]
    p = t + N_CLASSES * MB
    nblk = p // MB
    cls = cls.reshape(t)
    counts = counts[:N_CLASSES, 0].astype(jnp.int32)
    padded = (counts + MB - 1) // MB * MB
    pad_end = jnp.cumsum(padded)
    pad_start = pad_end - padded
    dest = (pad_start[cls] + rank.reshape(t)).astype(jnp.int32)
    blk_cls = jnp.clip(jnp.searchsorted(pad_end, jnp.arange(nblk, dtype=jnp.int32) * MB, side='right'),
                       0, N_CLASSES - 1).astype(jnp.int32)
    grp = blk_cls // N_PAIRS
    pr = blk_cls % N_PAIRS
    blk_lo = grp * EPG + jnp.asarray(_PAIR_LO, jnp.int32)[pr]
    blk_hi = grp * EPG + jnp.asarray(_PAIR_HI, jnp.int32)[pr]
    nused = (pad_end[-1:] // MB).astype(jnp.int32)
    glo_s = jnp.zeros((p,), F32).at[dest].set(g_lo.reshape(t))
    ghi_s = jnp.zeros((p,), F32).at[dest].set(g_hi.reshape(t))
    xs = row_scatter(dest, h2, p)
    ys = experts(xs, glo_s, ghi_s, blk_lo, blk_hi, nused, w_gate, w_up, w_down)
    return row_gather(dest, ys, t)


def _final_kernel(*refs):
    _, h, rest = _prologue(True, refs)
    rest[0][...] = h


def final_norm(x, moe_out, gf, nw, shift, scale, bsz, seq):
    t, d = x.shape
    tm = min(TM, seq)
    ns = seq // tm
    row = pl.BlockSpec((tm, d), lambda b, s: (b * ns + s, 0))
    return pl.pallas_call(
        _final_kernel, grid=(bsz, ns), in_specs=_prologue_specs(True, ns, d, tm), out_specs=row,
        out_shape=jax.ShapeDtypeStruct((t, d), F32),
        compiler_params=_cp(("arbitrary", "arbitrary")), name="final_norm",
    )(x, moe_out, gf, nw, shift, scale)


def kernel(x, c, a_w_in, a_b_gates, a_mh_norm, a_w_out, kv_norm, w_ada_kv, b_ada_kv, w_kv, b_fgate, b_w_q, b_w_o,
           norm_mix, norm_ffn, w_ada, b_ada, w_router, router_bias, w_gate, w_up, w_down, norm_final, w_ada_final,
           b_ada_final):
    bsz, seq, d = x.shape
    depth = w_ada.shape[0]
    n_a = a_w_in.shape[0]
    t = bsz * seq
    xf = x.reshape(t, d)

    mods = ada(c, w_ada, b_ada)
    kv_mod = ada(c, w_ada_kv[None], b_ada_kv[None])[0]
    fin = ada(c, w_ada_final[None], b_ada_final[None])[0]

    def vecs(m, n):
        return [m[:, None, i * d:(i + 1) * d] for i in range(n)]

    wg_b = w_gate.astype(BF16)
    wu_b = w_up.astype(BF16)
    wd_b = w_down.astype(BF16)

    moe_out = None
    gf_prev = None
    kv = None
    for layer in range(depth):
        sh_m, sc_m, g_m, sh_f, sc_f, g_f = vecs(mods[layer], 6)
        nm = norm_mix[layer].reshape(1, d)
        if layer < n_a:
            xf, (q, k, v, o, grow, gcol) = mlstm_in(xf, moe_out, gf_prev, nm, sh_m, sc_m, a_w_in[layer],
                                                     a_b_gates[layer], bsz, seq)
            hs = mlstm_cell(q, k, v, grow, gcol, bsz, seq)
            mix_args = (hs, o, a_mh_norm[layer].reshape(1, d))
            w_o = a_w_out[layer]
        else:
            j = layer - n_a
            xf, q = fox_q(xf, moe_out, gf_prev, nm, sh_m, sc_m, b_w_q[j], bsz, seq)
            if layer == n_a:
                kv_sh, kv_sc = vecs(kv_mod, 2)
                kv = shared_kv(xf, kv_norm.reshape(1, d), kv_sh, kv_sc, w_kv, b_fgate, bsz, seq)
            ke, ko, vt, eq = kv
            att = fox_attn(q, eq, ke, ko, vt, bsz, seq)
            mix_args = (att,)
            w_o = b_w_o[j]
        xf, h2, cls, g_lo, g_hi, rank, counts = post_mix(
            mix_args, layer < n_a, w_o, xf, g_m, norm_ffn[layer].reshape(1, d), sh_f, sc_f, w_router, router_bias,
            bsz, seq)
        moe_out = moe(h2, cls, g_lo, g_hi, rank, counts, wg_b[layer], wu_b[layer], wd_b[layer])
        gf_prev = g_f
    fin_sh, fin_sc = vecs(fin, 2)
    out = final_norm(xf, moe_out, gf_prev, norm_final.reshape(1, d), fin_sh, fin_sc, bsz, seq)
    return out.reshape(bsz, seq, d)
```

```python
import functools

import jax
import jax.numpy as jnp
from jax import lax
from jax.experimental import pallas as pl
from jax.experimental.pallas import tpu as pltpu

F32 = jnp.float32
BF16 = jnp.bfloat16
EPS = 1e-6
GATE_SOFTCAP = 15.0

A_HEADS = 8
B_HEADS = 16
N_EXPERTS = 16
N_GROUPS = 4
EPG = N_EXPERTS // N_GROUPS
N_PAIRS = 6
N_CLASSES = N_GROUPS * N_PAIRS
CLS_PAD = 32

VMEM_LIMIT = 56 * 1024 * 1024

TM = 512
A_L = 256
TQ = 1024
KT_PER_Q = 1
MB = 256
RB = 512
LANES = 128
LOG2E = 1.4426950408889634

HIGHEST = lax.Precision.HIGHEST


def _cp(sem):
    return pltpu.CompilerParams(dimension_semantics=sem, vmem_limit_bytes=VMEM_LIMIT)


def _dot(a, b, precision=None):
    return jnp.dot(a, b, preferred_element_type=F32, precision=precision)


def _dot_nt(a, b, precision=None):
    return lax.dot_general(a, b, (((1,), (1,)), ((), ())), preferred_element_type=F32, precision=precision)


def _dot_tn(a, b, precision=None):
    return lax.dot_general(a, b, (((0,), (0,)), ((), ())), preferred_element_type=F32, precision=precision)


def _norm_mod(xin, nw, shift, scale):
    ms = jnp.mean(xin * xin, axis=-1, keepdims=True)
    y = xin * lax.rsqrt(ms + EPS)
    return (y * nw) * (1.0 + scale) + shift


def _log_sigmoid(x):
    return jnp.minimum(x, 0.0) - jnp.log1p(jnp.exp(-jnp.abs(x)))


def _sigmoid(x):
    return 1.0 / (1.0 + jnp.exp(-x))


def _ada_kernel(c_ref, w_ref, b_ref, o_ref):
    c = c_ref[...]
    ca = c * _sigmoid(c)
    o_ref[...] = _dot(ca, w_ref[...], HIGHEST) + b_ref[...]


def ada(c, w, b):
    nl, d, n = w.shape
    bsz = c.shape[0]
    tn = 512
    return pl.pallas_call(
        _ada_kernel,
        grid=(nl, n // tn),
        in_specs=[
            pl.BlockSpec((bsz, d), lambda l, j: (0, 0)),
            pl.BlockSpec((None, d, tn), lambda l, j: (l, 0, j)),
            pl.BlockSpec((None, 1, tn), lambda l, j: (l, 0, j)),
        ],
        out_specs=pl.BlockSpec((None, bsz, tn), lambda l, j: (l, 0, j)),
        out_shape=jax.ShapeDtypeStruct((nl, bsz, n), F32),
        compiler_params=_cp(("arbitrary", "arbitrary")),
        name="ada",
    )(c, w, b.reshape(nl, 1, n))


def _prologue(has_moe, refs):
    if has_moe:
        x_ref, moe_ref, gf_ref, nw_ref, sh_ref, sc_ref = refs[:6]
        rest = refs[6:]
        xin = x_ref[...] + gf_ref[...] * moe_ref[...].reshape(x_ref.shape)
    else:
        x_ref, nw_ref, sh_ref, sc_ref = refs[:4]
        rest = refs[4:]
        xin = x_ref[...]
    h = _norm_mod(xin, nw_ref[...], sh_ref[...], sc_ref[...])
    return xin, h, rest


def _prologue_specs(has_moe, ns, d, tm):
    row = pl.BlockSpec((tm, d), lambda b, s: (b * ns + s, 0))
    vec = pl.BlockSpec((None, 1, d), lambda b, s: (b, 0, 0))
    one = pl.BlockSpec((1, d), lambda b, s: (0, 0))
    if has_moe:
        row3 = pl.BlockSpec((tm, d // LANES, LANES), lambda b, s: (b * ns + s, 0, 0))
        return [row, row3, vec, one, vec, vec]
    return [row, one, vec, vec]


def _full(shape):
    nd = len(shape)
    return pl.BlockSpec(shape, lambda b, s: (0,) * nd)


def _mlstm_in_kernel(has_moe, dqk, *refs):
    xin, h, rest = _prologue(has_moe, refs)
    wqt_ref, wk_ref, wvt_ref, wo_ref, wgt_ref, wg_ref, bcol_ref, brow_ref = rest[:8]
    outs = rest[8:]
    if has_moe:
        xnew_ref = outs[0]
        outs = outs[1:]
        xnew_ref[...] = xin
    qt_ref, k_ref, vt_ref, o_ref, grow_ref, gcol_ref = outs
    hb = h.astype(BF16)
    scale = float(dqk // A_HEADS) ** -0.5
    qt_ref[...] = (_dot_nt(wqt_ref[...], hb) * scale).astype(BF16)
    k_ref[...] = _dot(hb, wk_ref[...]).astype(BF16)
    vt_ref[...] = _dot_nt(wvt_ref[...], hb).astype(BF16)
    o_ref[...] = _dot(hb, wo_ref[...]).astype(BF16)
    gr = _dot_nt(wgt_ref[...], hb) + bcol_ref[...]
    gr = GATE_SOFTCAP * jnp.tanh(gr / GATE_SOFTCAP)
    ridx = lax.broadcasted_iota(jnp.int32, gr.shape, 0)
    grow_ref[...] = jnp.where(ridx < A_HEADS, gr, _log_sigmoid(gr))
    gc = _dot(hb, wg_ref[...]) + brow_ref[...]
    gc = GATE_SOFTCAP * jnp.tanh(gc / GATE_SOFTCAP)
    cidx = lax.broadcasted_iota(jnp.int32, gc.shape, 1)
    gcol_ref[...] = jnp.where(cidx < A_HEADS, gc, _log_sigmoid(gc))


def mlstm_in(x, moe, gf, nw, shift, scale, w_in, b_gates, bsz, seq):
    t, d = x.shape
    dqk, dv = d // 2, d
    has_moe = moe is not None
    tm = min(TM, seq)
    ns = seq // tm
    w_b = w_in.astype(BF16)
    w_qt = w_b[:, 0:dqk].T
    w_k = w_b[:, dqk:2 * dqk]
    w_vt = w_b[:, 2 * dqk:2 * dqk + dv].T
    w_o = w_b[:, 2 * dqk + dv:2 * dqk + 2 * dv]
    w_g = w_b[:, 2 * dqk + 2 * dv:]
    ng = 2 * A_HEADS
    args = [x] + ([moe, gf] if has_moe else []) + [nw, shift, scale, w_qt, w_k, w_vt, w_o, w_g.T, w_g,
                                                    b_gates.reshape(ng, 1), b_gates.reshape(1, ng)]
    in_specs = _prologue_specs(has_moe, ns, d, tm) + [
        _full((dqk, d)), _full((d, dqk)), _full((dv, d)), _full((d, dv)), _full((ng, d)), _full((d, ng)),
        _full((ng, 1)), _full((1, ng))]
    row = lambda n: pl.BlockSpec((tm, n), lambda b, s: (b * ns + s, 0))
    col = lambda n: pl.BlockSpec((n, tm), lambda b, s: (0, b * ns + s))
    out_specs = [col(dqk), row(dqk), col(dv), row(dv),
                 pl.BlockSpec((None, ng, tm), lambda b, s: (b, 0, s)),
                 pl.BlockSpec((None, tm, ng), lambda b, s: (b, s, 0))]
    out_shape = [jax.ShapeDtypeStruct((dqk, t), BF16), jax.ShapeDtypeStruct((t, dqk), BF16),
                 jax.ShapeDtypeStruct((dv, t), BF16), jax.ShapeDtypeStruct((t, dv), BF16),
                 jax.ShapeDtypeStruct((bsz, ng, seq), F32), jax.ShapeDtypeStruct((bsz, seq, ng), F32)]
    if has_moe:
        out_specs = [row(d)] + out_specs
        out_shape = [jax.ShapeDtypeStruct((t, d), F32)] + out_shape
    res = pl.pallas_call(
        functools.partial(_mlstm_in_kernel, has_moe, dqk),
        grid=(bsz, ns), in_specs=in_specs, out_specs=out_specs, out_shape=out_shape,
        compiler_params=_cp(("arbitrary", "arbitrary")), name="mlstm_in",
    )(*args)
    if has_moe:
        return res[0], res[1:]
    return x, res


N_ROWS = 16


def _mlstm_cell_kernel(qt_ref, k_ref, vt_ref, grow_ref, gcol_ref, hs_ref, ct_ref, m_ref):
    c = pl.program_id(1)
    L = k_ref.shape[0]
    dk = k_ref.shape[1] // A_HEADS
    dvh = vt_ref.shape[0] // A_HEADS

    @pl.when(c == 0)
    def _():
        ct_ref[...] = jnp.zeros_like(ct_ref)
        m_ref[...] = jnp.zeros_like(m_ref)

    r = lax.broadcasted_iota(jnp.int32, (L, L), 0)
    cc = lax.broadcasted_iota(jnp.int32, (L, L), 1)
    visible = r <= cc
    upper = visible.astype(F32)
    lower = (cc <= r).astype(F32)
    grow = grow_ref[...]
    gcol = gcol_ref[...]
    b_row_all = _dot(grow[A_HEADS:, :], upper, HIGHEST)
    b_col_all = _dot(lower, gcol[:, A_HEADS:], HIGHEST)
    x_col_all = gcol[:, :A_HEADS] - b_col_all
    ones_rows = jnp.ones((N_ROWS, L), BF16)

    for h in range(A_HEADS):
        qt = qt_ref[h * dk:(h + 1) * dk, :]
        kh = k_ref[:, h * dk:(h + 1) * dk]
        v_aug = jnp.concatenate([vt_ref[h * dvh:(h + 1) * dvh, :], ones_rows], axis=0)
        br = b_row_all[h:h + 1, :]
        igr = grow[h:h + 1, :]
        xc = x_col_all[:, h:h + 1]
        m_prev = m_ref[h:h + 1, 0:1]
        ct = ct_ref[h]

        d_intra = jnp.where(visible, br + xc, -jnp.inf)
        d_inter = br + m_prev
        m_t = jnp.maximum(d_inter, jnp.max(d_intra, axis=0, keepdims=True))
        w = jnp.exp(d_intra - m_t) * _dot(kh, qt)
        a = jnp.exp(d_inter - m_t)
        tot = _dot(v_aug, w.astype(BF16)) + a * _dot(ct.astype(BF16), qt)
        num = tot[:dvh, :]
        den = tot[dvh:dvh + 1, :]
        hh = num / jnp.maximum(jnp.abs(den), jnp.exp(-m_t))
        hh = hh * lax.rsqrt(jnp.mean(hh * hh, axis=0, keepdims=True) + EPS)
        hs_ref[:, h * dvh:(h + 1) * dvh] = hh.T.astype(hs_ref.dtype)

        b_last = br[:, L - 1:L]
        g = b_last - br + igr
        m_new = jnp.maximum(b_last + m_prev, jnp.max(g, axis=1, keepdims=True))
        wk = jnp.exp(g - m_new)
        decay = jnp.exp(b_last + m_prev - m_new)
        ct_ref[h] = decay * ct + _dot((v_aug.astype(F32) * wk).astype(BF16), kh)
        m_ref[h:h + 1, :] = jnp.broadcast_to(m_new, (1, m_ref.shape[1]))


def mlstm_cell(qt, k, vt, grow, gcol, bsz, seq):
    t, dqk = k.shape
    dv = vt.shape[0]
    L = min(A_L, seq)
    nc = seq // L
    ng = grow.shape[1]
    dk = dqk // A_HEADS
    dvh = dv // A_HEADS
    return pl.pallas_call(
        _mlstm_cell_kernel,
        grid=(bsz, nc),
        in_specs=[pl.BlockSpec((dqk, L), lambda b, c: (0, b * nc + c)),
                  pl.BlockSpec((L, dqk), lambda b, c: (b * nc + c, 0)),
                  pl.BlockSpec((dv, L), lambda b, c: (0, b * nc + c)),
                  pl.BlockSpec((None, ng, L), lambda b, c: (b, 0, c)),
                  pl.BlockSpec((None, L, ng), lambda b, c: (b, c, 0))],
        out_specs=pl.BlockSpec((L, dv), lambda b, c: (b * nc + c, 0)),
        out_shape=jax.ShapeDtypeStruct((t, dv), BF16),
        scratch_shapes=[pltpu.VMEM((A_HEADS, dvh + N_ROWS, dk), F32), pltpu.VMEM((A_HEADS, 128), F32)],
        compiler_params=_cp(("arbitrary", "arbitrary")), name="mlstm_cell",
    )(qt, k, vt, grow, gcol)


def _fox_q_kernel(has_moe, qscale, *refs):
    xin, h, rest = _prologue(has_moe, refs)
    w_ref = rest[0]
    outs = rest[1:]
    if has_moe:
        outs[0][...] = xin
        outs = outs[1:]
    outs[0][...] = (_dot(h.astype(BF16), w_ref[...]) * qscale).astype(BF16)


def fox_q(x, moe, gf, nw, shift, scale, w_q, bsz, seq):
    t, d = x.shape
    has_moe = moe is not None
    tm = min(TM, seq)
    ns = seq // tm
    args = [x] + ([moe, gf] if has_moe else []) + [nw, shift, scale, w_q.astype(BF16)]
    in_specs = _prologue_specs(has_moe, ns, d, tm) + [_full((d, d))]
    row = pl.BlockSpec((tm, d), lambda b, s: (b * ns + s, 0))
    out_specs = [row]
    out_shape = [jax.ShapeDtypeStruct((t, d), BF16)]
    if has_moe:
        out_specs = [row] + out_specs
        out_shape = [jax.ShapeDtypeStruct((t, d), F32)] + out_shape
    qscale = float(d // B_HEADS) ** -0.5 * LOG2E
    res = pl.pallas_call(
        functools.partial(_fox_q_kernel, has_moe, qscale),
        grid=(bsz, ns), in_specs=in_specs, out_specs=out_specs, out_shape=out_shape,
        compiler_params=_cp(("arbitrary", "arbitrary")), name="fox_q",
    )(*args)
    if has_moe:
        return res[0], res[1]
    return x, res[0]


N_BIAS_PIECES = 3


def _bias_select_matrices(d, n_heads):
    import numpy as np
    dh = d // n_heads
    pq = np.zeros((N_BIAS_PIECES, n_heads, d), np.float32)
    pk = np.zeros((N_BIAS_PIECES, n_heads, d), np.float32)
    oq = np.zeros((1, d), np.float32)
    ok = np.zeros((1, d), np.float32)
    for h in range(n_heads):
        base = (h // 2) * 2 * dh + (dh if h % 2 == 0 else 0)
        for p in range(N_BIAS_PIECES):
            pq[p, h, base + p] = 1.0
            ok[0, base + p] = 1.0
            oq[0, base + N_BIAS_PIECES + p] = 1.0
            pk[p, h, base + N_BIAS_PIECES + p] = -1.0
    return pq, pk, oq, ok


def _split3(f):
    a = f.astype(BF16)
    r1 = f - a.astype(F32)
    b = r1.astype(BF16)
    c = (r1 - b.astype(F32)).astype(BF16)
    return a, b, c


def _shared_kv_kernel(x_ref, nw_ref, sh_ref, sc_ref, w_ref, wvt_ref, wf_ref, brow_ref, lo_ref, pq_ref, pk_ref, oq_ref,
                      ok_ref, ke_ref, ko_ref, vt_ref, eq_ref, cc_ref):
    s = pl.program_id(1)
    d = x_ref.shape[1]
    dh = d // B_HEADS

    @pl.when(s == 0)
    def _():
        cc_ref[...] = jnp.zeros_like(cc_ref)

    h = _norm_mod(x_ref[...], nw_ref[...], sh_ref[...], sc_ref[...])
    hb = h.astype(BF16)
    lc = _log_sigmoid(_dot(hb, wf_ref[...]) + brow_ref[...])
    fcum = _dot(lo_ref[...], lc, HIGHEST) + cc_ref[...]
    cc_ref[...] += jnp.sum(lc, axis=0, keepdims=True)
    pieces = _split3(fcum * LOG2E)
    eq = oq_ref[...]
    ek = ok_ref[...]
    for p in range(N_BIAS_PIECES):
        eq = eq + _dot(pieces[p], pq_ref[p])
        ek = ek + _dot(pieces[p], pk_ref[p])
    eq_ref[...] = eq.astype(BF16)
    even = (lax.broadcasted_iota(jnp.int32, (1, d), 1) % (2 * dh)) < dh
    k = _dot(hb, w_ref[...])
    ke_ref[...] = jnp.where(even, k, ek).astype(BF16)
    ko_ref[...] = jnp.where(even, ek, k).astype(BF16)
    vt_ref[...] = _dot_nt(wvt_ref[...], hb).astype(BF16)


def shared_kv(x, nw, shift, scale, w_kv, b_fgate, bsz, seq):
    t, d = x.shape
    tm = min(TM, seq)
    ns = seq // tm
    nh = B_HEADS
    w_k = w_kv[:, :d].astype(BF16)
    w_vt = w_kv[:, d:2 * d].astype(BF16).T
    w_f = w_kv[:, 2 * d:].astype(BF16)
    idx = jnp.arange(tm)
    lower = (idx[None, :] <= idx[:, None]).astype(F32)
    pq, pk, oq, ok = _bias_select_matrices(d, nh)
    row = pl.BlockSpec((tm, d), lambda b, s: (b * ns + s, 0))
    vec = pl.BlockSpec((None, 1, d), lambda b, s: (b, 0, 0))
    sel = _full((N_BIAS_PIECES, nh, d))
    return pl.pallas_call(
        _shared_kv_kernel,
        grid=(bsz, ns),
        in_specs=[row, _full((1, d)), vec, vec, _full((d, d)), _full((d, d)), _full((d, nh)), _full((1, nh)),
                  _full((tm, tm)), sel, sel, _full((1, d)), _full((1, d))],
        out_specs=[row, row, pl.BlockSpec((d, tm), lambda b, s: (0, b * ns + s)), row],
        out_shape=[jax.ShapeDtypeStruct((t, d), BF16), jax.ShapeDtypeStruct((t, d), BF16),
                   jax.ShapeDtypeStruct((d, t), BF16), jax.ShapeDtypeStruct((t, d), BF16)],
        scratch_shapes=[pltpu.VMEM((1, nh), F32)],
        compiler_params=_cp(("arbitrary", "arbitrary")), name="shared_kv",
    )(x, nw, shift, scale, w_k, w_vt, w_f, b_fgate.reshape(1, nh), lower,
      jnp.asarray(pq, BF16), jnp.asarray(pk, BF16), jnp.asarray(oq), jnp.asarray(ok))


def _fox_attn_kernel(q_ref, eq_ref, ke_ref, ko_ref, vt_ref, o_ref):
    i = pl.program_id(2)
    tq = q_ref.shape[0]
    tk = tq // KT_PER_Q
    dh2 = q_ref.shape[1]
    dh = dh2 // 2
    lo_mask = lax.broadcasted_iota(jnp.int32, (1, dh2), 1) < dh
    q = q_ref[...]
    eq = eq_ref[...]
    qs = (jnp.where(lo_mask, q, eq), jnp.where(lo_mask, eq, q))
    k_refs = (ke_ref, ko_ref)
    key_pos = lax.broadcasted_iota(jnp.int32, (tk, tq), 0)
    qry_pos = lax.broadcasted_iota(jnp.int32, (tk, tq), 1)

    def tile(kt, carry, diag):
        off = pl.multiple_of(kt * tk, tk)
        out = []
        for e in range(2):
            m, l, acc = carry[e]
            st = _dot_nt(k_refs[e][pl.ds(off, tk), :], qs[e])
            if diag is not None:
                st = jnp.where(key_pos + diag * tk <= qry_pos, st, -jnp.inf)
            m_new = jnp.maximum(m, jnp.max(st, axis=0, keepdims=True))
            pt = jnp.exp2(st - m_new)
            corr = jnp.exp2(m - m_new)
            l = corr * l + jnp.sum(pt, axis=0, keepdims=True)
            vt = vt_ref[e * dh:(e + 1) * dh, pl.ds(off, tk)]
            acc = corr * acc + _dot(vt, pt.astype(BF16))
            out.append((m_new, l, acc))
        return tuple(out)

    def trip(j, carry):
        for u in range(KT_PER_Q):
            carry = tile(j * KT_PER_Q + u, carry, None)
        return carry

    init = tuple((jnp.full((1, tq), -1e30, F32), jnp.zeros((1, tq), F32), jnp.zeros((dh, tq), F32))
                 for _ in range(2))
    carry = lax.fori_loop(0, i, trip, init)
    for u in range(KT_PER_Q):
        carry = tile(i * KT_PER_Q + u, carry, u)
    ot = jnp.concatenate([carry[0][2] / carry[0][1], carry[1][2] / carry[1][1]], axis=0)
    o_ref[...] = ot.T.astype(o_ref.dtype)


def fox_attn(q, eq, ke, ko, vt, bsz, seq):
    t, d = q.shape
    nh = B_HEADS
    dh2 = 2 * (d // nh)
    tq = min(TQ, seq)
    nq = seq // tq
    qspec = pl.BlockSpec((tq, dh2), lambda b, hp, i: (b * nq + i, hp))
    kspec = pl.BlockSpec((seq, dh2), lambda b, hp, i: (b, hp))
    vspec = pl.BlockSpec((dh2, seq), lambda b, hp, i: (hp, b))
    return pl.pallas_call(
        _fox_attn_kernel,
        grid=(bsz, nh // 2, nq),
        in_specs=[qspec, qspec, kspec, kspec, vspec],
        out_specs=qspec,
        out_shape=jax.ShapeDtypeStruct((t, d), BF16),
        compiler_params=_cp(("arbitrary", "arbitrary", "arbitrary")), name="fox_attn",
    )(q, eq, ke, ko, vt)


def _route(logits, bias_col):
    aff = _sigmoid(logits)
    sel = aff + bias_col
    s = [sel[e:e + 1, :] for e in range(N_EXPERTS)]
    gs = []
    for g in range(N_GROUPS):
        v = s[EPG * g:EPG * (g + 1)]
        best = v[0] + v[1]
        for i in range(EPG):
            for j in range(i + 1, EPG):
                if (i, j) != (0, 1):
                    best = jnp.maximum(best, v[i] + v[j])
        gs.append(best)
    grp = jnp.zeros_like(gs[0], dtype=jnp.int32)
    best = gs[0]
    for g in range(1, N_GROUPS):
        upd = gs[g] > best
        grp = jnp.where(upd, g, grp)
        best = jnp.where(upd, gs[g], best)

    def pick(arrs, j):
        out = arrs[(N_GROUPS - 1) * EPG + j]
        for g in range(N_GROUPS - 2, -1, -1):
            out = jnp.where(grp == g, arrs[g * EPG + j], out)
        return out

    v = [pick(s, j) for j in range(EPG)]
    i1 = jnp.zeros_like(grp)
    b1 = v[0]
    for j in range(1, EPG):
        upd = v[j] > b1
        i1 = jnp.where(upd, j, i1)
        b1 = jnp.where(upd, v[j], b1)
    i2 = jnp.full_like(grp, -1)
    b2 = jnp.full_like(b1, -jnp.inf)
    for j in range(EPG):
        upd = (i1 != j) & ((i2 < 0) | (v[j] > b2))
        i2 = jnp.where(upd, j, i2)
        b2 = jnp.where(upd, v[j], b2)
    lo = jnp.minimum(i1, i2)
    hi = jnp.maximum(i1, i2)
    base = jnp.where(lo == 0, 0, jnp.where(lo == 1, 3, 5))
    return grp * N_PAIRS + base + (hi - lo - 1)


def _post_mix_kernel(is_mlstm, *refs):
    if is_mlstm:
        hs_ref, og_ref, mh_ref = refs[:3]
        refs = refs[3:]
        og = og_ref[...].astype(F32)
        mix = hs_ref[...].astype(F32) * mh_ref[...] * _sigmoid(og)
    else:
        mix = refs[0][...]
        refs = refs[1:]
    (w_ref, x_ref, gm_ref, nw_ref, sh_ref, sc_ref, wrt_ref, rb_ref, us_ref,
     xnew_ref, h2_ref, cls_ref, rank_ref, cnt_ref) = refs
    first = (pl.program_id(0) == 0) & (pl.program_id(1) == 0)

    @pl.when(first)
    def _():
        cnt_ref[...] = jnp.zeros_like(cnt_ref)

    y = _dot(mix.astype(BF16), w_ref[...])
    xnew = x_ref[...] + gm_ref[...] * y
    xnew_ref[...] = xnew
    h2 = _norm_mod(xnew, nw_ref[...], sh_ref[...], sc_ref[...])
    h2_ref[...] = h2.reshape(h2_ref.shape)
    h_hi = h2.astype(BF16)
    h_lo = (h2 - h_hi.astype(F32)).astype(BF16)
    ne = wrt_ref.shape[0] // 2
    part = _dot_nt(wrt_ref[...], h_hi)
    logits = part[:ne] + part[ne:] + _dot_nt(wrt_ref[:ne, :], h_lo)
    cls = _route(logits, rb_ref[...])
    cls_ref[...] = cls
    tm = cls.shape[1]
    onehot = (lax.broadcasted_iota(jnp.int32, (CLS_PAD, tm), 0) == cls).astype(F32)
    prefix = _dot(onehot.astype(BF16), us_ref[...])
    carry = cnt_ref[:, 0:1]
    rank = jnp.sum(onehot * (prefix + carry), axis=0, keepdims=True)
    rank_ref[...] = rank.astype(jnp.int32)
    cnt_ref[...] += jnp.sum(onehot, axis=1, keepdims=True)


def post_mix(mix_args, is_mlstm, w_o, x, gm, nw, shift, scale, w_router, router_bias, bsz, seq):
    t, d = x.shape
    tm = min(TM, seq)
    ns = seq // tm
    ne = N_EXPERTS
    idx = jnp.arange(tm)
    upper_strict = (idx[:, None] < idx[None, :]).astype(BF16)
    row = pl.BlockSpec((tm, d), lambda b, s: (b * ns + s, 0))
    vec = pl.BlockSpec((None, 1, d), lambda b, s: (b, 0, 0))
    lane_row = pl.BlockSpec((None, 1, tm), lambda b, s: (b, 0, s))
    if is_mlstm:
        mix_specs = [row, row, _full((1, d))]
    else:
        mix_specs = [row]
    in_specs = mix_specs + [_full((d, d)), row, vec, _full((1, d)), vec, vec, _full((2 * ne, d)), _full((ne, 1)),
                            _full((tm, tm))]
    wr = w_router.T
    wr_hi = wr.astype(BF16)
    wr_split = jnp.concatenate([wr_hi, (wr - wr_hi.astype(F32)).astype(BF16)], axis=0)
    row3 = pl.BlockSpec((tm, d // LANES, LANES), lambda b, s: (b * ns + s, 0, 0))
    out_specs = [row, row3, lane_row, lane_row, _full((CLS_PAD, 128))]
    out_shape = [jax.ShapeDtypeStruct((t, d), F32), jax.ShapeDtypeStruct((t, d // LANES, LANES), F32),
                 jax.ShapeDtypeStruct((bsz, 1, seq), jnp.int32), jax.ShapeDtypeStruct((bsz, 1, seq), jnp.int32),
                 jax.ShapeDtypeStruct((CLS_PAD, 128), F32)]
    return pl.pallas_call(
        functools.partial(_post_mix_kernel, is_mlstm),
        grid=(bsz, ns), in_specs=in_specs, out_specs=out_specs, out_shape=out_shape,
        compiler_params=_cp(("arbitrary", "arbitrary")), name="post_mix",
    )(*mix_args, w_o.astype(BF16), x, gm, nw, shift, scale, wr_split, router_bias.reshape(ne, 1), upper_strict)


def _row_scatter_kernel(rb, dest_ref, src_ref, dst_in_ref, dst_ref, sem):
    del dst_in_ref
    base = pl.program_id(0) * rb

    def start(rr, carry):
        pltpu.make_async_copy(src_ref.at[rr], dst_ref.at[dest_ref[base + rr]], sem).start()
        return carry

    def wait(rr, carry):
        pltpu.make_async_copy(src_ref.at[0], dst_ref.at[0], sem).wait()
        return carry

    lax.fori_loop(0, rb, start, 0, unroll=8)
    lax.fori_loop(0, rb, wait, 0, unroll=8)


def row_scatter(dest, src, n_rows):
    t = src.shape[0]
    rb = min(RB, t)
    dst0 = jnp.zeros((n_rows,) + src.shape[1:], src.dtype)
    return pl.pallas_call(
        functools.partial(_row_scatter_kernel, rb),
        grid_spec=pltpu.PrefetchScalarGridSpec(
            num_scalar_prefetch=1, grid=(t // rb,),
            in_specs=[pl.BlockSpec((rb,) + src.shape[1:], lambda i, dest: (i, 0, 0)),
                      pl.BlockSpec(memory_space=pl.ANY)],
            out_specs=pl.BlockSpec(memory_space=pl.ANY),
            scratch_shapes=[pltpu.SemaphoreType.DMA]),
        out_shape=jax.ShapeDtypeStruct(dst0.shape, src.dtype),
        input_output_aliases={2: 0},
        compiler_params=_cp(("arbitrary",)), name="row_scatter",
    )(dest, src, dst0)


def _row_gather_kernel(rb, dest_ref, src_ref, out_ref, sem):
    base = pl.program_id(0) * rb

    def start(rr, carry):
        pltpu.make_async_copy(src_ref.at[dest_ref[base + rr]], out_ref.at[rr], sem).start()
        return carry

    def wait(rr, carry):
        pltpu.make_async_copy(src_ref.at[0], out_ref.at[0], sem).wait()
        return carry

    lax.fori_loop(0, rb, start, 0, unroll=8)
    lax.fori_loop(0, rb, wait, 0, unroll=8)


def row_gather(dest, src, t):
    rb = min(RB, t)
    return pl.pallas_call(
        functools.partial(_row_gather_kernel, rb),
        grid_spec=pltpu.PrefetchScalarGridSpec(
            num_scalar_prefetch=1, grid=(t // rb,),
            in_specs=[pl.BlockSpec(memory_space=pl.ANY)],
            out_specs=pl.BlockSpec((rb,) + src.shape[1:], lambda i, dest: (i, 0, 0)),
            scratch_shapes=[pltpu.SemaphoreType.DMA]),
        out_shape=jax.ShapeDtypeStruct((t,) + src.shape[1:], src.dtype),
        compiler_params=_cp(("arbitrary",)), name="row_gather",
    )(dest, src)


def _experts_kernel(elo_ref, ehi_ref, nused_ref, x_ref, wrt_ref,
                    wg_lo, wu_lo, wd_lo, wg_hi, wu_hi, wd_hi, y_ref):
    i = pl.program_id(0)

    @pl.when(i < nused_ref[0])
    def _():
        x = x_ref[...].reshape(x_ref.shape[0], -1)
        xb = x.astype(BF16)
        aff_lo = _sigmoid(jnp.sum(x * wrt_ref[pl.ds(elo_ref[i], 1), :], axis=1, keepdims=True))
        aff_hi = _sigmoid(jnp.sum(x * wrt_ref[pl.ds(ehi_ref[i], 1), :], axis=1, keepdims=True))
        tot = aff_lo + aff_hi
        acc = None
        for gate, wg, wu, wd in ((aff_lo / tot, wg_lo, wu_lo, wd_lo), (aff_hi / tot, wg_hi, wu_hi, wd_hi)):
            hg = _dot(xb, wg[...])
            hu = _dot(xb, wu[...])
            act = hg * _sigmoid(hg) * hu * gate
            part = _dot(act.astype(BF16), wd[...])
            acc = part if acc is None else acc + part
        y_ref[...] = acc.reshape(y_ref.shape)

    @pl.when(i >= nused_ref[0])
    def _():
        y_ref[...] = jnp.zeros_like(y_ref)


def experts(xs, w_router_t, blk_lo, blk_hi, nused, w_gate, w_up, w_down):
    p = xs.shape[0]
    d = xs.shape[1] * xs.shape[2]
    nblk = p // MB
    de = w_gate.shape[2]
    xrow = pl.BlockSpec((MB,) + xs.shape[1:], lambda i, lo, hi, nu: (i, 0, 0))
    wrt = pl.BlockSpec(w_router_t.shape, lambda i, lo, hi, nu: (0, 0))
    w_in_lo = pl.BlockSpec((None, d, de), lambda i, lo, hi, nu: (lo[i], 0, 0))
    w_in_hi = pl.BlockSpec((None, d, de), lambda i, lo, hi, nu: (hi[i], 0, 0))
    w_out_lo = pl.BlockSpec((None, de, d), lambda i, lo, hi, nu: (lo[i], 0, 0))
    w_out_hi = pl.BlockSpec((None, de, d), lambda i, lo, hi, nu: (hi[i], 0, 0))
    return pl.pallas_call(
        _experts_kernel,
        grid_spec=pltpu.PrefetchScalarGridSpec(
            num_scalar_prefetch=3, grid=(nblk,),
            in_specs=[xrow, wrt, w_in_lo, w_in_lo, w_out_lo, w_in_hi, w_in_hi, w_out_hi],
            out_specs=xrow),
        out_shape=jax.ShapeDtypeStruct(xs.shape, F32),
        compiler_params=_cp(("arbitrary",)), name="experts",
    )(blk_lo, blk_hi, nused, xs, w_router_t, w_gate, w_up, w_down, w_gate, w_up, w_down)


_PAIR_LO = (0, 0, 0, 1, 1, 2)
_PAIR_HI = (1, 2, 3, 2, 3, 3)


def moe(h2, cls, rank, counts, w_router_t, w_gate, w_up, w_down):
    t = h2.shape[0]
    p = t + N_CLASSES * MB
    nblk = p // MB
    cls = cls.reshape(t)
    counts = counts[:N_CLASSES, 0].astype(jnp.int32)
    padded = (counts + MB - 1) // MB * MB
    pad_end = jnp.cumsum(padded)
    pad_start = pad_end - padded
    dest = (pad_start[cls] + rank.reshape(t)).astype(jnp.int32)
    blk_row = jnp.arange(nblk, dtype=jnp.int32) * MB
    blk_cls = jnp.minimum(jnp.sum((pad_end[None, :] <= blk_row[:, None]).astype(jnp.int32), axis=1), N_CLASSES - 1)
    grp = blk_cls // N_PAIRS
    pr = blk_cls % N_PAIRS
    blk_lo = grp * EPG + jnp.asarray(_PAIR_LO, jnp.int32)[pr]
    blk_hi = grp * EPG + jnp.asarray(_PAIR_HI, jnp.int32)[pr]
    nused = (pad_end[-1:] // MB).astype(jnp.int32)
    xs = row_scatter(dest, h2, p)
    ys = experts(xs, w_router_t, blk_lo, blk_hi, nused, w_gate, w_up, w_down)
    return row_gather(dest, ys, t)


def _final_kernel(*refs):
    _, h, rest = _prologue(True, refs)
    rest[0][...] = h


def final_norm(x, moe_out, gf, nw, shift, scale, bsz, seq):
    t, d = x.shape
    tm = min(TM, seq)
    ns = seq // tm
    row = pl.BlockSpec((tm, d), lambda b, s: (b * ns + s, 0))
    return pl.pallas_call(
        _final_kernel, grid=(bsz, ns), in_specs=_prologue_specs(True, ns, d, tm), out_specs=row,
        out_shape=jax.ShapeDtypeStruct((t, d), F32),
        compiler_params=_cp(("arbitrary", "arbitrary")), name="final_norm",
    )(x, moe_out, gf, nw, shift, scale)


def kernel(x, c, a_w_in, a_b_gates, a_mh_norm, a_w_out, kv_norm, w_ada_kv, b_ada_kv, w_kv, b_fgate, b_w_q, b_w_o,
           norm_mix, norm_ffn, w_ada, b_ada, w_router, router_bias, w_gate, w_up, w_down, norm_final, w_ada_final,
           b_ada_final):
    bsz, seq, d = x.shape
    depth = w_ada.shape[0]
    n_a = a_w_in.shape[0]
    t = bsz * seq
    xf = x.reshape(t, d)

    mods = ada(c, w_ada, b_ada)
    kv_mod = ada(c, w_ada_kv[None], b_ada_kv[None])[0]
    fin = ada(c, w_ada_final[None], b_ada_final[None])[0]

    def vecs(m, n):
        return [m[:, None, i * d:(i + 1) * d] for i in range(n)]

    wg_b = w_gate.astype(BF16)
    wu_b = w_up.astype(BF16)
    wd_b = w_down.astype(BF16)

    moe_out = None
    gf_prev = None
    kv = None
    for layer in range(depth):
        sh_m, sc_m, g_m, sh_f, sc_f, g_f = vecs(mods[layer], 6)
        nm = norm_mix[layer].reshape(1, d)
        if layer < n_a:
            xf, (q, k, v, o, grow, gcol) = mlstm_in(xf, moe_out, gf_prev, nm, sh_m, sc_m, a_w_in[layer],
                                                     a_b_gates[layer], bsz, seq)
            hs = mlstm_cell(q, k, v, grow, gcol, bsz, seq)
            mix_args = (hs, o, a_mh_norm[layer].reshape(1, d))
            w_o = a_w_out[layer]
        else:
            j = layer - n_a
            xf, q = fox_q(xf, moe_out, gf_prev, nm, sh_m, sc_m, b_w_q[j], bsz, seq)
            if layer == n_a:
                kv_sh, kv_sc = vecs(kv_mod, 2)
                kv = shared_kv(xf, kv_norm.reshape(1, d), kv_sh, kv_sc, w_kv, b_fgate, bsz, seq)
            ke, ko, vt, eq = kv
            att = fox_attn(q, eq, ke, ko, vt, bsz, seq)
            mix_args = (att,)
            w_o = b_w_o[j]
        xf, h2, cls, rank, counts = post_mix(
            mix_args, layer < n_a, w_o, xf, g_m, norm_ffn[layer].reshape(1, d), sh_f, sc_f, w_router, router_bias,
            bsz, seq)
        moe_out = moe(h2, cls, rank, counts, w_router.T, wg_b[layer], wu_b[layer], wd_b[layer])
        gf_prev = g_f
    fin_sh, fin_sc = vecs(fin, 2)
    out = final_norm(xf, moe_out, gf_prev, norm_final.reshape(1, d), fin_sh, fin_sc, bsz, seq)
    return out.reshape(bsz, seq, d)
```

```python
import functools

import jax
import jax.numpy as jnp
from jax import lax
from jax.experimental import pallas as pl
from jax.experimental.pallas import tpu as pltpu

F32 = jnp.float32
BF16 = jnp.bfloat16
EPS = 1e-6
GATE_SOFTCAP = 15.0

A_HEADS = 8
B_HEADS = 16
N_EXPERTS = 16
N_GROUPS = 4
EPG = N_EXPERTS // N_GROUPS
N_PAIRS = 6
N_CLASSES = N_GROUPS * N_PAIRS
CLS_PAD = 32

VMEM_LIMIT = 56 * 1024 * 1024

TM = 512
A_L = 256
TQ = 1024
ATT_CHUNK = 256
MB = 256
RB = 512
LANES = 128
LOG2E = 1.4426950408889634

HIGHEST = lax.Precision.HIGHEST


def _cp(sem):
    return pltpu.CompilerParams(dimension_semantics=sem, vmem_limit_bytes=VMEM_LIMIT)


def _dot(a, b, precision=None):
    return jnp.dot(a, b, preferred_element_type=F32, precision=precision)


def _dot_nt(a, b, precision=None):
    return lax.dot_general(a, b, (((1,), (1,)), ((), ())), preferred_element_type=F32, precision=precision)


def _dot_tn(a, b, precision=None):
    return lax.dot_general(a, b, (((0,), (0,)), ((), ())), preferred_element_type=F32, precision=precision)


def _norm_mod(xin, nw, shift, scale):
    ms = jnp.mean(xin * xin, axis=-1, keepdims=True)
    y = xin * lax.rsqrt(ms + EPS)
    return (y * nw) * (1.0 + scale) + shift


def _log_sigmoid(x):
    return jnp.minimum(x, 0.0) - jnp.log1p(jnp.exp(-jnp.abs(x)))


def _sigmoid(x):
    return 1.0 / (1.0 + jnp.exp(-x))


def _ada_kernel(c_ref, w_ref, b_ref, o_ref):
    c = c_ref[...]
    ca = c * _sigmoid(c)
    o_ref[...] = _dot(ca, w_ref[...], HIGHEST) + b_ref[...]


def ada(c, w, b):
    nl, d, n = w.shape
    bsz = c.shape[0]
    tn = 512
    return pl.pallas_call(
        _ada_kernel,
        grid=(nl, n // tn),
        in_specs=[
            pl.BlockSpec((bsz, d), lambda l, j: (0, 0)),
            pl.BlockSpec((None, d, tn), lambda l, j: (l, 0, j)),
            pl.BlockSpec((None, 1, tn), lambda l, j: (l, 0, j)),
        ],
        out_specs=pl.BlockSpec((None, bsz, tn), lambda l, j: (l, 0, j)),
        out_shape=jax.ShapeDtypeStruct((nl, bsz, n), F32),
        compiler_params=_cp(("arbitrary", "arbitrary")),
        name="ada",
    )(c, w, b.reshape(nl, 1, n))


def _prologue(has_moe, refs):
    if has_moe:
        x_ref, moe_ref, gf_ref, nw_ref, sh_ref, sc_ref = refs[:6]
        rest = refs[6:]
        xin = x_ref[...] + gf_ref[...] * moe_ref[...].reshape(x_ref.shape)
    else:
        x_ref, nw_ref, sh_ref, sc_ref = refs[:4]
        rest = refs[4:]
        xin = x_ref[...]
    h = _norm_mod(xin, nw_ref[...], sh_ref[...], sc_ref[...])
    return xin, h, rest


def _prologue_specs(has_moe, ns, d, tm):
    row = pl.BlockSpec((tm, d), lambda b, s: (b * ns + s, 0))
    vec = pl.BlockSpec((None, 1, d), lambda b, s: (b, 0, 0))
    one = pl.BlockSpec((1, d), lambda b, s: (0, 0))
    if has_moe:
        row3 = pl.BlockSpec((tm, d // LANES, LANES), lambda b, s: (b * ns + s, 0, 0))
        return [row, row3, vec, one, vec, vec]
    return [row, one, vec, vec]


def _full(shape):
    nd = len(shape)
    return pl.BlockSpec(shape, lambda b, s: (0,) * nd)


def _mlstm_in_kernel(has_moe, dqk, *refs):
    xin, h, rest = _prologue(has_moe, refs)
    wqt_ref, wk_ref, wvt_ref, wo_ref, wgt_ref, wg_ref, bcol_ref, brow_ref = rest[:8]
    outs = rest[8:]
    if has_moe:
        xnew_ref = outs[0]
        outs = outs[1:]
        xnew_ref[...] = xin
    qt_ref, k_ref, vt_ref, o_ref, grow_ref, gcol_ref = outs
    hb = h.astype(BF16)
    scale = float(dqk // A_HEADS) ** -0.5
    qt_ref[...] = (_dot_nt(wqt_ref[...], hb) * scale).astype(BF16)
    k_ref[...] = _dot(hb, wk_ref[...]).astype(BF16)
    vt_ref[...] = _dot_nt(wvt_ref[...], hb).astype(BF16)
    o_ref[...] = _dot(hb, wo_ref[...]).astype(BF16)
    gr = _dot_nt(wgt_ref[...], hb) + bcol_ref[...]
    gr = GATE_SOFTCAP * jnp.tanh(gr / GATE_SOFTCAP)
    ridx = lax.broadcasted_iota(jnp.int32, gr.shape, 0)
    grow_ref[...] = jnp.where(ridx < A_HEADS, gr, _log_sigmoid(gr))
    gc = _dot(hb, wg_ref[...]) + brow_ref[...]
    gc = GATE_SOFTCAP * jnp.tanh(gc / GATE_SOFTCAP)
    cidx = lax.broadcasted_iota(jnp.int32, gc.shape, 1)
    gcol_ref[...] = jnp.where(cidx < A_HEADS, gc, _log_sigmoid(gc))


def mlstm_in(x, moe, gf, nw, shift, scale, w_in, b_gates, bsz, seq):
    t, d = x.shape
    dqk, dv = d // 2, d
    has_moe = moe is not None
    tm = min(TM, seq)
    ns = seq // tm
    w_b = w_in.astype(BF16)
    w_qt = w_b[:, 0:dqk].T
    w_k = w_b[:, dqk:2 * dqk]
    w_vt = w_b[:, 2 * dqk:2 * dqk + dv].T
    w_o = w_b[:, 2 * dqk + dv:2 * dqk + 2 * dv]
    w_g = w_b[:, 2 * dqk + 2 * dv:]
    ng = 2 * A_HEADS
    args = [x] + ([moe, gf] if has_moe else []) + [nw, shift, scale, w_qt, w_k, w_vt, w_o, w_g.T, w_g,
                                                    b_gates.reshape(ng, 1), b_gates.reshape(1, ng)]
    in_specs = _prologue_specs(has_moe, ns, d, tm) + [
        _full((dqk, d)), _full((d, dqk)), _full((dv, d)), _full((d, dv)), _full((ng, d)), _full((d, ng)),
        _full((ng, 1)), _full((1, ng))]
    row = lambda n: pl.BlockSpec((tm, n), lambda b, s: (b * ns + s, 0))
    col = lambda n: pl.BlockSpec((n, tm), lambda b, s: (0, b * ns + s))
    out_specs = [col(dqk), row(dqk), col(dv), row(dv),
                 pl.BlockSpec((None, ng, tm), lambda b, s: (b, 0, s)),
                 pl.BlockSpec((None, tm, ng), lambda b, s: (b, s, 0))]
    out_shape = [jax.ShapeDtypeStruct((dqk, t), BF16), jax.ShapeDtypeStruct((t, dqk), BF16),
                 jax.ShapeDtypeStruct((dv, t), BF16), jax.ShapeDtypeStruct((t, dv), BF16),
                 jax.ShapeDtypeStruct((bsz, ng, seq), F32), jax.ShapeDtypeStruct((bsz, seq, ng), F32)]
    if has_moe:
        out_specs = [row(d)] + out_specs
        out_shape = [jax.ShapeDtypeStruct((t, d), F32)] + out_shape
    res = pl.pallas_call(
        functools.partial(_mlstm_in_kernel, has_moe, dqk),
        grid=(bsz, ns), in_specs=in_specs, out_specs=out_specs, out_shape=out_shape,
        compiler_params=_cp(("arbitrary", "arbitrary")), name="mlstm_in",
    )(*args)
    if has_moe:
        return res[0], res[1:]
    return x, res


N_ROWS = 16


def _mlstm_cell_kernel(qt_ref, k_ref, vt_ref, grow_ref, gcol_ref, hs_ref, ct_ref, m_ref):
    c = pl.program_id(1)
    L = k_ref.shape[0]
    dk = k_ref.shape[1] // A_HEADS
    dvh = vt_ref.shape[0] // A_HEADS

    @pl.when(c == 0)
    def _():
        ct_ref[...] = jnp.zeros_like(ct_ref)
        m_ref[...] = jnp.zeros_like(m_ref)

    r = lax.broadcasted_iota(jnp.int32, (L, L), 0)
    cc = lax.broadcasted_iota(jnp.int32, (L, L), 1)
    visible = r <= cc
    upper = visible.astype(F32)
    lower = (cc <= r).astype(F32)
    grow = grow_ref[...]
    gcol = gcol_ref[...]
    b_row_all = _dot(grow[A_HEADS:, :], upper, HIGHEST)
    b_col_all = _dot(lower, gcol[:, A_HEADS:], HIGHEST)
    x_col_all = gcol[:, :A_HEADS] - b_col_all
    ones_rows = jnp.ones((N_ROWS, L), BF16)

    for h in range(A_HEADS):
        qt = qt_ref[h * dk:(h + 1) * dk, :]
        kh = k_ref[:, h * dk:(h + 1) * dk]
        v_aug = jnp.concatenate([vt_ref[h * dvh:(h + 1) * dvh, :], ones_rows], axis=0)
        br = b_row_all[h:h + 1, :]
        igr = grow[h:h + 1, :]
        xc = x_col_all[:, h:h + 1]
        m_prev = m_ref[h:h + 1, 0:1]
        ct = ct_ref[h]

        d_intra = jnp.where(visible, br + xc, -jnp.inf)
        d_inter = br + m_prev
        m_t = jnp.maximum(d_inter, jnp.max(d_intra, axis=0, keepdims=True))
        w = jnp.exp(d_intra - m_t) * _dot(kh, qt)
        a = jnp.exp(d_inter - m_t)
        tot = _dot(v_aug, w.astype(BF16)) + a * _dot(ct.astype(BF16), qt)
        num = tot[:dvh, :]
        den = tot[dvh:dvh + 1, :]
        hh = num / jnp.maximum(jnp.abs(den), jnp.exp(-m_t))
        hh = hh * lax.rsqrt(jnp.mean(hh * hh, axis=0, keepdims=True) + EPS)
        hs_ref[:, h * dvh:(h + 1) * dvh] = hh.T.astype(hs_ref.dtype)

        b_last = br[:, L - 1:L]
        g = b_last - br + igr
        m_new = jnp.maximum(b_last + m_prev, jnp.max(g, axis=1, keepdims=True))
        wk = jnp.exp(g - m_new)
        decay = jnp.exp(b_last + m_prev - m_new)
        ct_ref[h] = decay * ct + _dot((v_aug.astype(F32) * wk).astype(BF16), kh)
        m_ref[h:h + 1, :] = jnp.broadcast_to(m_new, (1, m_ref.shape[1]))


def mlstm_cell(qt, k, vt, grow, gcol, bsz, seq):
    t, dqk = k.shape
    dv = vt.shape[0]
    L = min(A_L, seq)
    nc = seq // L
    ng = grow.shape[1]
    dk = dqk // A_HEADS
    dvh = dv // A_HEADS
    return pl.pallas_call(
        _mlstm_cell_kernel,
        grid=(bsz, nc),
        in_specs=[pl.BlockSpec((dqk, L), lambda b, c: (0, b * nc + c)),
                  pl.BlockSpec((L, dqk), lambda b, c: (b * nc + c, 0)),
                  pl.BlockSpec((dv, L), lambda b, c: (0, b * nc + c)),
                  pl.BlockSpec((None, ng, L), lambda b, c: (b, 0, c)),
                  pl.BlockSpec((None, L, ng), lambda b, c: (b, c, 0))],
        out_specs=pl.BlockSpec((L, dv), lambda b, c: (b * nc + c, 0)),
        out_shape=jax.ShapeDtypeStruct((t, dv), BF16),
        scratch_shapes=[pltpu.VMEM((A_HEADS, dvh + N_ROWS, dk), F32), pltpu.VMEM((A_HEADS, 128), F32)],
        compiler_params=_cp(("arbitrary", "arbitrary")), name="mlstm_cell",
    )(qt, k, vt, grow, gcol)


def _fox_q_kernel(has_moe, qscale, *refs):
    xin, h, rest = _prologue(has_moe, refs)
    w_ref = rest[0]
    outs = rest[1:]
    if has_moe:
        outs[0][...] = xin
        outs = outs[1:]
    outs[0][...] = (_dot(h.astype(BF16), w_ref[...]) * qscale).astype(BF16)


def fox_q(x, moe, gf, nw, shift, scale, w_q, bsz, seq):
    t, d = x.shape
    has_moe = moe is not None
    tm = min(TM, seq)
    ns = seq // tm
    args = [x] + ([moe, gf] if has_moe else []) + [nw, shift, scale, w_q.astype(BF16)]
    in_specs = _prologue_specs(has_moe, ns, d, tm) + [_full((d, d))]
    row = pl.BlockSpec((tm, d), lambda b, s: (b * ns + s, 0))
    out_specs = [row]
    out_shape = [jax.ShapeDtypeStruct((t, d), BF16)]
    if has_moe:
        out_specs = [row] + out_specs
        out_shape = [jax.ShapeDtypeStruct((t, d), F32)] + out_shape
    qscale = float(d // B_HEADS) ** -0.5 * LOG2E
    res = pl.pallas_call(
        functools.partial(_fox_q_kernel, has_moe, qscale),
        grid=(bsz, ns), in_specs=in_specs, out_specs=out_specs, out_shape=out_shape,
        compiler_params=_cp(("arbitrary", "arbitrary")), name="fox_q",
    )(*args)
    if has_moe:
        return res[0], res[1]
    return x, res[0]


N_BIAS_PIECES = 3


def _bias_select_matrices(d, n_heads):
    import numpy as np
    dh = d // n_heads
    pq = np.zeros((N_BIAS_PIECES, n_heads, d), np.float32)
    pk = np.zeros((N_BIAS_PIECES, n_heads, d), np.float32)
    oq = np.zeros((1, d), np.float32)
    ok = np.zeros((1, d), np.float32)
    for h in range(n_heads):
        base = (h // 2) * 2 * dh + (dh if h % 2 == 0 else 0)
        for p in range(N_BIAS_PIECES):
            pq[p, h, base + p] = 1.0
            ok[0, base + p] = 1.0
            oq[0, base + N_BIAS_PIECES + p] = 1.0
            pk[p, h, base + N_BIAS_PIECES + p] = -1.0
    return pq, pk, oq, ok


def _split3(f):
    a = f.astype(BF16)
    r1 = f - a.astype(F32)
    b = r1.astype(BF16)
    c = (r1 - b.astype(F32)).astype(BF16)
    return a, b, c


def _shared_kv_kernel(x_ref, nw_ref, sh_ref, sc_ref, w_ref, wvt_ref, wf_ref, brow_ref, lo_ref, pq_ref, pk_ref, oq_ref,
                      ok_ref, ke_ref, ko_ref, vt_ref, eq_ref, cc_ref):
    s = pl.program_id(1)
    d = x_ref.shape[1]
    dh = d // B_HEADS

    @pl.when(s == 0)
    def _():
        cc_ref[...] = jnp.zeros_like(cc_ref)

    h = _norm_mod(x_ref[...], nw_ref[...], sh_ref[...], sc_ref[...])
    hb = h.astype(BF16)
    lc = _log_sigmoid(_dot(hb, wf_ref[...]) + brow_ref[...])
    fcum = _dot(lo_ref[...], lc, HIGHEST) + cc_ref[...]
    cc_ref[...] += jnp.sum(lc, axis=0, keepdims=True)
    pieces = _split3(fcum * LOG2E)
    eq = oq_ref[...]
    ek = ok_ref[...]
    for p in range(N_BIAS_PIECES):
        eq = eq + _dot(pieces[p], pq_ref[p])
        ek = ek + _dot(pieces[p], pk_ref[p])
    eq_ref[...] = eq.astype(BF16)
    even = (lax.broadcasted_iota(jnp.int32, (1, d), 1) % (2 * dh)) < dh
    k = _dot(hb, w_ref[...])
    ke_ref[...] = jnp.where(even, k, ek).astype(BF16)
    ko_ref[...] = jnp.where(even, ek, k).astype(BF16)
    vt_ref[...] = _dot_nt(wvt_ref[...], hb).astype(BF16)


def shared_kv(x, nw, shift, scale, w_kv, b_fgate, bsz, seq):
    t, d = x.shape
    tm = min(TM, seq)
    ns = seq // tm
    nh = B_HEADS
    w_k = w_kv[:, :d].astype(BF16)
    w_vt = w_kv[:, d:2 * d].astype(BF16).T
    w_f = w_kv[:, 2 * d:].astype(BF16)
    idx = jnp.arange(tm)
    lower = (idx[None, :] <= idx[:, None]).astype(F32)
    pq, pk, oq, ok = _bias_select_matrices(d, nh)
    row = pl.BlockSpec((tm, d), lambda b, s: (b * ns + s, 0))
    vec = pl.BlockSpec((None, 1, d), lambda b, s: (b, 0, 0))
    sel = _full((N_BIAS_PIECES, nh, d))
    return pl.pallas_call(
        _shared_kv_kernel,
        grid=(bsz, ns),
        in_specs=[row, _full((1, d)), vec, vec, _full((d, d)), _full((d, d)), _full((d, nh)), _full((1, nh)),
                  _full((tm, tm)), sel, sel, _full((1, d)), _full((1, d))],
        out_specs=[row, row, pl.BlockSpec((d, tm), lambda b, s: (0, b * ns + s)), row],
        out_shape=[jax.ShapeDtypeStruct((t, d), BF16), jax.ShapeDtypeStruct((t, d), BF16),
                   jax.ShapeDtypeStruct((d, t), BF16), jax.ShapeDtypeStruct((t, d), BF16)],
        scratch_shapes=[pltpu.VMEM((1, nh), F32)],
        compiler_params=_cp(("arbitrary", "arbitrary")), name="shared_kv",
    )(x, nw, shift, scale, w_k, w_vt, w_f, b_fgate.reshape(1, nh), lower,
      jnp.asarray(pq, BF16), jnp.asarray(pk, BF16), jnp.asarray(oq), jnp.asarray(ok))


def _fox_attn_kernel(q_ref, eq_ref, ke_ref, ko_ref, vt_ref, o_ref, st_ref):
    i = pl.program_id(2)
    tq = q_ref.shape[0]
    tk = tq
    ck = min(ATT_CHUNK, tk)
    nchunk = tk // ck
    dh2 = q_ref.shape[1]
    dh = dh2 // 2
    lo_mask = lax.broadcasted_iota(jnp.int32, (1, dh2), 1) < dh
    q = q_ref[...]
    eq = eq_ref[...]
    qs = (jnp.where(lo_mask, q, eq), jnp.where(lo_mask, eq, q))
    k_refs = (ke_ref, ko_ref)
    ones_rows = jnp.ones((N_ROWS, ck), BF16)

    def tail(x, q0, new):
        return new if q0 == 0 else jnp.concatenate([x[:, :q0], new], axis=1)

    def qk_chunk(e, kt, c, mx, diag):
        off = pl.multiple_of(kt * tk + c * ck, ck)
        q0 = c * ck if diag else 0
        st = _dot_nt(k_refs[e][pl.ds(off, ck), :], qs[e][q0:, :])
        if diag:
            visible = (lax.broadcasted_iota(jnp.int32, st.shape, 0) <= lax.broadcasted_iota(jnp.int32, st.shape, 1))
            st = jnp.where(visible, st, -jnp.inf)
        st_ref[e, c * ck:(c + 1) * ck, q0:] = st
        cm = jnp.max(st, axis=0, keepdims=True)
        return cm if mx is None else tail(mx, q0, jnp.maximum(mx[:, q0:], cm))

    def pv_chunk(e, kt, c, state, diag):
        m, acc = state
        off = pl.multiple_of(kt * tk + c * ck, ck)
        q0 = c * ck if diag else 0
        pt = jnp.exp2(st_ref[e, c * ck:(c + 1) * ck, q0:] - m[:, q0:])
        v_aug = jnp.concatenate([vt_ref[e * dh:(e + 1) * dh, pl.ds(off, ck)], ones_rows], axis=0)
        return m, tail(acc, q0, acc[:, q0:] + _dot(v_aug, pt.astype(BF16)))

    def phase(cur, state, tile_max, nxt):
        m, acc = state
        m_new = jnp.maximum(m, tile_max)
        state = (m_new, jnp.exp2(m - m_new) * acc)
        mx = None
        for c in range(nchunk):
            mx = qk_chunk(nxt[0], nxt[1], c, mx, nxt[2])
            state = pv_chunk(cur[0], cur[1], c, state, cur[2])
        return state, mx

    init = (jnp.full((1, tq), -1e30, F32), jnp.zeros((dh + N_ROWS, tq), F32))
    mx0 = None
    for c in range(nchunk):
        mx0 = qk_chunk(0, i, c, mx0, True)
    s0, mx1 = phase((0, i, True), init, mx0, (1, i, True))
    s1, mx0 = phase((1, i, True), init, mx1, (0, 0, False))

    def trip(j, carry):
        s0, s1, mx0 = carry
        s0, mx1 = phase((0, j, False), s0, mx0, (1, j, False))
        s1, mx0 = phase((1, j, False), s1, mx1, (0, j + 1, False))
        return s0, s1, mx0

    s0, s1, _ = lax.fori_loop(0, i, trip, (s0, s1, mx0))
    ot = jnp.concatenate([acc[:dh] / acc[dh:dh + 1] for _, acc in (s0, s1)], axis=0)
    o_ref[...] = ot.T.astype(o_ref.dtype)


def fox_attn(q, eq, ke, ko, vt, bsz, seq):
    t, d = q.shape
    nh = B_HEADS
    dh2 = 2 * (d // nh)
    tq = min(TQ, seq)
    nq = seq // tq
    qspec = pl.BlockSpec((tq, dh2), lambda b, hp, i: (b * nq + i, hp))
    kspec = pl.BlockSpec((seq, dh2), lambda b, hp, i: (b, hp))
    vspec = pl.BlockSpec((dh2, seq), lambda b, hp, i: (hp, b))
    return pl.pallas_call(
        _fox_attn_kernel,
        grid=(bsz, nh // 2, nq),
        in_specs=[qspec, qspec, kspec, kspec, vspec],
        out_specs=qspec,
        out_shape=jax.ShapeDtypeStruct((t, d), BF16),
        scratch_shapes=[pltpu.VMEM((2, tq, tq), F32)],
        compiler_params=_cp(("arbitrary", "arbitrary", "arbitrary")), name="fox_attn",
    )(q, eq, ke, ko, vt)


def _route(logits, bias_col):
    aff = _sigmoid(logits)
    sel = aff + bias_col
    s = [sel[e:e + 1, :] for e in range(N_EXPERTS)]
    gs = []
    for g in range(N_GROUPS):
        v = s[EPG * g:EPG * (g + 1)]
        best = v[0] + v[1]
        for i in range(EPG):
            for j in range(i + 1, EPG):
                if (i, j) != (0, 1):
                    best = jnp.maximum(best, v[i] + v[j])
        gs.append(best)
    grp = jnp.zeros_like(gs[0], dtype=jnp.int32)
    best = gs[0]
    for g in range(1, N_GROUPS):
        upd = gs[g] > best
        grp = jnp.where(upd, g, grp)
        best = jnp.where(upd, gs[g], best)

    def pick(arrs, j):
        out = arrs[(N_GROUPS - 1) * EPG + j]
        for g in range(N_GROUPS - 2, -1, -1):
            out = jnp.where(grp == g, arrs[g * EPG + j], out)
        return out

    v = [pick(s, j) for j in range(EPG)]
    i1 = jnp.zeros_like(grp)
    b1 = v[0]
    for j in range(1, EPG):
        upd = v[j] > b1
        i1 = jnp.where(upd, j, i1)
        b1 = jnp.where(upd, v[j], b1)
    i2 = jnp.full_like(grp, -1)
    b2 = jnp.full_like(b1, -jnp.inf)
    for j in range(EPG):
        upd = (i1 != j) & ((i2 < 0) | (v[j] > b2))
        i2 = jnp.where(upd, j, i2)
        b2 = jnp.where(upd, v[j], b2)
    lo = jnp.minimum(i1, i2)
    hi = jnp.maximum(i1, i2)
    base = jnp.where(lo == 0, 0, jnp.where(lo == 1, 3, 5))
    return grp * N_PAIRS + base + (hi - lo - 1)


def _post_mix_kernel(is_mlstm, *refs):
    if is_mlstm:
        hs_ref, og_ref, mh_ref = refs[:3]
        refs = refs[3:]
        og = og_ref[...].astype(F32)
        mix = hs_ref[...].astype(F32) * mh_ref[...] * _sigmoid(og)
    else:
        mix = refs[0][...]
        refs = refs[1:]
    (w_ref, x_ref, gm_ref, nw_ref, sh_ref, sc_ref, wrt_ref, rb_ref, us_ref,
     xnew_ref, h2_ref, cls_ref, rank_ref, cnt_ref) = refs
    first = (pl.program_id(0) == 0) & (pl.program_id(1) == 0)

    @pl.when(first)
    def _():
        cnt_ref[...] = jnp.zeros_like(cnt_ref)

    y = _dot(mix.astype(BF16), w_ref[...])
    xnew = x_ref[...] + gm_ref[...] * y
    xnew_ref[...] = xnew
    h2 = _norm_mod(xnew, nw_ref[...], sh_ref[...], sc_ref[...])
    h2_ref[...] = h2.reshape(h2_ref.shape)
    h_hi = h2.astype(BF16)
    h_lo = (h2 - h_hi.astype(F32)).astype(BF16)
    ne = wrt_ref.shape[0] // 2
    part = _dot_nt(wrt_ref[...], h_hi)
    logits = part[:ne] + part[ne:] + _dot_nt(wrt_ref[:ne, :], h_lo)
    cls = _route(logits, rb_ref[...])
    cls_ref[...] = cls
    tm = cls.shape[1]
    onehot = (lax.broadcasted_iota(jnp.int32, (CLS_PAD, tm), 0) == cls).astype(F32)
    prefix = _dot(onehot.astype(BF16), us_ref[...])
    carry = cnt_ref[:, 0:1]
    rank = jnp.sum(onehot * (prefix + carry), axis=0, keepdims=True)
    rank_ref[...] = rank.astype(jnp.int32)
    cnt_ref[...] += jnp.sum(onehot, axis=1, keepdims=True)


def post_mix(mix_args, is_mlstm, w_o, x, gm, nw, shift, scale, w_router, router_bias, bsz, seq):
    t, d = x.shape
    tm = min(TM, seq)
    ns = seq // tm
    ne = N_EXPERTS
    idx = jnp.arange(tm)
    upper_strict = (idx[:, None] < idx[None, :]).astype(BF16)
    row = pl.BlockSpec((tm, d), lambda b, s: (b * ns + s, 0))
    vec = pl.BlockSpec((None, 1, d), lambda b, s: (b, 0, 0))
    lane_row = pl.BlockSpec((None, 1, tm), lambda b, s: (b, 0, s))
    if is_mlstm:
        mix_specs = [row, row, _full((1, d))]
    else:
        mix_specs = [row]
    in_specs = mix_specs + [_full((d, d)), row, vec, _full((1, d)), vec, vec, _full((2 * ne, d)), _full((ne, 1)),
                            _full((tm, tm))]
    wr = w_router.T
    wr_hi = wr.astype(BF16)
    wr_split = jnp.concatenate([wr_hi, (wr - wr_hi.astype(F32)).astype(BF16)], axis=0)
    row3 = pl.BlockSpec((tm, d // LANES, LANES), lambda b, s: (b * ns + s, 0, 0))
    out_specs = [row, row3, lane_row, lane_row, _full((CLS_PAD, 128))]
    out_shape = [jax.ShapeDtypeStruct((t, d), F32), jax.ShapeDtypeStruct((t, d // LANES, LANES), F32),
                 jax.ShapeDtypeStruct((bsz, 1, seq), jnp.int32), jax.ShapeDtypeStruct((bsz, 1, seq), jnp.int32),
                 jax.ShapeDtypeStruct((CLS_PAD, 128), F32)]
    return pl.pallas_call(
        functools.partial(_post_mix_kernel, is_mlstm),
        grid=(bsz, ns), in_specs=in_specs, out_specs=out_specs, out_shape=out_shape,
        compiler_params=_cp(("arbitrary", "arbitrary")), name="post_mix",
    )(*mix_args, w_o.astype(BF16), x, gm, nw, shift, scale, wr_split, router_bias.reshape(ne, 1), upper_strict)


def _row_scatter_kernel(rb, dest_ref, src_ref, dst_in_ref, dst_ref, sem):
    del dst_in_ref
    base = pl.program_id(0) * rb

    def start(rr, carry):
        pltpu.make_async_copy(src_ref.at[rr], dst_ref.at[dest_ref[base + rr]], sem).start()
        return carry

    def wait(rr, carry):
        pltpu.make_async_copy(src_ref.at[0], dst_ref.at[0], sem).wait()
        return carry

    lax.fori_loop(0, rb, start, 0, unroll=8)
    lax.fori_loop(0, rb, wait, 0, unroll=8)


def row_scatter(dest, src, n_rows):
    t = src.shape[0]
    rb = min(RB, t)
    dst0 = jnp.zeros((n_rows,) + src.shape[1:], src.dtype)
    return pl.pallas_call(
        functools.partial(_row_scatter_kernel, rb),
        grid_spec=pltpu.PrefetchScalarGridSpec(
            num_scalar_prefetch=1, grid=(t // rb,),
            in_specs=[pl.BlockSpec((rb,) + src.shape[1:], lambda i, dest: (i, 0, 0)),
                      pl.BlockSpec(memory_space=pl.ANY)],
            out_specs=pl.BlockSpec(memory_space=pl.ANY),
            scratch_shapes=[pltpu.SemaphoreType.DMA]),
        out_shape=jax.ShapeDtypeStruct(dst0.shape, src.dtype),
        input_output_aliases={2: 0},
        compiler_params=_cp(("arbitrary",)), name="row_scatter",
    )(dest, src, dst0)


def _row_gather_kernel(rb, dest_ref, src_ref, out_ref, sem):
    base = pl.program_id(0) * rb

    def start(rr, carry):
        pltpu.make_async_copy(src_ref.at[dest_ref[base + rr]], out_ref.at[rr], sem).start()
        return carry

    def wait(rr, carry):
        pltpu.make_async_copy(src_ref.at[0], out_ref.at[0], sem).wait()
        return carry

    lax.fori_loop(0, rb, start, 0, unroll=8)
    lax.fori_loop(0, rb, wait, 0, unroll=8)


def row_gather(dest, src, t):
    rb = min(RB, t)
    return pl.pallas_call(
        functools.partial(_row_gather_kernel, rb),
        grid_spec=pltpu.PrefetchScalarGridSpec(
            num_scalar_prefetch=1, grid=(t // rb,),
            in_specs=[pl.BlockSpec(memory_space=pl.ANY)],
            out_specs=pl.BlockSpec((rb,) + src.shape[1:], lambda i, dest: (i, 0, 0)),
            scratch_shapes=[pltpu.SemaphoreType.DMA]),
        out_shape=jax.ShapeDtypeStruct((t,) + src.shape[1:], src.dtype),
        compiler_params=_cp(("arbitrary",)), name="row_gather",
    )(dest, src)


def _experts_kernel(elo_ref, ehi_ref, nused_ref, x_ref, wrt_ref,
                    wg_lo, wu_lo, wd_lo, wg_hi, wu_hi, wd_hi, y_ref):
    i = pl.program_id(0)

    @pl.when(i < nused_ref[0])
    def _():
        x = x_ref[...].reshape(x_ref.shape[0], -1)
        xb = x.astype(BF16)
        aff_lo = _sigmoid(jnp.sum(x * wrt_ref[pl.ds(elo_ref[i], 1), :], axis=1, keepdims=True))
        aff_hi = _sigmoid(jnp.sum(x * wrt_ref[pl.ds(ehi_ref[i], 1), :], axis=1, keepdims=True))
        tot = aff_lo + aff_hi
        acc = None
        for gate, wg, wu, wd in ((aff_lo / tot, wg_lo, wu_lo, wd_lo), (aff_hi / tot, wg_hi, wu_hi, wd_hi)):
            hg = _dot(xb, wg[...])
            hu = _dot(xb, wu[...])
            act = hg * _sigmoid(hg) * hu * gate
            part = _dot(act.astype(BF16), wd[...])
            acc = part if acc is None else acc + part
        y_ref[...] = acc.reshape(y_ref.shape)

    @pl.when(i >= nused_ref[0])
    def _():
        y_ref[...] = jnp.zeros_like(y_ref)


def experts(xs, w_router_t, blk_lo, blk_hi, nused, w_gate, w_up, w_down):
    p = xs.shape[0]
    d = xs.shape[1] * xs.shape[2]
    nblk = p // MB
    de = w_gate.shape[2]
    xrow = pl.BlockSpec((MB,) + xs.shape[1:], lambda i, lo, hi, nu: (i, 0, 0))
    wrt = pl.BlockSpec(w_router_t.shape, lambda i, lo, hi, nu: (0, 0))
    w_in_lo = pl.BlockSpec((None, d, de), lambda i, lo, hi, nu: (lo[i], 0, 0))
    w_in_hi = pl.BlockSpec((None, d, de), lambda i, lo, hi, nu: (hi[i], 0, 0))
    w_out_lo = pl.BlockSpec((None, de, d), lambda i, lo, hi, nu: (lo[i], 0, 0))
    w_out_hi = pl.BlockSpec((None, de, d), lambda i, lo, hi, nu: (hi[i], 0, 0))
    return pl.pallas_call(
        _experts_kernel,
        grid_spec=pltpu.PrefetchScalarGridSpec(
            num_scalar_prefetch=3, grid=(nblk,),
            in_specs=[xrow, wrt, w_in_lo, w_in_lo, w_out_lo, w_in_hi, w_in_hi, w_out_hi],
            out_specs=xrow),
        out_shape=jax.ShapeDtypeStruct(xs.shape, F32),
        compiler_params=_cp(("arbitrary",)), name="experts",
    )(blk_lo, blk_hi, nused, xs, w_router_t, w_gate, w_up, w_down, w_gate, w_up, w_down)


_PAIR_LO = (0, 0, 0, 1, 1, 2)
_PAIR_HI = (1, 2, 3, 2, 3, 3)


def moe(h2, cls, rank, counts, w_router_t, w_gate, w_up, w_down):
    t = h2.shape[0]
    p = t + N_CLASSES * MB
    nblk = p // MB
    cls = cls.reshape(t)
    counts = counts[:N_CLASSES, 0].astype(jnp.int32)
    padded = (counts + MB - 1) // MB * MB
    pad_end = jnp.cumsum(padded)
    pad_start = pad_end - padded
    dest = (pad_start[cls] + rank.reshape(t)).astype(jnp.int32)
    blk_row = jnp.arange(nblk, dtype=jnp.int32) * MB
    blk_cls = jnp.minimum(jnp.sum((pad_end[None, :] <= blk_row[:, None]).astype(jnp.int32), axis=1), N_CLASSES - 1)
    grp = blk_cls // N_PAIRS
    pr = blk_cls % N_PAIRS
    blk_lo = grp * EPG + jnp.asarray(_PAIR_LO, jnp.int32)[pr]
    blk_hi = grp * EPG + jnp.asarray(_PAIR_HI, jnp.int32)[pr]
    nused = (pad_end[-1:] // MB).astype(jnp.int32)
    xs = row_scatter(dest, h2, p)
    ys = experts(xs, w_router_t, blk_lo, blk_hi, nused, w_gate, w_up, w_down)
    return row_gather(dest, ys, t)


def _final_kernel(*refs):
    _, h, rest = _prologue(True, refs)
    rest[0][...] = h


def final_norm(x, moe_out, gf, nw, shift, scale, bsz, seq):
    t, d = x.shape
    tm = min(TM, seq)
    ns = seq // tm
    row = pl.BlockSpec((tm, d), lambda b, s: (b * ns + s, 0))
    return pl.pallas_call(
        _final_kernel, grid=(bsz, ns), in_specs=_prologue_specs(True, ns, d, tm), out_specs=row,
        out_shape=jax.ShapeDtypeStruct((t, d), F32),
        compiler_params=_cp(("arbitrary", "arbitrary")), name="final_norm",
    )(x, moe_out, gf, nw, shift, scale)


def kernel(x, c, a_w_in, a_b_gates, a_mh_norm, a_w_out, kv_norm, w_ada_kv, b_ada_kv, w_kv, b_fgate, b_w_q, b_w_o,
           norm_mix, norm_ffn, w_ada, b_ada, w_router, router_bias, w_gate, w_up, w_down, norm_final, w_ada_final,
           b_ada_final):
    bsz, seq, d = x.shape
    depth = w_ada.shape[0]
    n_a = a_w_in.shape[0]
    t = bsz * seq
    xf = x.reshape(t, d)

    mods = ada(c, w_ada, b_ada)
    kv_mod = ada(c, w_ada_kv[None], b_ada_kv[None])[0]
    fin = ada(c, w_ada_final[None], b_ada_final[None])[0]

    def vecs(m, n):
        return [m[:, None, i * d:(i + 1) * d] for i in range(n)]

    wg_b = w_gate.astype(BF16)
    wu_b = w_up.astype(BF16)
    wd_b = w_down.astype(BF16)

    moe_out = None
    gf_prev = None
    kv = None
    for layer in range(depth):
        sh_m, sc_m, g_m, sh_f, sc_f, g_f = vecs(mods[layer], 6)
        nm = norm_mix[layer].reshape(1, d)
        if layer < n_a:
            xf, (q, k, v, o, grow, gcol) = mlstm_in(xf, moe_out, gf_prev, nm, sh_m, sc_m, a_w_in[layer],
                                                     a_b_gates[layer], bsz, seq)
            hs = mlstm_cell(q, k, v, grow, gcol, bsz, seq)
            mix_args = (hs, o, a_mh_norm[layer].reshape(1, d))
            w_o = a_w_out[layer]
        else:
            j = layer - n_a
            xf, q = fox_q(xf, moe_out, gf_prev, nm, sh_m, sc_m, b_w_q[j], bsz, seq)
            if layer == n_a:
                kv_sh, kv_sc = vecs(kv_mod, 2)
                kv = shared_kv(xf, kv_norm.reshape(1, d), kv_sh, kv_sc, w_kv, b_fgate, bsz, seq)
            ke, ko, vt, eq = kv
            att = fox_attn(q, eq, ke, ko, vt, bsz, seq)
            mix_args = (att,)
            w_o = b_w_o[j]
        xf, h2, cls, rank, counts = post_mix(
            mix_args, layer < n_a, w_o, xf, g_m, norm_ffn[layer].reshape(1, d), sh_f, sc_f, w_router, router_bias,
            bsz, seq)
        moe_out = moe(h2, cls, rank, counts, w_router.T, wg_b[layer], wu_b[layer], wd_b[layer])
        gf_prev = g_f
    fin_sh, fin_sc = vecs(fin, 2)
    out = final_norm(xf, moe_out, gf_prev, norm_final.reshape(1, d), fin_sh, fin_sc, bsz, seq)
    return out.reshape(bsz, seq, d)
```

```python
import functools

import jax
import jax.numpy as jnp
from jax import lax
from jax.experimental import pallas as pl
from jax.experimental.pallas import tpu as pltpu

F32 = jnp.float32
BF16 = jnp.bfloat16
EPS = 1e-6
GATE_SOFTCAP = 15.0

A_HEADS = 8
B_HEADS = 16
N_EXPERTS = 16
N_GROUPS = 4
EPG = N_EXPERTS // N_GROUPS
N_PAIRS = 6
N_CLASSES = N_GROUPS * N_PAIRS
CLS_PAD = 32

VMEM_LIMIT = 56 * 1024 * 1024

TM = 512
A_L = 256
TQ = 1024
ATT_CHUNK = 256
MB = 256
RB = 2048
DMA_GROUP = 8
DMA_QUEUES = 2
LANES = 128
LOG2E = 1.4426950408889634

HIGHEST = lax.Precision.HIGHEST


def _cp(sem):
    return pltpu.CompilerParams(dimension_semantics=sem, vmem_limit_bytes=VMEM_LIMIT)


def _dot(a, b, precision=None):
    return jnp.dot(a, b, preferred_element_type=F32, precision=precision)


def _dot_nt(a, b, precision=None):
    return lax.dot_general(a, b, (((1,), (1,)), ((), ())), preferred_element_type=F32, precision=precision)


def _dot_tn(a, b, precision=None):
    return lax.dot_general(a, b, (((0,), (0,)), ((), ())), preferred_element_type=F32, precision=precision)


def _norm_mod(xin, nw, shift, scale):
    ms = jnp.mean(xin * xin, axis=-1, keepdims=True)
    y = xin * lax.rsqrt(ms + EPS)
    return (y * nw) * (1.0 + scale) + shift


def _log_sigmoid(x):
    return jnp.minimum(x, 0.0) - jnp.log1p(jnp.exp(-jnp.abs(x)))


def _sigmoid(x):
    return 1.0 / (1.0 + jnp.exp(-x))


def _ada_kernel(c_ref, w_ref, b_ref, o_ref):
    c = c_ref[...]
    ca = c * _sigmoid(c)
    o_ref[...] = _dot(ca, w_ref[...], HIGHEST) + b_ref[...]


def ada(c, w, b):
    nl, d, n = w.shape
    bsz = c.shape[0]
    tn = 512
    return pl.pallas_call(
        _ada_kernel,
        grid=(nl, n // tn),
        in_specs=[
            pl.BlockSpec((bsz, d), lambda l, j: (0, 0)),
            pl.BlockSpec((None, d, tn), lambda l, j: (l, 0, j)),
            pl.BlockSpec((None, 1, tn), lambda l, j: (l, 0, j)),
        ],
        out_specs=pl.BlockSpec((None, bsz, tn), lambda l, j: (l, 0, j)),
        out_shape=jax.ShapeDtypeStruct((nl, bsz, n), F32),
        compiler_params=_cp(("arbitrary", "arbitrary")),
        name="ada",
    )(c, w, b.reshape(nl, 1, n))


def _prologue(has_moe, refs):
    if has_moe:
        x_ref, moe_ref, gf_ref, nw_ref, sh_ref, sc_ref = refs[:6]
        rest = refs[6:]
        xin = x_ref[...] + gf_ref[...] * moe_ref[...].reshape(x_ref.shape)
    else:
        x_ref, nw_ref, sh_ref, sc_ref = refs[:4]
        rest = refs[4:]
        xin = x_ref[...]
    h = _norm_mod(xin, nw_ref[...], sh_ref[...], sc_ref[...])
    return xin, h, rest


def _prologue_specs(has_moe, ns, d, tm):
    row = pl.BlockSpec((tm, d), lambda b, s: (b * ns + s, 0))
    vec = pl.BlockSpec((None, 1, d), lambda b, s: (b, 0, 0))
    one = pl.BlockSpec((1, d), lambda b, s: (0, 0))
    if has_moe:
        row3 = pl.BlockSpec((tm, d // LANES, LANES), lambda b, s: (b * ns + s, 0, 0))
        return [row, row3, vec, one, vec, vec]
    return [row, one, vec, vec]


def _full(shape):
    nd = len(shape)
    return pl.BlockSpec(shape, lambda b, s: (0,) * nd)


def _mlstm_in_kernel(has_moe, dqk, *refs):
    xin, h, rest = _prologue(has_moe, refs)
    wqt_ref, wk_ref, wvt_ref, wo_ref, wgt_ref, wg_ref, bcol_ref, brow_ref = rest[:8]
    outs = rest[8:]
    if has_moe:
        xnew_ref = outs[0]
        outs = outs[1:]
        xnew_ref[...] = xin
    qt_ref, k_ref, vt_ref, o_ref, grow_ref, gcol_ref = outs
    hb = h.astype(BF16)
    scale = float(dqk // A_HEADS) ** -0.5
    qt_ref[...] = (_dot_nt(wqt_ref[...], hb) * scale).astype(BF16)
    k_ref[...] = _dot(hb, wk_ref[...]).astype(BF16)
    vt_ref[...] = _dot_nt(wvt_ref[...], hb).astype(BF16)
    o_ref[...] = _dot(hb, wo_ref[...]).astype(BF16)
    gr = _dot_nt(wgt_ref[...], hb) + bcol_ref[...]
    gr = GATE_SOFTCAP * jnp.tanh(gr / GATE_SOFTCAP)
    ridx = lax.broadcasted_iota(jnp.int32, gr.shape, 0)
    grow_ref[...] = jnp.where(ridx < A_HEADS, gr, _log_sigmoid(gr))
    gc = _dot(hb, wg_ref[...]) + brow_ref[...]
    gc = GATE_SOFTCAP * jnp.tanh(gc / GATE_SOFTCAP)
    cidx = lax.broadcasted_iota(jnp.int32, gc.shape, 1)
    gcol_ref[...] = jnp.where(cidx < A_HEADS, gc, _log_sigmoid(gc))


def mlstm_in(x, moe, gf, nw, shift, scale, w_in, b_gates, bsz, seq):
    t, d = x.shape
    dqk, dv = d // 2, d
    has_moe = moe is not None
    tm = min(TM, seq)
    ns = seq // tm
    w_b = w_in.astype(BF16)
    w_qt = w_b[:, 0:dqk].T
    w_k = w_b[:, dqk:2 * dqk]
    w_vt = w_b[:, 2 * dqk:2 * dqk + dv].T
    w_o = w_b[:, 2 * dqk + dv:2 * dqk + 2 * dv]
    w_g = w_b[:, 2 * dqk + 2 * dv:]
    ng = 2 * A_HEADS
    args = [x] + ([moe, gf] if has_moe else []) + [nw, shift, scale, w_qt, w_k, w_vt, w_o, w_g.T, w_g,
                                                    b_gates.reshape(ng, 1), b_gates.reshape(1, ng)]
    in_specs = _prologue_specs(has_moe, ns, d, tm) + [
        _full((dqk, d)), _full((d, dqk)), _full((dv, d)), _full((d, dv)), _full((ng, d)), _full((d, ng)),
        _full((ng, 1)), _full((1, ng))]
    row = lambda n: pl.BlockSpec((tm, n), lambda b, s: (b * ns + s, 0))
    col = lambda n: pl.BlockSpec((n, tm), lambda b, s: (0, b * ns + s))
    out_specs = [col(dqk), row(dqk), col(dv), row(dv),
                 pl.BlockSpec((None, ng, tm), lambda b, s: (b, 0, s)),
                 pl.BlockSpec((None, tm, ng), lambda b, s: (b, s, 0))]
    out_shape = [jax.ShapeDtypeStruct((dqk, t), BF16), jax.ShapeDtypeStruct((t, dqk), BF16),
                 jax.ShapeDtypeStruct((dv, t), BF16), jax.ShapeDtypeStruct((t, dv), BF16),
                 jax.ShapeDtypeStruct((bsz, ng, seq), F32), jax.ShapeDtypeStruct((bsz, seq, ng), F32)]
    if has_moe:
        out_specs = [row(d)] + out_specs
        out_shape = [jax.ShapeDtypeStruct((t, d), F32)] + out_shape
    res = pl.pallas_call(
        functools.partial(_mlstm_in_kernel, has_moe, dqk),
        grid=(bsz, ns), in_specs=in_specs, out_specs=out_specs, out_shape=out_shape,
        compiler_params=_cp(("arbitrary", "arbitrary")), name="mlstm_in",
    )(*args)
    if has_moe:
        return res[0], res[1:]
    return x, res


N_ROWS = 16


def _mlstm_cell_kernel(qt_ref, k_ref, vt_ref, grow_ref, gcol_ref, hs_ref, ct_ref, m_ref):
    c = pl.program_id(1)
    L = k_ref.shape[0]
    dk = k_ref.shape[1] // A_HEADS
    dvh = vt_ref.shape[0] // A_HEADS

    @pl.when(c == 0)
    def _():
        ct_ref[...] = jnp.zeros_like(ct_ref)
        m_ref[...] = jnp.zeros_like(m_ref)

    r = lax.broadcasted_iota(jnp.int32, (L, L), 0)
    cc = lax.broadcasted_iota(jnp.int32, (L, L), 1)
    visible = r <= cc
    upper = visible.astype(F32)
    lower = (cc <= r).astype(F32)
    grow = grow_ref[...]
    gcol = gcol_ref[...]
    b_row_all = _dot(grow[A_HEADS:, :], upper, HIGHEST)
    b_col_all = _dot(lower, gcol[:, A_HEADS:], HIGHEST)
    x_col_all = gcol[:, :A_HEADS] - b_col_all
    ones_rows = jnp.ones((N_ROWS, L), BF16)

    for h in range(A_HEADS):
        qt = qt_ref[h * dk:(h + 1) * dk, :]
        kh = k_ref[:, h * dk:(h + 1) * dk]
        v_aug = jnp.concatenate([vt_ref[h * dvh:(h + 1) * dvh, :], ones_rows], axis=0)
        br = b_row_all[h:h + 1, :]
        igr = grow[h:h + 1, :]
        xc = x_col_all[:, h:h + 1]
        m_prev = m_ref[h:h + 1, 0:1]
        ct = ct_ref[h]

        d_intra = jnp.where(visible, br + xc, -jnp.inf)
        d_inter = br + m_prev
        m_t = jnp.maximum(d_inter, jnp.max(d_intra, axis=0, keepdims=True))
        w = jnp.exp(d_intra - m_t) * _dot(kh, qt)
        a = jnp.exp(d_inter - m_t)
        tot = _dot(v_aug, w.astype(BF16)) + a * _dot(ct.astype(BF16), qt)
        num = tot[:dvh, :]
        den = tot[dvh:dvh + 1, :]
        hh = num / jnp.maximum(jnp.abs(den), jnp.exp(-m_t))
        hh = hh * lax.rsqrt(jnp.mean(hh * hh, axis=0, keepdims=True) + EPS)
        hs_ref[:, h * dvh:(h + 1) * dvh] = hh.T.astype(hs_ref.dtype)

        b_last = br[:, L - 1:L]
        g = b_last - br + igr
        m_new = jnp.maximum(b_last + m_prev, jnp.max(g, axis=1, keepdims=True))
        wk = jnp.exp(g - m_new)
        decay = jnp.exp(b_last + m_prev - m_new)
        ct_ref[h] = decay * ct + _dot((v_aug.astype(F32) * wk).astype(BF16), kh)
        m_ref[h:h + 1, :] = jnp.broadcast_to(m_new, (1, m_ref.shape[1]))


def mlstm_cell(qt, k, vt, grow, gcol, bsz, seq):
    t, dqk = k.shape
    dv = vt.shape[0]
    L = min(A_L, seq)
    nc = seq // L
    ng = grow.shape[1]
    dk = dqk // A_HEADS
    dvh = dv // A_HEADS
    return pl.pallas_call(
        _mlstm_cell_kernel,
        grid=(bsz, nc),
        in_specs=[pl.BlockSpec((dqk, L), lambda b, c: (0, b * nc + c)),
                  pl.BlockSpec((L, dqk), lambda b, c: (b * nc + c, 0)),
                  pl.BlockSpec((dv, L), lambda b, c: (0, b * nc + c)),
                  pl.BlockSpec((None, ng, L), lambda b, c: (b, 0, c)),
                  pl.BlockSpec((None, L, ng), lambda b, c: (b, c, 0))],
        out_specs=pl.BlockSpec((L, dv), lambda b, c: (b * nc + c, 0)),
        out_shape=jax.ShapeDtypeStruct((t, dv), BF16),
        scratch_shapes=[pltpu.VMEM((A_HEADS, dvh + N_ROWS, dk), F32), pltpu.VMEM((A_HEADS, 128), F32)],
        compiler_params=_cp(("arbitrary", "arbitrary")), name="mlstm_cell",
    )(qt, k, vt, grow, gcol)


def _fox_q_kernel(has_moe, qscale, *refs):
    xin, h, rest = _prologue(has_moe, refs)
    w_ref = rest[0]
    outs = rest[1:]
    if has_moe:
        outs[0][...] = xin
        outs = outs[1:]
    outs[0][...] = (_dot(h.astype(BF16), w_ref[...]) * qscale).astype(BF16)


def fox_q(x, moe, gf, nw, shift, scale, w_q, bsz, seq):
    t, d = x.shape
    has_moe = moe is not None
    tm = min(TM, seq)
    ns = seq // tm
    args = [x] + ([moe, gf] if has_moe else []) + [nw, shift, scale, w_q.astype(BF16)]
    in_specs = _prologue_specs(has_moe, ns, d, tm) + [_full((d, d))]
    row = pl.BlockSpec((tm, d), lambda b, s: (b * ns + s, 0))
    out_specs = [row]
    out_shape = [jax.ShapeDtypeStruct((t, d), BF16)]
    if has_moe:
        out_specs = [row] + out_specs
        out_shape = [jax.ShapeDtypeStruct((t, d), F32)] + out_shape
    qscale = float(d // B_HEADS) ** -0.5 * LOG2E
    res = pl.pallas_call(
        functools.partial(_fox_q_kernel, has_moe, qscale),
        grid=(bsz, ns), in_specs=in_specs, out_specs=out_specs, out_shape=out_shape,
        compiler_params=_cp(("arbitrary", "arbitrary")), name="fox_q",
    )(*args)
    if has_moe:
        return res[0], res[1]
    return x, res[0]


N_BIAS_PIECES = 3


def _bias_select_matrices(d, n_heads):
    import numpy as np
    dh = d // n_heads
    pq = np.zeros((N_BIAS_PIECES, n_heads, d), np.float32)
    pk = np.zeros((N_BIAS_PIECES, n_heads, d), np.float32)
    oq = np.zeros((1, d), np.float32)
    ok = np.zeros((1, d), np.float32)
    for h in range(n_heads):
        base = (h // 2) * 2 * dh + (dh if h % 2 == 0 else 0)
        for p in range(N_BIAS_PIECES):
            pq[p, h, base + p] = 1.0
            ok[0, base + p] = 1.0
            oq[0, base + N_BIAS_PIECES + p] = 1.0
            pk[p, h, base + N_BIAS_PIECES + p] = -1.0
    return pq, pk, oq, ok


def _split3(f):
    a = f.astype(BF16)
    r1 = f - a.astype(F32)
    b = r1.astype(BF16)
    c = (r1 - b.astype(F32)).astype(BF16)
    return a, b, c


def _shared_kv_kernel(x_ref, nw_ref, sh_ref, sc_ref, w_ref, wvt_ref, wf_ref, brow_ref, lo_ref, pq_ref, pk_ref, oq_ref,
                      ok_ref, ke_ref, ko_ref, vt_ref, eq_ref, cc_ref):
    s = pl.program_id(1)
    d = x_ref.shape[1]
    dh = d // B_HEADS

    @pl.when(s == 0)
    def _():
        cc_ref[...] = jnp.zeros_like(cc_ref)

    h = _norm_mod(x_ref[...], nw_ref[...], sh_ref[...], sc_ref[...])
    hb = h.astype(BF16)
    lc = _log_sigmoid(_dot(hb, wf_ref[...]) + brow_ref[...])
    fcum = _dot(lo_ref[...], lc, HIGHEST) + cc_ref[...]
    cc_ref[...] += jnp.sum(lc, axis=0, keepdims=True)
    pieces = _split3(fcum * LOG2E)
    eq = oq_ref[...]
    ek = ok_ref[...]
    for p in range(N_BIAS_PIECES):
        eq = eq + _dot(pieces[p], pq_ref[p])
        ek = ek + _dot(pieces[p], pk_ref[p])
    eq_ref[...] = eq.astype(BF16)
    even = (lax.broadcasted_iota(jnp.int32, (1, d), 1) % (2 * dh)) < dh
    k = _dot(hb, w_ref[...])
    ke_ref[...] = jnp.where(even, k, ek).astype(BF16)
    ko_ref[...] = jnp.where(even, ek, k).astype(BF16)
    vt_ref[...] = _dot_nt(wvt_ref[...], hb).astype(BF16)


def shared_kv(x, nw, shift, scale, w_kv, b_fgate, bsz, seq):
    t, d = x.shape
    tm = min(TM, seq)
    ns = seq // tm
    nh = B_HEADS
    w_k = w_kv[:, :d].astype(BF16)
    w_vt = w_kv[:, d:2 * d].astype(BF16).T
    w_f = w_kv[:, 2 * d:].astype(BF16)
    idx = jnp.arange(tm)
    lower = (idx[None, :] <= idx[:, None]).astype(F32)
    pq, pk, oq, ok = _bias_select_matrices(d, nh)
    row = pl.BlockSpec((tm, d), lambda b, s: (b * ns + s, 0))
    vec = pl.BlockSpec((None, 1, d), lambda b, s: (b, 0, 0))
    sel = _full((N_BIAS_PIECES, nh, d))
    return pl.pallas_call(
        _shared_kv_kernel,
        grid=(bsz, ns),
        in_specs=[row, _full((1, d)), vec, vec, _full((d, d)), _full((d, d)), _full((d, nh)), _full((1, nh)),
                  _full((tm, tm)), sel, sel, _full((1, d)), _full((1, d))],
        out_specs=[row, row, pl.BlockSpec((d, tm), lambda b, s: (0, b * ns + s)), row],
        out_shape=[jax.ShapeDtypeStruct((t, d), BF16), jax.ShapeDtypeStruct((t, d), BF16),
                   jax.ShapeDtypeStruct((d, t), BF16), jax.ShapeDtypeStruct((t, d), BF16)],
        scratch_shapes=[pltpu.VMEM((1, nh), F32)],
        compiler_params=_cp(("arbitrary", "arbitrary")), name="shared_kv",
    )(x, nw, shift, scale, w_k, w_vt, w_f, b_fgate.reshape(1, nh), lower,
      jnp.asarray(pq, BF16), jnp.asarray(pk, BF16), jnp.asarray(oq), jnp.asarray(ok))


def _fox_attn_kernel(q_ref, eq_ref, ke_ref, ko_ref, vt_ref, o_ref, st_ref):
    i = pl.program_id(2)
    tq = q_ref.shape[0]
    tk = tq
    ck = min(ATT_CHUNK, tk)
    nchunk = tk // ck
    dh2 = q_ref.shape[1]
    dh = dh2 // 2
    lo_mask = lax.broadcasted_iota(jnp.int32, (1, dh2), 1) < dh
    q = q_ref[...]
    eq = eq_ref[...]
    qs = (jnp.where(lo_mask, q, eq), jnp.where(lo_mask, eq, q))
    k_refs = (ke_ref, ko_ref)
    ones_rows = jnp.ones((N_ROWS, ck), BF16)

    def tail(x, q0, new):
        return new if q0 == 0 else jnp.concatenate([x[:, :q0], new], axis=1)

    def qk_chunk(e, kt, c, mx, diag):
        off = pl.multiple_of(kt * tk + c * ck, ck)
        q0 = c * ck if diag else 0
        st = _dot_nt(k_refs[e][pl.ds(off, ck), :], qs[e][q0:, :])
        if diag:
            visible = (lax.broadcasted_iota(jnp.int32, st.shape, 0) <= lax.broadcasted_iota(jnp.int32, st.shape, 1))
            st = jnp.where(visible, st, -jnp.inf)
        st_ref[e, c * ck:(c + 1) * ck, q0:] = st
        cm = jnp.max(st, axis=0, keepdims=True)
        return cm if mx is None else tail(mx, q0, jnp.maximum(mx[:, q0:], cm))

    def pv_chunk(e, kt, c, state, diag):
        m, acc = state
        off = pl.multiple_of(kt * tk + c * ck, ck)
        q0 = c * ck if diag else 0
        pt = jnp.exp2(st_ref[e, c * ck:(c + 1) * ck, q0:] - m[:, q0:])
        v_aug = jnp.concatenate([vt_ref[e * dh:(e + 1) * dh, pl.ds(off, ck)], ones_rows], axis=0)
        return m, tail(acc, q0, acc[:, q0:] + _dot(v_aug, pt.astype(BF16)))

    def phase(cur, state, tile_max, nxt):
        m, acc = state
        m_new = jnp.maximum(m, tile_max)
        state = (m_new, jnp.exp2(m - m_new) * acc)
        mx = None
        for c in range(nchunk):
            mx = qk_chunk(nxt[0], nxt[1], c, mx, nxt[2])
            state = pv_chunk(cur[0], cur[1], c, state, cur[2])
        return state, mx

    init = (jnp.full((1, tq), -1e30, F32), jnp.zeros((dh + N_ROWS, tq), F32))
    mx0 = None
    for c in range(nchunk):
        mx0 = qk_chunk(0, i, c, mx0, True)
    s0, mx1 = phase((0, i, True), init, mx0, (1, i, True))
    s1, mx0 = phase((1, i, True), init, mx1, (0, 0, False))

    def trip(j, carry):
        s0, s1, mx0 = carry
        s0, mx1 = phase((0, j, False), s0, mx0, (1, j, False))
        s1, mx0 = phase((1, j, False), s1, mx1, (0, j + 1, False))
        return s0, s1, mx0

    s0, s1, _ = lax.fori_loop(0, i, trip, (s0, s1, mx0))
    ot = jnp.concatenate([acc[:dh] / acc[dh:dh + 1] for _, acc in (s0, s1)], axis=0)
    o_ref[...] = ot.T.astype(o_ref.dtype)


def fox_attn(q, eq, ke, ko, vt, bsz, seq):
    t, d = q.shape
    nh = B_HEADS
    dh2 = 2 * (d // nh)
    tq = min(TQ, seq)
    nq = seq // tq
    qspec = pl.BlockSpec((tq, dh2), lambda b, hp, i: (b * nq + i, hp))
    kspec = pl.BlockSpec((seq, dh2), lambda b, hp, i: (b, hp))
    vspec = pl.BlockSpec((dh2, seq), lambda b, hp, i: (hp, b))
    return pl.pallas_call(
        _fox_attn_kernel,
        grid=(bsz, nh // 2, nq),
        in_specs=[qspec, qspec, kspec, kspec, vspec],
        out_specs=qspec,
        out_shape=jax.ShapeDtypeStruct((t, d), BF16),
        scratch_shapes=[pltpu.VMEM((2, tq, tq), F32)],
        compiler_params=_cp(("arbitrary", "arbitrary", "arbitrary")), name="fox_attn",
    )(q, eq, ke, ko, vt)


def _route(logits, bias_col):
    aff = _sigmoid(logits)
    sel = aff + bias_col
    s = [sel[e:e + 1, :] for e in range(N_EXPERTS)]
    gs = []
    for g in range(N_GROUPS):
        v = s[EPG * g:EPG * (g + 1)]
        best = v[0] + v[1]
        for i in range(EPG):
            for j in range(i + 1, EPG):
                if (i, j) != (0, 1):
                    best = jnp.maximum(best, v[i] + v[j])
        gs.append(best)
    grp = jnp.zeros_like(gs[0], dtype=jnp.int32)
    best = gs[0]
    for g in range(1, N_GROUPS):
        upd = gs[g] > best
        grp = jnp.where(upd, g, grp)
        best = jnp.where(upd, gs[g], best)

    def pick(arrs, j):
        out = arrs[(N_GROUPS - 1) * EPG + j]
        for g in range(N_GROUPS - 2, -1, -1):
            out = jnp.where(grp == g, arrs[g * EPG + j], out)
        return out

    v = [pick(s, j) for j in range(EPG)]
    i1 = jnp.zeros_like(grp)
    b1 = v[0]
    for j in range(1, EPG):
        upd = v[j] > b1
        i1 = jnp.where(upd, j, i1)
        b1 = jnp.where(upd, v[j], b1)
    i2 = jnp.full_like(grp, -1)
    b2 = jnp.full_like(b1, -jnp.inf)
    for j in range(EPG):
        upd = (i1 != j) & ((i2 < 0) | (v[j] > b2))
        i2 = jnp.where(upd, j, i2)
        b2 = jnp.where(upd, v[j], b2)
    lo = jnp.minimum(i1, i2)
    hi = jnp.maximum(i1, i2)
    base = jnp.where(lo == 0, 0, jnp.where(lo == 1, 3, 5))
    return grp * N_PAIRS + base + (hi - lo - 1)


def _post_mix_kernel(is_mlstm, *refs):
    if is_mlstm:
        hs_ref, og_ref, mh_ref = refs[:3]
        refs = refs[3:]
        og = og_ref[...].astype(F32)
        mix = hs_ref[...].astype(F32) * mh_ref[...] * _sigmoid(og)
    else:
        mix = refs[0][...]
        refs = refs[1:]
    (w_ref, x_ref, gm_ref, nw_ref, sh_ref, sc_ref, wrt_ref, rb_ref, us_ref,
     xnew_ref, h2_ref, cls_ref, rank_ref, cnt_ref) = refs
    first = (pl.program_id(0) == 0) & (pl.program_id(1) == 0)

    @pl.when(first)
    def _():
        cnt_ref[...] = jnp.zeros_like(cnt_ref)

    y = _dot(mix.astype(BF16), w_ref[...])
    xnew = x_ref[...] + gm_ref[...] * y
    xnew_ref[...] = xnew
    h2 = _norm_mod(xnew, nw_ref[...], sh_ref[...], sc_ref[...])
    h2_ref[...] = h2.reshape(h2_ref.shape)
    h_hi = h2.astype(BF16)
    h_lo = (h2 - h_hi.astype(F32)).astype(BF16)
    ne = wrt_ref.shape[0] // 2
    part = _dot_nt(wrt_ref[...], h_hi)
    logits = part[:ne] + part[ne:] + _dot_nt(wrt_ref[:ne, :], h_lo)
    cls = _route(logits, rb_ref[...])
    cls_ref[...] = cls
    tm = cls.shape[1]
    onehot = (lax.broadcasted_iota(jnp.int32, (CLS_PAD, tm), 0) == cls).astype(F32)
    prefix = _dot(onehot.astype(BF16), us_ref[...])
    carry = cnt_ref[:, 0:1]
    rank = jnp.sum(onehot * (prefix + carry), axis=0, keepdims=True)
    rank_ref[...] = rank.astype(jnp.int32)
    cnt_ref[...] += jnp.sum(onehot, axis=1, keepdims=True)


def post_mix(mix_args, is_mlstm, w_o, x, gm, nw, shift, scale, w_router, router_bias, bsz, seq):
    t, d = x.shape
    tm = min(TM, seq)
    ns = seq // tm
    ne = N_EXPERTS
    idx = jnp.arange(tm)
    upper_strict = (idx[:, None] < idx[None, :]).astype(BF16)
    row = pl.BlockSpec((tm, d), lambda b, s: (b * ns + s, 0))
    vec = pl.BlockSpec((None, 1, d), lambda b, s: (b, 0, 0))
    lane_row = pl.BlockSpec((None, 1, tm), lambda b, s: (b, 0, s))
    if is_mlstm:
        mix_specs = [row, row, _full((1, d))]
    else:
        mix_specs = [row]
    in_specs = mix_specs + [_full((d, d)), row, vec, _full((1, d)), vec, vec, _full((2 * ne, d)), _full((ne, 1)),
                            _full((tm, tm))]
    wr = w_router.T
    wr_hi = wr.astype(BF16)
    wr_split = jnp.concatenate([wr_hi, (wr - wr_hi.astype(F32)).astype(BF16)], axis=0)
    row3 = pl.BlockSpec((tm, d // LANES, LANES), lambda b, s: (b * ns + s, 0, 0))
    out_specs = [row, row3, lane_row, lane_row, _full((CLS_PAD, 128))]
    out_shape = [jax.ShapeDtypeStruct((t, d), F32), jax.ShapeDtypeStruct((t, d // LANES, LANES), F32),
                 jax.ShapeDtypeStruct((bsz, 1, seq), jnp.int32), jax.ShapeDtypeStruct((bsz, 1, seq), jnp.int32),
                 jax.ShapeDtypeStruct((CLS_PAD, 128), F32)]
    return pl.pallas_call(
        functools.partial(_post_mix_kernel, is_mlstm),
        grid=(bsz, ns), in_specs=in_specs, out_specs=out_specs, out_shape=out_shape,
        compiler_params=_cp(("arbitrary", "arbitrary")), name="post_mix",
    )(*mix_args, w_o.astype(BF16), x, gm, nw, shift, scale, wr_split, router_bias.reshape(ne, 1), upper_strict)


def _row_scatter_kernel(rb, dest_ref, src_ref, dst_in_ref, dst_ref, sem):
    del dst_in_ref
    base = pl.program_id(0) * rb

    def start(g, carry):
        for u in range(DMA_GROUP):
            rr = g * DMA_GROUP + u
            pltpu.make_async_copy(src_ref.at[rr], dst_ref.at[dest_ref[base + rr]], sem).start(priority=u % DMA_QUEUES)
        return carry

    def wait(rr, carry):
        pltpu.make_async_copy(src_ref.at[0], dst_ref.at[0], sem).wait()
        return carry

    lax.fori_loop(0, rb // DMA_GROUP, start, 0)
    lax.fori_loop(0, rb, wait, 0, unroll=8)


def row_scatter(dest, src, n_rows):
    t = src.shape[0]
    rb = min(RB, t)
    dst0 = jnp.zeros((n_rows,) + src.shape[1:], src.dtype)
    return pl.pallas_call(
        functools.partial(_row_scatter_kernel, rb),
        grid_spec=pltpu.PrefetchScalarGridSpec(
            num_scalar_prefetch=1, grid=(t // rb,),
            in_specs=[pl.BlockSpec((rb,) + src.shape[1:], lambda i, dest: (i, 0, 0)),
                      pl.BlockSpec(memory_space=pl.ANY)],
            out_specs=pl.BlockSpec(memory_space=pl.ANY),
            scratch_shapes=[pltpu.SemaphoreType.DMA]),
        out_shape=jax.ShapeDtypeStruct(dst0.shape, src.dtype),
        input_output_aliases={2: 0},
        compiler_params=_cp(("arbitrary",)), name="row_scatter",
    )(dest, src, dst0)


def _row_gather_kernel(rb, dest_ref, src_ref, out_ref, sem):
    base = pl.program_id(0) * rb

    def start(g, carry):
        for u in range(DMA_GROUP):
            rr = g * DMA_GROUP + u
            pltpu.make_async_copy(src_ref.at[dest_ref[base + rr]], out_ref.at[rr], sem).start(priority=u % DMA_QUEUES)
        return carry

    def wait(rr, carry):
        pltpu.make_async_copy(src_ref.at[0], out_ref.at[0], sem).wait()
        return carry

    lax.fori_loop(0, rb // DMA_GROUP, start, 0)
    lax.fori_loop(0, rb, wait, 0, unroll=8)


def row_gather(dest, src, t):
    rb = min(RB, t)
    return pl.pallas_call(
        functools.partial(_row_gather_kernel, rb),
        grid_spec=pltpu.PrefetchScalarGridSpec(
            num_scalar_prefetch=1, grid=(t // rb,),
            in_specs=[pl.BlockSpec(memory_space=pl.ANY)],
            out_specs=pl.BlockSpec((rb,) + src.shape[1:], lambda i, dest: (i, 0, 0)),
            scratch_shapes=[pltpu.SemaphoreType.DMA]),
        out_shape=jax.ShapeDtypeStruct((t,) + src.shape[1:], src.dtype),
        compiler_params=_cp(("arbitrary",)), name="row_gather",
    )(dest, src)


def _experts_kernel(elo_ref, ehi_ref, nused_ref, x_ref, wrt_ref,
                    wg_lo, wu_lo, wd_lo, wg_hi, wu_hi, wd_hi, y_ref):
    i = pl.program_id(0)

    @pl.when(i < nused_ref[0])
    def _():
        x = x_ref[...].reshape(x_ref.shape[0], -1)
        xb = x.astype(BF16)
        aff_lo = _sigmoid(jnp.sum(x * wrt_ref[pl.ds(elo_ref[i], 1), :], axis=1, keepdims=True))
        aff_hi = _sigmoid(jnp.sum(x * wrt_ref[pl.ds(ehi_ref[i], 1), :], axis=1, keepdims=True))
        tot = aff_lo + aff_hi
        acc = None
        for gate, wg, wu, wd in ((aff_lo / tot, wg_lo, wu_lo, wd_lo), (aff_hi / tot, wg_hi, wu_hi, wd_hi)):
            hg = _dot(xb, wg[...])
            hu = _dot(xb, wu[...])
            act = hg * _sigmoid(hg) * hu * gate
            part = _dot(act.astype(BF16), wd[...])
            acc = part if acc is None else acc + part
        y_ref[...] = acc.reshape(y_ref.shape)

    @pl.when(i >= nused_ref[0])
    def _():
        y_ref[...] = jnp.zeros_like(y_ref)


def experts(xs, w_router_t, blk_lo, blk_hi, nused, w_gate, w_up, w_down):
    p = xs.shape[0]
    d = xs.shape[1] * xs.shape[2]
    nblk = p // MB
    de = w_gate.shape[2]
    xrow = pl.BlockSpec((MB,) + xs.shape[1:], lambda i, lo, hi, nu: (i, 0, 0))
    wrt = pl.BlockSpec(w_router_t.shape, lambda i, lo, hi, nu: (0, 0))
    w_in_lo = pl.BlockSpec((None, d, de), lambda i, lo, hi, nu: (lo[i], 0, 0))
    w_in_hi = pl.BlockSpec((None, d, de), lambda i, lo, hi, nu: (hi[i], 0, 0))
    w_out_lo = pl.BlockSpec((None, de, d), lambda i, lo, hi, nu: (lo[i], 0, 0))
    w_out_hi = pl.BlockSpec((None, de, d), lambda i, lo, hi, nu: (hi[i], 0, 0))
    return pl.pallas_call(
        _experts_kernel,
        grid_spec=pltpu.PrefetchScalarGridSpec(
            num_scalar_prefetch=3, grid=(nblk,),
            in_specs=[xrow, wrt, w_in_lo, w_in_lo, w_out_lo, w_in_hi, w_in_hi, w_out_hi],
            out_specs=xrow),
        out_shape=jax.ShapeDtypeStruct(xs.shape, F32),
        compiler_params=_cp(("arbitrary",)), name="experts",
    )(blk_lo, blk_hi, nused, xs, w_router_t, w_gate, w_up, w_down, w_gate, w_up, w_down)


_PAIR_LO = (0, 0, 0, 1, 1, 2)
_PAIR_HI = (1, 2, 3, 2, 3, 3)


def moe(h2, cls, rank, counts, w_router_t, w_gate, w_up, w_down):
    t = h2.shape[0]
    p = t + N_CLASSES * MB
    nblk = p // MB
    cls = cls.reshape(t)
    counts = counts[:N_CLASSES, 0].astype(jnp.int32)
    padded = (counts + MB - 1) // MB * MB
    pad_end = jnp.cumsum(padded)
    pad_start = pad_end - padded
    dest = (pad_start[cls] + rank.reshape(t)).astype(jnp.int32)
    blk_row = jnp.arange(nblk, dtype=jnp.int32) * MB
    blk_cls = jnp.minimum(jnp.sum((pad_end[None, :] <= blk_row[:, None]).astype(jnp.int32), axis=1), N_CLASSES - 1)
    grp = blk_cls // N_PAIRS
    pr = blk_cls % N_PAIRS
    blk_lo = grp * EPG + jnp.asarray(_PAIR_LO, jnp.int32)[pr]
    blk_hi = grp * EPG + jnp.asarray(_PAIR_HI, jnp.int32)[pr]
    nused = (pad_end[-1:] // MB).astype(jnp.int32)
    xs = row_scatter(dest, h2, p)
    ys = experts(xs, w_router_t, blk_lo, blk_hi, nused, w_gate, w_up, w_down)
    return row_gather(dest, ys, t)


def _final_kernel(*refs):
    _, h, rest = _prologue(True, refs)
    rest[0][...] = h


def final_norm(x, moe_out, gf, nw, shift, scale, bsz, seq):
    t, d = x.shape
    tm = min(TM, seq)
    ns = seq // tm
    row = pl.BlockSpec((tm, d), lambda b, s: (b * ns + s, 0))
    return pl.pallas_call(
        _final_kernel, grid=(bsz, ns), in_specs=_prologue_specs(True, ns, d, tm), out_specs=row,
        out_shape=jax.ShapeDtypeStruct((t, d), F32),
        compiler_params=_cp(("arbitrary", "arbitrary")), name="final_norm",
    )(x, moe_out, gf, nw, shift, scale)


def kernel(x, c, a_w_in, a_b_gates, a_mh_norm, a_w_out, kv_norm, w_ada_kv, b_ada_kv, w_kv, b_fgate, b_w_q, b_w_o,
           norm_mix, norm_ffn, w_ada, b_ada, w_router, router_bias, w_gate, w_up, w_down, norm_final, w_ada_final,
           b_ada_final):
    bsz, seq, d = x.shape
    depth = w_ada.shape[0]
    n_a = a_w_in.shape[0]
    t = bsz * seq
    xf = x.reshape(t, d)

    mods = ada(c, w_ada, b_ada)
    kv_mod = ada(c, w_ada_kv[None], b_ada_kv[None])[0]
    fin = ada(c, w_ada_final[None], b_ada_final[None])[0]

    def vecs(m, n):
        return [m[:, None, i * d:(i + 1) * d] for i in range(n)]

    wg_b = w_gate.astype(BF16)
    wu_b = w_up.astype(BF16)
    wd_b = w_down.astype(BF16)

    moe_out = None
    gf_prev = None
    kv = None
    for layer in range(depth):
        sh_m, sc_m, g_m, sh_f, sc_f, g_f = vecs(mods[layer], 6)
        nm = norm_mix[layer].reshape(1, d)
        if layer < n_a:
            xf, (q, k, v, o, grow, gcol) = mlstm_in(xf, moe_out, gf_prev, nm, sh_m, sc_m, a_w_in[layer],
                                                     a_b_gates[layer], bsz, seq)
            hs = mlstm_cell(q, k, v, grow, gcol, bsz, seq)
            mix_args = (hs, o, a_mh_norm[layer].reshape(1, d))
            w_o = a_w_out[layer]
        else:
            j = layer - n_a
            xf, q = fox_q(xf, moe_out, gf_prev, nm, sh_m, sc_m, b_w_q[j], bsz, seq)
            if layer == n_a:
                kv_sh, kv_sc = vecs(kv_mod, 2)
                kv = shared_kv(xf, kv_norm.reshape(1, d), kv_sh, kv_sc, w_kv, b_fgate, bsz, seq)
            ke, ko, vt, eq = kv
            att = fox_attn(q, eq, ke, ko, vt, bsz, seq)
            mix_args = (att,)
            w_o = b_w_o[j]
        xf, h2, cls, rank, counts = post_mix(
            mix_args, layer < n_a, w_o, xf, g_m, norm_ffn[layer].reshape(1, d), sh_f, sc_f, w_router, router_bias,
            bsz, seq)
        moe_out = moe(h2, cls, rank, counts, w_router.T, wg_b[layer], wu_b[layer], wd_b[layer])
        gf_prev = g_f
    fin_sh, fin_sc = vecs(fin, 2)
    out = final_norm(xf, moe_out, gf_prev, norm_final.reshape(1, d), fin_sh, fin_sc, bsz, seq)
    return out.reshape(bsz, seq, d)
```

```python
import functools

import jax
import jax.numpy as jnp
from jax import lax
from jax.experimental import pallas as pl
from jax.experimental.pallas import tpu as pltpu

F32 = jnp.float32
BF16 = jnp.bfloat16
EPS = 1e-6
GATE_SOFTCAP = 15.0

A_HEADS = 8
B_HEADS = 16
N_EXPERTS = 16
N_GROUPS = 4
EPG = N_EXPERTS // N_GROUPS
N_PAIRS = 6
N_CLASSES = N_GROUPS * N_PAIRS
CLS_PAD = 32

VMEM_LIMIT = 56 * 1024 * 1024

TM = 512
A_L = 256
TQ = 1024
ATT_CHUNK = 256
MB = 256
RB = 2048
DMA_GROUP = 8
DMA_QUEUES = 2
PIECE = 256
LANES = 128
LOG2E = 1.4426950408889634

HIGHEST = lax.Precision.HIGHEST


def _cp(sem):
    return pltpu.CompilerParams(dimension_semantics=sem, vmem_limit_bytes=VMEM_LIMIT)


def _dot(a, b, precision=None):
    return jnp.dot(a, b, preferred_element_type=F32, precision=precision)


def _dot_nt(a, b, precision=None):
    return lax.dot_general(a, b, (((1,), (1,)), ((), ())), preferred_element_type=F32, precision=precision)


def _dot_tn(a, b, precision=None):
    return lax.dot_general(a, b, (((0,), (0,)), ((), ())), preferred_element_type=F32, precision=precision)


def _norm_mod(xin, nw, shift, scale):
    ms = jnp.mean(xin * xin, axis=-1, keepdims=True)
    y = xin * lax.rsqrt(ms + EPS)
    return (y * nw) * (1.0 + scale) + shift


def _log_sigmoid(x):
    return jnp.minimum(x, 0.0) - jnp.log1p(jnp.exp(-jnp.abs(x)))


def _sigmoid(x):
    return 1.0 / (1.0 + jnp.exp(-x))


def _ada_kernel(c_ref, w_ref, b_ref, o_ref):
    c = c_ref[...]
    ca = c * _sigmoid(c)
    o_ref[...] = _dot(ca, w_ref[...], HIGHEST) + b_ref[...]


def ada(c, w, b):
    nl, d, n = w.shape
    bsz = c.shape[0]
    tn = 512
    return pl.pallas_call(
        _ada_kernel,
        grid=(nl, n // tn),
        in_specs=[
            pl.BlockSpec((bsz, d), lambda l, j: (0, 0)),
            pl.BlockSpec((None, d, tn), lambda l, j: (l, 0, j)),
            pl.BlockSpec((None, 1, tn), lambda l, j: (l, 0, j)),
        ],
        out_specs=pl.BlockSpec((None, bsz, tn), lambda l, j: (l, 0, j)),
        out_shape=jax.ShapeDtypeStruct((nl, bsz, n), F32),
        compiler_params=_cp(("arbitrary", "arbitrary")),
        name="ada",
    )(c, w, b.reshape(nl, 1, n))


def _prologue(has_moe, refs):
    if not has_moe:
        x_ref, nw_ref, sh_ref, sc_ref = refs[:4]
        xin = x_ref[...]
        h = _norm_mod(xin, nw_ref[...], sh_ref[...], sc_ref[...])
        return xin, h, refs[4:], (lambda g, n: None), (lambda: None)
    dest_ref, x_ref, ys_ref, gf_ref, nw_ref, sh_ref, sc_ref = refs[:7]
    gbuf, gsem = refs[-2:]
    tm = x_ref.shape[0]
    lin = pl.program_id(0) * pl.num_programs(1) + pl.program_id(1)
    last = pl.num_programs(0) * pl.num_programs(1) - 1
    slot = lin % 2

    def start_rows(tile, slot_, lo, hi):
        base = tile * tm
        for r in range(lo, hi):
            pltpu.make_async_copy(ys_ref.at[dest_ref[base + r]], gbuf.at[slot_, r], gsem.at[slot_]).start(
                priority=r % DMA_QUEUES)

    def wait_rows(slot_):
        def body(r, carry):
            pltpu.make_async_copy(ys_ref.at[0], gbuf.at[slot_, 0], gsem.at[slot_]).wait()
            return carry
        lax.fori_loop(0, tm, body, 0, unroll=8)

    @pl.when(lin == 0)
    def _():
        start_rows(0, 0, 0, tm)

    wait_rows(slot)
    xin = x_ref[...] + gf_ref[...] * gbuf[slot].reshape(x_ref.shape)
    h = _norm_mod(xin, nw_ref[...], sh_ref[...], sc_ref[...])
    nxt = jnp.minimum(lin + 1, last)

    def issue(g, n):
        start_rows(nxt, 1 - slot, g * tm // n, (g + 1) * tm // n)

    def finish():
        @pl.when(lin == last)
        def _():
            wait_rows(1 - slot)

    return xin, h, refs[7:-2], issue, finish


def _prologue_specs(has_moe, ns, d, tm):
    row = pl.BlockSpec((tm, d), lambda b, s, *_: (b * ns + s, 0))
    vec = pl.BlockSpec((None, 1, d), lambda b, s, *_: (b, 0, 0))
    one = pl.BlockSpec((1, d), lambda b, s, *_: (0, 0))
    if has_moe:
        return [row, pl.BlockSpec(memory_space=pl.ANY), vec, one, vec, vec]
    return [row, one, vec, vec]


def _prologue_call(kernel_fn, has_moe, dest, args, in_specs, out_specs, out_shape, bsz, ns, tm, d, name):
    if not has_moe:
        return pl.pallas_call(
            kernel_fn, grid=(bsz, ns), in_specs=in_specs, out_specs=out_specs, out_shape=out_shape,
            compiler_params=_cp(("arbitrary", "arbitrary")), name=name)(*args)
    return pl.pallas_call(
        kernel_fn,
        grid_spec=pltpu.PrefetchScalarGridSpec(
            num_scalar_prefetch=1, grid=(bsz, ns), in_specs=in_specs, out_specs=out_specs,
            scratch_shapes=[pltpu.VMEM((2, tm, d // LANES, LANES), F32), pltpu.SemaphoreType.DMA((2,))]),
        out_shape=out_shape, compiler_params=_cp(("arbitrary", "arbitrary")), name=name)(dest, *args)


def _full(shape):
    nd = len(shape)
    return pl.BlockSpec(shape, lambda b, s, *_: (0,) * nd)


def _mlstm_in_kernel(has_moe, dqk, *refs):
    xin, h, rest, issue, finish = _prologue(has_moe, refs)
    wqt_ref, wk_ref, wvt_ref, wo_ref, wgt_ref, wg_ref, bcol_ref, brow_ref = rest[:8]
    outs = rest[8:]
    if has_moe:
        xnew_ref = outs[0]
        outs = outs[1:]
        xnew_ref[...] = xin
    qt_ref, k_ref, vt_ref, o_ref, grow_ref, gcol_ref = outs
    hb = h.astype(BF16)
    scale = float(dqk // A_HEADS) ** -0.5
    pieces = []
    for kind, w_ref, out_ref, sc in (("t", wqt_ref, qt_ref, scale), ("n", wk_ref, k_ref, None),
                                     ("t", wvt_ref, vt_ref, None), ("n", wo_ref, o_ref, None)):
        nf = w_ref.shape[0] if kind == "t" else w_ref.shape[1]
        for f0 in range(0, nf, PIECE):
            pieces.append((kind, w_ref, out_ref, slice(f0, f0 + PIECE), sc))
    for g, (kind, w_ref, out_ref, fs, sc) in enumerate(pieces):
        issue(g, len(pieces))
        if kind == "t":
            res = _dot_nt(w_ref[fs, :], hb)
            out_ref[fs, :] = (res if sc is None else res * sc).astype(BF16)
        else:
            out_ref[:, fs] = _dot(hb, w_ref[:, fs]).astype(BF16)
    gr = _dot_nt(wgt_ref[...], hb) + bcol_ref[...]
    gr = GATE_SOFTCAP * jnp.tanh(gr / GATE_SOFTCAP)
    ridx = lax.broadcasted_iota(jnp.int32, gr.shape, 0)
    grow_ref[...] = jnp.where(ridx < A_HEADS, gr, _log_sigmoid(gr))
    gc = _dot(hb, wg_ref[...]) + brow_ref[...]
    gc = GATE_SOFTCAP * jnp.tanh(gc / GATE_SOFTCAP)
    cidx = lax.broadcasted_iota(jnp.int32, gc.shape, 1)
    gcol_ref[...] = jnp.where(cidx < A_HEADS, gc, _log_sigmoid(gc))
    finish()


def mlstm_in(x, moe, gf, nw, shift, scale, w_in, b_gates, bsz, seq):
    t, d = x.shape
    dqk, dv = d // 2, d
    has_moe = moe is not None
    tm = min(TM, seq)
    ns = seq // tm
    w_b = w_in.astype(BF16)
    w_qt = w_b[:, 0:dqk].T
    w_k = w_b[:, dqk:2 * dqk]
    w_vt = w_b[:, 2 * dqk:2 * dqk + dv].T
    w_o = w_b[:, 2 * dqk + dv:2 * dqk + 2 * dv]
    w_g = w_b[:, 2 * dqk + 2 * dv:]
    ng = 2 * A_HEADS
    ys, dest = moe if has_moe else (None, None)
    args = [x] + ([ys, gf] if has_moe else []) + [nw, shift, scale, w_qt, w_k, w_vt, w_o, w_g.T, w_g,
                                                   b_gates.reshape(ng, 1), b_gates.reshape(1, ng)]
    in_specs = _prologue_specs(has_moe, ns, d, tm) + [
        _full((dqk, d)), _full((d, dqk)), _full((dv, d)), _full((d, dv)), _full((ng, d)), _full((d, ng)),
        _full((ng, 1)), _full((1, ng))]
    row = lambda n: pl.BlockSpec((tm, n), lambda b, s, *_: (b * ns + s, 0))
    col = lambda n: pl.BlockSpec((n, tm), lambda b, s, *_: (0, b * ns + s))
    out_specs = [col(dqk), row(dqk), col(dv), row(dv),
                 pl.BlockSpec((None, ng, tm), lambda b, s, *_: (b, 0, s)),
                 pl.BlockSpec((None, tm, ng), lambda b, s, *_: (b, s, 0))]
    out_shape = [jax.ShapeDtypeStruct((dqk, t), BF16), jax.ShapeDtypeStruct((t, dqk), BF16),
                 jax.ShapeDtypeStruct((dv, t), BF16), jax.ShapeDtypeStruct((t, dv), BF16),
                 jax.ShapeDtypeStruct((bsz, ng, seq), F32), jax.ShapeDtypeStruct((bsz, seq, ng), F32)]
    if has_moe:
        out_specs = [row(d)] + out_specs
        out_shape = [jax.ShapeDtypeStruct((t, d), F32)] + out_shape
    res = _prologue_call(functools.partial(_mlstm_in_kernel, has_moe, dqk), has_moe, dest, args, in_specs, out_specs,
                         out_shape, bsz, ns, tm, d, "mlstm_in")
    if has_moe:
        return res[0], res[1:]
    return x, res


N_ROWS = 16


def _mlstm_cell_kernel(qt_ref, k_ref, vt_ref, grow_ref, gcol_ref, hs_ref, ct_ref, m_ref):
    c = pl.program_id(1)
    L = k_ref.shape[0]
    dk = k_ref.shape[1] // A_HEADS
    dvh = vt_ref.shape[0] // A_HEADS

    @pl.when(c == 0)
    def _():
        ct_ref[...] = jnp.zeros_like(ct_ref)
        m_ref[...] = jnp.zeros_like(m_ref)

    r = lax.broadcasted_iota(jnp.int32, (L, L), 0)
    cc = lax.broadcasted_iota(jnp.int32, (L, L), 1)
    visible = r <= cc
    upper = visible.astype(F32)
    lower = (cc <= r).astype(F32)
    grow = grow_ref[...]
    gcol = gcol_ref[...]
    b_row_all = _dot(grow[A_HEADS:, :], upper, HIGHEST)
    b_col_all = _dot(lower, gcol[:, A_HEADS:], HIGHEST)
    x_col_all = gcol[:, :A_HEADS] - b_col_all
    ones_rows = jnp.ones((N_ROWS, L), BF16)

    for h in range(A_HEADS):
        qt = qt_ref[h * dk:(h + 1) * dk, :]
        kh = k_ref[:, h * dk:(h + 1) * dk]
        v_aug = jnp.concatenate([vt_ref[h * dvh:(h + 1) * dvh, :], ones_rows], axis=0)
        br = b_row_all[h:h + 1, :]
        igr = grow[h:h + 1, :]
        xc = x_col_all[:, h:h + 1]
        m_prev = m_ref[h:h + 1, 0:1]
        ct = ct_ref[h]

        d_intra = jnp.where(visible, br + xc, -jnp.inf)
        d_inter = br + m_prev
        m_t = jnp.maximum(d_inter, jnp.max(d_intra, axis=0, keepdims=True))
        w = jnp.exp(d_intra - m_t) * _dot(kh, qt)
        a = jnp.exp(d_inter - m_t)
        tot = _dot(v_aug, w.astype(BF16)) + a * _dot(ct.astype(BF16), qt)
        num = tot[:dvh, :]
        den = tot[dvh:dvh + 1, :]
        hh = num / jnp.maximum(jnp.abs(den), jnp.exp(-m_t))
        hh = hh * lax.rsqrt(jnp.mean(hh * hh, axis=0, keepdims=True) + EPS)
        hs_ref[:, h * dvh:(h + 1) * dvh] = hh.T.astype(hs_ref.dtype)

        b_last = br[:, L - 1:L]
        g = b_last - br + igr
        m_new = jnp.maximum(b_last + m_prev, jnp.max(g, axis=1, keepdims=True))
        wk = jnp.exp(g - m_new)
        decay = jnp.exp(b_last + m_prev - m_new)
        ct_ref[h] = decay * ct + _dot((v_aug.astype(F32) * wk).astype(BF16), kh)
        m_ref[h:h + 1, :] = jnp.broadcast_to(m_new, (1, m_ref.shape[1]))


def mlstm_cell(qt, k, vt, grow, gcol, bsz, seq):
    t, dqk = k.shape
    dv = vt.shape[0]
    L = min(A_L, seq)
    nc = seq // L
    ng = grow.shape[1]
    dk = dqk // A_HEADS
    dvh = dv // A_HEADS
    return pl.pallas_call(
        _mlstm_cell_kernel,
        grid=(bsz, nc),
        in_specs=[pl.BlockSpec((dqk, L), lambda b, c: (0, b * nc + c)),
                  pl.BlockSpec((L, dqk), lambda b, c: (b * nc + c, 0)),
                  pl.BlockSpec((dv, L), lambda b, c: (0, b * nc + c)),
                  pl.BlockSpec((None, ng, L), lambda b, c: (b, 0, c)),
                  pl.BlockSpec((None, L, ng), lambda b, c: (b, c, 0))],
        out_specs=pl.BlockSpec((L, dv), lambda b, c: (b * nc + c, 0)),
        out_shape=jax.ShapeDtypeStruct((t, dv), BF16),
        scratch_shapes=[pltpu.VMEM((A_HEADS, dvh + N_ROWS, dk), F32), pltpu.VMEM((A_HEADS, 128), F32)],
        compiler_params=_cp(("arbitrary", "arbitrary")), name="mlstm_cell",
    )(qt, k, vt, grow, gcol)


def _fox_q_kernel(has_moe, qscale, *refs):
    xin, h, rest, issue, finish = _prologue(has_moe, refs)
    w_ref = rest[0]
    outs = rest[1:]
    if has_moe:
        outs[0][...] = xin
        outs = outs[1:]
    hb = h.astype(BF16)
    d = w_ref.shape[1]
    n = d // PIECE
    for g in range(n):
        issue(g, n)
        cols = slice(g * PIECE, (g + 1) * PIECE)
        outs[0][:, cols] = (_dot(hb, w_ref[:, cols]) * qscale).astype(BF16)
    finish()


def fox_q(x, moe, gf, nw, shift, scale, w_q, bsz, seq):
    t, d = x.shape
    has_moe = moe is not None
    tm = min(TM, seq)
    ns = seq // tm
    ys, dest = moe if has_moe else (None, None)
    args = [x] + ([ys, gf] if has_moe else []) + [nw, shift, scale, w_q.astype(BF16)]
    in_specs = _prologue_specs(has_moe, ns, d, tm) + [_full((d, d))]
    row = pl.BlockSpec((tm, d), lambda b, s, *_: (b * ns + s, 0))
    out_specs = [row]
    out_shape = [jax.ShapeDtypeStruct((t, d), BF16)]
    if has_moe:
        out_specs = [row] + out_specs
        out_shape = [jax.ShapeDtypeStruct((t, d), F32)] + out_shape
    qscale = float(d // B_HEADS) ** -0.5 * LOG2E
    res = _prologue_call(functools.partial(_fox_q_kernel, has_moe, qscale), has_moe, dest, args, in_specs, out_specs,
                         out_shape, bsz, ns, tm, d, "fox_q")
    if has_moe:
        return res[0], res[1]
    return x, res[0]


N_BIAS_PIECES = 3


def _bias_select_matrices(d, n_heads):
    import numpy as np
    dh = d // n_heads
    pq = np.zeros((N_BIAS_PIECES, n_heads, d), np.float32)
    pk = np.zeros((N_BIAS_PIECES, n_heads, d), np.float32)
    oq = np.zeros((1, d), np.float32)
    ok = np.zeros((1, d), np.float32)
    for h in range(n_heads):
        base = (h // 2) * 2 * dh + (dh if h % 2 == 0 else 0)
        for p in range(N_BIAS_PIECES):
            pq[p, h, base + p] = 1.0
            ok[0, base + p] = 1.0
            oq[0, base + N_BIAS_PIECES + p] = 1.0
            pk[p, h, base + N_BIAS_PIECES + p] = -1.0
    return pq, pk, oq, ok


def _split3(f):
    a = f.astype(BF16)
    r1 = f - a.astype(F32)
    b = r1.astype(BF16)
    c = (r1 - b.astype(F32)).astype(BF16)
    return a, b, c


def _shared_kv_kernel(x_ref, nw_ref, sh_ref, sc_ref, w_ref, wvt_ref, wf_ref, brow_ref, lo_ref, pq_ref, pk_ref, oq_ref,
                      ok_ref, ke_ref, ko_ref, vt_ref, eq_ref, cc_ref):
    s = pl.program_id(1)
    d = x_ref.shape[1]
    dh = d // B_HEADS

    @pl.when(s == 0)
    def _():
        cc_ref[...] = jnp.zeros_like(cc_ref)

    h = _norm_mod(x_ref[...], nw_ref[...], sh_ref[...], sc_ref[...])
    hb = h.astype(BF16)
    lc = _log_sigmoid(_dot(hb, wf_ref[...]) + brow_ref[...])
    fcum = _dot(lo_ref[...], lc, HIGHEST) + cc_ref[...]
    cc_ref[...] += jnp.sum(lc, axis=0, keepdims=True)
    pieces = _split3(fcum * LOG2E)
    eq = oq_ref[...]
    ek = ok_ref[...]
    for p in range(N_BIAS_PIECES):
        eq = eq + _dot(pieces[p], pq_ref[p])
        ek = ek + _dot(pieces[p], pk_ref[p])
    eq_ref[...] = eq.astype(BF16)
    even = (lax.broadcasted_iota(jnp.int32, (1, d), 1) % (2 * dh)) < dh
    k = _dot(hb, w_ref[...])
    ke_ref[...] = jnp.where(even, k, ek).astype(BF16)
    ko_ref[...] = jnp.where(even, ek, k).astype(BF16)
    vt_ref[...] = _dot_nt(wvt_ref[...], hb).astype(BF16)


def shared_kv(x, nw, shift, scale, w_kv, b_fgate, bsz, seq):
    t, d = x.shape
    tm = min(TM, seq)
    ns = seq // tm
    nh = B_HEADS
    w_k = w_kv[:, :d].astype(BF16)
    w_vt = w_kv[:, d:2 * d].astype(BF16).T
    w_f = w_kv[:, 2 * d:].astype(BF16)
    idx = jnp.arange(tm)
    lower = (idx[None, :] <= idx[:, None]).astype(F32)
    pq, pk, oq, ok = _bias_select_matrices(d, nh)
    row = pl.BlockSpec((tm, d), lambda b, s: (b * ns + s, 0))
    vec = pl.BlockSpec((None, 1, d), lambda b, s: (b, 0, 0))
    sel = _full((N_BIAS_PIECES, nh, d))
    return pl.pallas_call(
        _shared_kv_kernel,
        grid=(bsz, ns),
        in_specs=[row, _full((1, d)), vec, vec, _full((d, d)), _full((d, d)), _full((d, nh)), _full((1, nh)),
                  _full((tm, tm)), sel, sel, _full((1, d)), _full((1, d))],
        out_specs=[row, row, pl.BlockSpec((d, tm), lambda b, s: (0, b * ns + s)), row],
        out_shape=[jax.ShapeDtypeStruct((t, d), BF16), jax.ShapeDtypeStruct((t, d), BF16),
                   jax.ShapeDtypeStruct((d, t), BF16), jax.ShapeDtypeStruct((t, d), BF16)],
        scratch_shapes=[pltpu.VMEM((1, nh), F32)],
        compiler_params=_cp(("arbitrary", "arbitrary")), name="shared_kv",
    )(x, nw, shift, scale, w_k, w_vt, w_f, b_fgate.reshape(1, nh), lower,
      jnp.asarray(pq, BF16), jnp.asarray(pk, BF16), jnp.asarray(oq), jnp.asarray(ok))


def _fox_attn_kernel(q_ref, eq_ref, ke_ref, ko_ref, vt_ref, o_ref, st_ref):
    i = pl.program_id(2)
    tq = q_ref.shape[0]
    tk = tq
    ck = min(ATT_CHUNK, tk)
    nchunk = tk // ck
    dh2 = q_ref.shape[1]
    dh = dh2 // 2
    lo_mask = lax.broadcasted_iota(jnp.int32, (1, dh2), 1) < dh
    q = q_ref[...]
    eq = eq_ref[...]
    qs = (jnp.where(lo_mask, q, eq), jnp.where(lo_mask, eq, q))
    k_refs = (ke_ref, ko_ref)
    ones_rows = jnp.ones((N_ROWS, ck), BF16)

    def tail(x, q0, new):
        return new if q0 == 0 else jnp.concatenate([x[:, :q0], new], axis=1)

    def qk_chunk(e, kt, c, mx, diag):
        off = pl.multiple_of(kt * tk + c * ck, ck)
        q0 = c * ck if diag else 0
        st = _dot_nt(k_refs[e][pl.ds(off, ck), :], qs[e][q0:, :])
        if diag:
            visible = (lax.broadcasted_iota(jnp.int32, st.shape, 0) <= lax.broadcasted_iota(jnp.int32, st.shape, 1))
            st = jnp.where(visible, st, -jnp.inf)
        st_ref[e, c * ck:(c + 1) * ck, q0:] = st
        cm = jnp.max(st, axis=0, keepdims=True)
        return cm if mx is None else tail(mx, q0, jnp.maximum(mx[:, q0:], cm))

    def pv_chunk(e, kt, c, state, diag):
        m, acc = state
        off = pl.multiple_of(kt * tk + c * ck, ck)
        q0 = c * ck if diag else 0
        pt = jnp.exp2(st_ref[e, c * ck:(c + 1) * ck, q0:] - m[:, q0:])
        v_aug = jnp.concatenate([vt_ref[e * dh:(e + 1) * dh, pl.ds(off, ck)], ones_rows], axis=0)
        return m, tail(acc, q0, acc[:, q0:] + _dot(v_aug, pt.astype(BF16)))

    def phase(cur, state, tile_max, nxt):
        m, acc = state
        m_new = jnp.maximum(m, tile_max)
        state = (m_new, jnp.exp2(m - m_new) * acc)
        mx = None
        for c in range(nchunk):
            mx = qk_chunk(nxt[0], nxt[1], c, mx, nxt[2])
            state = pv_chunk(cur[0], cur[1], c, state, cur[2])
        return state, mx

    init = (jnp.full((1, tq), -1e30, F32), jnp.zeros((dh + N_ROWS, tq), F32))
    mx0 = None
    for c in range(nchunk):
        mx0 = qk_chunk(0, i, c, mx0, True)
    s0, mx1 = phase((0, i, True), init, mx0, (1, i, True))
    s1, mx0 = phase((1, i, True), init, mx1, (0, 0, False))

    def trip(j, carry):
        s0, s1, mx0 = carry
        s0, mx1 = phase((0, j, False), s0, mx0, (1, j, False))
        s1, mx0 = phase((1, j, False), s1, mx1, (0, j + 1, False))
        return s0, s1, mx0

    s0, s1, _ = lax.fori_loop(0, i, trip, (s0, s1, mx0))
    ot = jnp.concatenate([acc[:dh] / acc[dh:dh + 1] for _, acc in (s0, s1)], axis=0)
    o_ref[...] = ot.T.astype(o_ref.dtype)


def fox_attn(q, eq, ke, ko, vt, bsz, seq):
    t, d = q.shape
    nh = B_HEADS
    dh2 = 2 * (d // nh)
    tq = min(TQ, seq)
    nq = seq // tq
    qspec = pl.BlockSpec((tq, dh2), lambda b, hp, i: (b * nq + i, hp))
    kspec = pl.BlockSpec((seq, dh2), lambda b, hp, i: (b, hp))
    vspec = pl.BlockSpec((dh2, seq), lambda b, hp, i: (hp, b))
    return pl.pallas_call(
        _fox_attn_kernel,
        grid=(bsz, nh // 2, nq),
        in_specs=[qspec, qspec, kspec, kspec, vspec],
        out_specs=qspec,
        out_shape=jax.ShapeDtypeStruct((t, d), BF16),
        scratch_shapes=[pltpu.VMEM((2, tq, tq), F32)],
        compiler_params=_cp(("arbitrary", "arbitrary", "arbitrary")), name="fox_attn",
    )(q, eq, ke, ko, vt)


def _route(logits, bias_col):
    aff = _sigmoid(logits)
    sel = aff + bias_col
    s = [sel[e:e + 1, :] for e in range(N_EXPERTS)]
    gs = []
    for g in range(N_GROUPS):
        v = s[EPG * g:EPG * (g + 1)]
        best = v[0] + v[1]
        for i in range(EPG):
            for j in range(i + 1, EPG):
                if (i, j) != (0, 1):
                    best = jnp.maximum(best, v[i] + v[j])
        gs.append(best)
    grp = jnp.zeros_like(gs[0], dtype=jnp.int32)
    best = gs[0]
    for g in range(1, N_GROUPS):
        upd = gs[g] > best
        grp = jnp.where(upd, g, grp)
        best = jnp.where(upd, gs[g], best)

    def pick(arrs, j):
        out = arrs[(N_GROUPS - 1) * EPG + j]
        for g in range(N_GROUPS - 2, -1, -1):
            out = jnp.where(grp == g, arrs[g * EPG + j], out)
        return out

    v = [pick(s, j) for j in range(EPG)]
    i1 = jnp.zeros_like(grp)
    b1 = v[0]
    for j in range(1, EPG):
        upd = v[j] > b1
        i1 = jnp.where(upd, j, i1)
        b1 = jnp.where(upd, v[j], b1)
    i2 = jnp.full_like(grp, -1)
    b2 = jnp.full_like(b1, -jnp.inf)
    for j in range(EPG):
        upd = (i1 != j) & ((i2 < 0) | (v[j] > b2))
        i2 = jnp.where(upd, j, i2)
        b2 = jnp.where(upd, v[j], b2)
    lo = jnp.minimum(i1, i2)
    hi = jnp.maximum(i1, i2)
    base = jnp.where(lo == 0, 0, jnp.where(lo == 1, 3, 5))
    return grp * N_PAIRS + base + (hi - lo - 1)


def _post_mix_kernel(is_mlstm, *refs):
    if is_mlstm:
        hs_ref, og_ref, mh_ref = refs[:3]
        refs = refs[3:]
        og = og_ref[...].astype(F32)
        mix = hs_ref[...].astype(F32) * mh_ref[...] * _sigmoid(og)
    else:
        mix = refs[0][...]
        refs = refs[1:]
    (w_ref, x_ref, gm_ref, nw_ref, sh_ref, sc_ref, wrt_ref, rb_ref, us_ref,
     xnew_ref, h2_ref, cls_ref, rank_ref, cnt_ref) = refs
    first = (pl.program_id(0) == 0) & (pl.program_id(1) == 0)

    @pl.when(first)
    def _():
        cnt_ref[...] = jnp.zeros_like(cnt_ref)

    y = _dot(mix.astype(BF16), w_ref[...])
    xnew = x_ref[...] + gm_ref[...] * y
    xnew_ref[...] = xnew
    h2 = _norm_mod(xnew, nw_ref[...], sh_ref[...], sc_ref[...])
    h2_ref[...] = h2.reshape(h2_ref.shape)
    h_hi = h2.astype(BF16)
    h_lo = (h2 - h_hi.astype(F32)).astype(BF16)
    ne = wrt_ref.shape[0] // 2
    part = _dot_nt(wrt_ref[...], h_hi)
    logits = part[:ne] + part[ne:] + _dot_nt(wrt_ref[:ne, :], h_lo)
    cls = _route(logits, rb_ref[...])
    cls_ref[...] = cls
    tm = cls.shape[1]
    onehot = (lax.broadcasted_iota(jnp.int32, (CLS_PAD, tm), 0) == cls).astype(F32)
    prefix = _dot(onehot.astype(BF16), us_ref[...])
    carry = cnt_ref[:, 0:1]
    rank = jnp.sum(onehot * (prefix + carry), axis=0, keepdims=True)
    rank_ref[...] = rank.astype(jnp.int32)
    cnt_ref[...] += jnp.sum(onehot, axis=1, keepdims=True)


def post_mix(mix_args, is_mlstm, w_o, x, gm, nw, shift, scale, w_router, router_bias, bsz, seq):
    t, d = x.shape
    tm = min(TM, seq)
    ns = seq // tm
    ne = N_EXPERTS
    idx = jnp.arange(tm)
    upper_strict = (idx[:, None] < idx[None, :]).astype(BF16)
    row = pl.BlockSpec((tm, d), lambda b, s: (b * ns + s, 0))
    vec = pl.BlockSpec((None, 1, d), lambda b, s: (b, 0, 0))
    lane_row = pl.BlockSpec((None, 1, tm), lambda b, s: (b, 0, s))
    if is_mlstm:
        mix_specs = [row, row, _full((1, d))]
    else:
        mix_specs = [row]
    in_specs = mix_specs + [_full((d, d)), row, vec, _full((1, d)), vec, vec, _full((2 * ne, d)), _full((ne, 1)),
                            _full((tm, tm))]
    wr = w_router.T
    wr_hi = wr.astype(BF16)
    wr_split = jnp.concatenate([wr_hi, (wr - wr_hi.astype(F32)).astype(BF16)], axis=0)
    row3 = pl.BlockSpec((tm, d // LANES, LANES), lambda b, s: (b * ns + s, 0, 0))
    out_specs = [row, row3, lane_row, lane_row, _full((CLS_PAD, 128))]
    out_shape = [jax.ShapeDtypeStruct((t, d), F32), jax.ShapeDtypeStruct((t, d // LANES, LANES), F32),
                 jax.ShapeDtypeStruct((bsz, 1, seq), jnp.int32), jax.ShapeDtypeStruct((bsz, 1, seq), jnp.int32),
                 jax.ShapeDtypeStruct((CLS_PAD, 128), F32)]
    return pl.pallas_call(
        functools.partial(_post_mix_kernel, is_mlstm),
        grid=(bsz, ns), in_specs=in_specs, out_specs=out_specs, out_shape=out_shape,
        compiler_params=_cp(("arbitrary", "arbitrary")), name="post_mix",
    )(*mix_args, w_o.astype(BF16), x, gm, nw, shift, scale, wr_split, router_bias.reshape(ne, 1), upper_strict)


def _row_scatter_kernel(rb, dest_ref, src_ref, dst_in_ref, dst_ref, sem):
    del dst_in_ref
    base = pl.program_id(0) * rb

    def start(g, carry):
        for u in range(DMA_GROUP):
            rr = g * DMA_GROUP + u
            pltpu.make_async_copy(src_ref.at[rr], dst_ref.at[dest_ref[base + rr]], sem).start(priority=u % DMA_QUEUES)
        return carry

    def wait(rr, carry):
        pltpu.make_async_copy(src_ref.at[0], dst_ref.at[0], sem).wait()
        return carry

    lax.fori_loop(0, rb // DMA_GROUP, start, 0)
    lax.fori_loop(0, rb, wait, 0, unroll=8)


def row_scatter(dest, src, n_rows):
    t = src.shape[0]
    rb = min(RB, t)
    dst0 = jnp.zeros((n_rows,) + src.shape[1:], src.dtype)
    return pl.pallas_call(
        functools.partial(_row_scatter_kernel, rb),
        grid_spec=pltpu.PrefetchScalarGridSpec(
            num_scalar_prefetch=1, grid=(t // rb,),
            in_specs=[pl.BlockSpec((rb,) + src.shape[1:], lambda i, dest: (i, 0, 0)),
                      pl.BlockSpec(memory_space=pl.ANY)],
            out_specs=pl.BlockSpec(memory_space=pl.ANY),
            scratch_shapes=[pltpu.SemaphoreType.DMA]),
        out_shape=jax.ShapeDtypeStruct(dst0.shape, src.dtype),
        input_output_aliases={2: 0},
        compiler_params=_cp(("arbitrary",)), name="row_scatter",
    )(dest, src, dst0)


def _experts_kernel(elo_ref, ehi_ref, nused_ref, x_ref, wrt_ref,
                    wg_lo, wu_lo, wd_lo, wg_hi, wu_hi, wd_hi, y_ref):
    i = pl.program_id(0)

    @pl.when(i < nused_ref[0])
    def _():
        x = x_ref[...].reshape(x_ref.shape[0], -1)
        xb = x.astype(BF16)
        aff_lo = _sigmoid(jnp.sum(x * wrt_ref[pl.ds(elo_ref[i], 1), :], axis=1, keepdims=True))
        aff_hi = _sigmoid(jnp.sum(x * wrt_ref[pl.ds(ehi_ref[i], 1), :], axis=1, keepdims=True))
        tot = aff_lo + aff_hi
        acc = None
        for gate, wg, wu, wd in ((aff_lo / tot, wg_lo, wu_lo, wd_lo), (aff_hi / tot, wg_hi, wu_hi, wd_hi)):
            hg = _dot(xb, wg[...])
            hu = _dot(xb, wu[...])
            act = hg * _sigmoid(hg) * hu * gate
            part = _dot(act.astype(BF16), wd[...])
            acc = part if acc is None else acc + part
        y_ref[...] = acc.reshape(y_ref.shape)

    @pl.when(i >= nused_ref[0])
    def _():
        y_ref[...] = jnp.zeros_like(y_ref)


def experts(xs, w_router_t, blk_lo, blk_hi, nused, w_gate, w_up, w_down):
    p = xs.shape[0]
    d = xs.shape[1] * xs.shape[2]
    nblk = p // MB
    de = w_gate.shape[2]
    xrow = pl.BlockSpec((MB,) + xs.shape[1:], lambda i, lo, hi, nu: (i, 0, 0))
    wrt = pl.BlockSpec(w_router_t.shape, lambda i, lo, hi, nu: (0, 0))
    w_in_lo = pl.BlockSpec((None, d, de), lambda i, lo, hi, nu: (lo[i], 0, 0))
    w_in_hi = pl.BlockSpec((None, d, de), lambda i, lo, hi, nu: (hi[i], 0, 0))
    w_out_lo = pl.BlockSpec((None, de, d), lambda i, lo, hi, nu: (lo[i], 0, 0))
    w_out_hi = pl.BlockSpec((None, de, d), lambda i, lo, hi, nu: (hi[i], 0, 0))
    return pl.pallas_call(
        _experts_kernel,
        grid_spec=pltpu.PrefetchScalarGridSpec(
            num_scalar_prefetch=3, grid=(nblk,),
            in_specs=[xrow, wrt, w_in_lo, w_in_lo, w_out_lo, w_in_hi, w_in_hi, w_out_hi],
            out_specs=xrow),
        out_shape=jax.ShapeDtypeStruct(xs.shape, F32),
        compiler_params=_cp(("arbitrary",)), name="experts",
    )(blk_lo, blk_hi, nused, xs, w_router_t, w_gate, w_up, w_down, w_gate, w_up, w_down)


_PAIR_LO = (0, 0, 0, 1, 1, 2)
_PAIR_HI = (1, 2, 3, 2, 3, 3)


def moe(h2, cls, rank, counts, w_router_t, w_gate, w_up, w_down):
    t = h2.shape[0]
    p = t + N_CLASSES * MB
    nblk = p // MB
    cls = cls.reshape(t)
    counts = counts[:N_CLASSES, 0].astype(jnp.int32)
    padded = (counts + MB - 1) // MB * MB
    pad_end = jnp.cumsum(padded)
    pad_start = pad_end - padded
    dest = (pad_start[cls] + rank.reshape(t)).astype(jnp.int32)
    blk_row = jnp.arange(nblk, dtype=jnp.int32) * MB
    blk_cls = jnp.minimum(jnp.sum((pad_end[None, :] <= blk_row[:, None]).astype(jnp.int32), axis=1), N_CLASSES - 1)
    grp = blk_cls // N_PAIRS
    pr = blk_cls % N_PAIRS
    blk_lo = grp * EPG + jnp.asarray(_PAIR_LO, jnp.int32)[pr]
    blk_hi = grp * EPG + jnp.asarray(_PAIR_HI, jnp.int32)[pr]
    nused = (pad_end[-1:] // MB).astype(jnp.int32)
    xs = row_scatter(dest, h2, p)
    ys = experts(xs, w_router_t, blk_lo, blk_hi, nused, w_gate, w_up, w_down)
    return ys, dest


def _final_kernel(*refs):
    _, h, rest, issue, finish = _prologue(True, refs)
    n = 8
    rows = h.shape[0] // n
    for g in range(n):
        issue(g, n)
        rest[0][g * rows:(g + 1) * rows, :] = h[g * rows:(g + 1) * rows, :]
    finish()


def final_norm(x, moe_out, gf, nw, shift, scale, bsz, seq):
    t, d = x.shape
    tm = min(TM, seq)
    ns = seq // tm
    ys, dest = moe_out
    row = pl.BlockSpec((tm, d), lambda b, s, *_: (b * ns + s, 0))
    return _prologue_call(_final_kernel, True, dest, [x, ys, gf, nw, shift, scale], _prologue_specs(True, ns, d, tm),
                          row, jax.ShapeDtypeStruct((t, d), F32), bsz, ns, tm, d, "final_norm")


def kernel(x, c, a_w_in, a_b_gates, a_mh_norm, a_w_out, kv_norm, w_ada_kv, b_ada_kv, w_kv, b_fgate, b_w_q, b_w_o,
           norm_mix, norm_ffn, w_ada, b_ada, w_router, router_bias, w_gate, w_up, w_down, norm_final, w_ada_final,
           b_ada_final):
    bsz, seq, d = x.shape
    depth = w_ada.shape[0]
    n_a = a_w_in.shape[0]
    t = bsz * seq
    xf = x.reshape(t, d)

    mods = ada(c, w_ada, b_ada)
    kv_mod = ada(c, w_ada_kv[None], b_ada_kv[None])[0]
    fin = ada(c, w_ada_final[None], b_ada_final[None])[0]

    def vecs(m, n):
        return [m[:, None, i * d:(i + 1) * d] for i in range(n)]

    wg_b = w_gate.astype(BF16)
    wu_b = w_up.astype(BF16)
    wd_b = w_down.astype(BF16)

    moe_out = None
    gf_prev = None
    kv = None
    for layer in range(depth):
        sh_m, sc_m, g_m, sh_f, sc_f, g_f = vecs(mods[layer], 6)
        nm = norm_mix[layer].reshape(1, d)
        if layer < n_a:
            xf, (q, k, v, o, grow, gcol) = mlstm_in(xf, moe_out, gf_prev, nm, sh_m, sc_m, a_w_in[layer],
                                                     a_b_gates[layer], bsz, seq)
            hs = mlstm_cell(q, k, v, grow, gcol, bsz, seq)
            mix_args = (hs, o, a_mh_norm[layer].reshape(1, d))
            w_o = a_w_out[layer]
        else:
            j = layer - n_a
            xf, q = fox_q(xf, moe_out, gf_prev, nm, sh_m, sc_m, b_w_q[j], bsz, seq)
            if layer == n_a:
                kv_sh, kv_sc = vecs(kv_mod, 2)
                kv = shared_kv(xf, kv_norm.reshape(1, d), kv_sh, kv_sc, w_kv, b_fgate, bsz, seq)
            ke, ko, vt, eq = kv
            att = fox_attn(q, eq, ke, ko, vt, bsz, seq)
            mix_args = (att,)
            w_o = b_w_o[j]
        xf, h2, cls, rank, counts = post_mix(
            mix_args, layer < n_a, w_o, xf, g_m, norm_ffn[layer].reshape(1, d), sh_f, sc_f, w_router, router_bias,
            bsz, seq)
        moe_out = moe(h2, cls, rank, counts, w_router.T, wg_b[layer], wu_b[layer], wd_b[layer])
        gf_prev = g_f
    fin_sh, fin_sc = vecs(fin, 2)
    out = final_norm(xf, moe_out, gf_prev, norm_final.reshape(1, d), fin_sh, fin_sc, bsz, seq)
    return out.reshape(bsz, seq, d)
```

```python
import functools

import jax
import jax.numpy as jnp
from jax import lax
from jax.experimental import pallas as pl
from jax.experimental.pallas import tpu as pltpu

F32 = jnp.float32
BF16 = jnp.bfloat16
EPS = 1e-6
GATE_SOFTCAP = 15.0

A_HEADS = 8
B_HEADS = 16
N_EXPERTS = 16
N_GROUPS = 4
EPG = N_EXPERTS // N_GROUPS
N_PAIRS = 6
N_CLASSES = N_GROUPS * N_PAIRS
CLS_PAD = 32

VMEM_LIMIT = 56 * 1024 * 1024

TM = 512
A_L = 256
TQ = 1024
ATT_CHUNK = 256
MB = 256
RB = 2048
DMA_GROUP = 8
DMA_QUEUES = 2
PIECE = 256
LANES = 128
LOG2E = 1.4426950408889634

HIGHEST = lax.Precision.HIGHEST


def _cp(sem):
    return pltpu.CompilerParams(dimension_semantics=sem, vmem_limit_bytes=VMEM_LIMIT)


def _dot(a, b, precision=None):
    return jnp.dot(a, b, preferred_element_type=F32, precision=precision)


def _dot_nt(a, b, precision=None):
    return lax.dot_general(a, b, (((1,), (1,)), ((), ())), preferred_element_type=F32, precision=precision)


def _dot_tn(a, b, precision=None):
    return lax.dot_general(a, b, (((0,), (0,)), ((), ())), preferred_element_type=F32, precision=precision)


def _norm_mod(xin, nw, shift, scale):
    ms = jnp.mean(xin * xin, axis=-1, keepdims=True)
    y = xin * lax.rsqrt(ms + EPS)
    return (y * nw) * (1.0 + scale) + shift


def _log_sigmoid(x):
    return jnp.minimum(x, 0.0) - jnp.log1p(jnp.exp(-jnp.abs(x)))


def _sigmoid(x):
    return 1.0 / (1.0 + jnp.exp(-x))


def _ada_kernel(c_ref, w_ref, b_ref, o_ref):
    c = c_ref[...]
    ca = c * _sigmoid(c)
    o_ref[...] = _dot(ca, w_ref[...], HIGHEST) + b_ref[...]


def ada(c, w, b):
    nl, d, n = w.shape
    bsz = c.shape[0]
    tn = 512
    return pl.pallas_call(
        _ada_kernel,
        grid=(nl, n // tn),
        in_specs=[
            pl.BlockSpec((bsz, d), lambda l, j: (0, 0)),
            pl.BlockSpec((None, d, tn), lambda l, j: (l, 0, j)),
            pl.BlockSpec((None, 1, tn), lambda l, j: (l, 0, j)),
        ],
        out_specs=pl.BlockSpec((None, bsz, tn), lambda l, j: (l, 0, j)),
        out_shape=jax.ShapeDtypeStruct((nl, bsz, n), F32),
        compiler_params=_cp(("arbitrary", "arbitrary")),
        name="ada",
    )(c, w, b.reshape(nl, 1, n))


def _prologue(has_moe, refs):
    if not has_moe:
        x_ref, nw_ref, sh_ref, sc_ref = refs[:4]
        xin = x_ref[...]
        h = _norm_mod(xin, nw_ref[...], sh_ref[...], sc_ref[...])
        return xin, h, refs[4:], (lambda g, n: None), (lambda: None)
    dest_ref, x_ref, ys_ref, gf_ref, nw_ref, sh_ref, sc_ref = refs[:7]
    gbuf, gsem = refs[-2:]
    tm = x_ref.shape[0]
    lin = pl.program_id(0) * pl.num_programs(1) + pl.program_id(1)
    last = pl.num_programs(0) * pl.num_programs(1) - 1
    slot = lin % 2

    def start_rows(tile, slot_, lo, hi):
        base = tile * tm
        for r in range(lo, hi):
            pltpu.make_async_copy(ys_ref.at[dest_ref[base + r]], gbuf.at[slot_, r], gsem.at[slot_]).start(
                priority=r % DMA_QUEUES)

    def wait_rows(slot_):
        def body(r, carry):
            pltpu.make_async_copy(ys_ref.at[0], gbuf.at[slot_, 0], gsem.at[slot_]).wait()
            return carry
        lax.fori_loop(0, tm, body, 0, unroll=8)

    @pl.when(lin == 0)
    def _():
        start_rows(0, 0, 0, tm)

    wait_rows(slot)
    xin = x_ref[...] + gf_ref[...] * gbuf[slot].reshape(x_ref.shape)
    h = _norm_mod(xin, nw_ref[...], sh_ref[...], sc_ref[...])
    nxt = jnp.minimum(lin + 1, last)

    def issue(g, n):
        start_rows(nxt, 1 - slot, g * tm // n, (g + 1) * tm // n)

    def finish():
        @pl.when(lin == last)
        def _():
            wait_rows(1 - slot)

    return xin, h, refs[7:-2], issue, finish


def _prologue_specs(has_moe, ns, d, tm):
    row = pl.BlockSpec((tm, d), lambda b, s, *_: (b * ns + s, 0))
    vec = pl.BlockSpec((None, 1, d), lambda b, s, *_: (b, 0, 0))
    one = pl.BlockSpec((1, d), lambda b, s, *_: (0, 0))
    if has_moe:
        return [row, pl.BlockSpec(memory_space=pl.ANY), vec, one, vec, vec]
    return [row, one, vec, vec]


def _prologue_call(kernel_fn, has_moe, dest, args, in_specs, out_specs, out_shape, bsz, ns, tm, d, name):
    if not has_moe:
        return pl.pallas_call(
            kernel_fn, grid=(bsz, ns), in_specs=in_specs, out_specs=out_specs, out_shape=out_shape,
            compiler_params=_cp(("arbitrary", "arbitrary")), name=name)(*args)
    return pl.pallas_call(
        kernel_fn,
        grid_spec=pltpu.PrefetchScalarGridSpec(
            num_scalar_prefetch=1, grid=(bsz, ns), in_specs=in_specs, out_specs=out_specs,
            scratch_shapes=[pltpu.VMEM((2, tm, d // LANES, LANES), F32), pltpu.SemaphoreType.DMA((2,))]),
        out_shape=out_shape, compiler_params=_cp(("arbitrary", "arbitrary")), name=name)(dest, *args)


def _full(shape):
    nd = len(shape)
    return pl.BlockSpec(shape, lambda b, s, *_: (0,) * nd)


def _mlstm_in_kernel(has_moe, dqk, *refs):
    xin, h, rest, issue, finish = _prologue(has_moe, refs)
    wqt_ref, wk_ref, wvt_ref, wo_ref, wgt_ref, wg_ref, bcol_ref, brow_ref = rest[:8]
    outs = rest[8:]
    if has_moe:
        xnew_ref = outs[0]
        outs = outs[1:]
        xnew_ref[...] = xin
    qt_ref, k_ref, vt_ref, o_ref, grow_ref, gcol_ref = outs
    hb = h.astype(BF16)
    scale = float(dqk // A_HEADS) ** -0.5
    pieces = []
    for kind, w_ref, out_ref, sc in (("t", wqt_ref, qt_ref, scale), ("n", wk_ref, k_ref, None),
                                     ("t", wvt_ref, vt_ref, None), ("n", wo_ref, o_ref, None)):
        nf = w_ref.shape[0] if kind == "t" else w_ref.shape[1]
        for f0 in range(0, nf, PIECE):
            pieces.append((kind, w_ref, out_ref, slice(f0, f0 + PIECE), sc))
    for g, (kind, w_ref, out_ref, fs, sc) in enumerate(pieces):
        issue(g, len(pieces))
        if kind == "t":
            res = _dot_nt(w_ref[fs, :], hb)
            out_ref[fs, :] = (res if sc is None else res * sc).astype(BF16)
        else:
            out_ref[:, fs] = _dot(hb, w_ref[:, fs]).astype(BF16)
    gr = _dot_nt(wgt_ref[...], hb) + bcol_ref[...]
    gr = GATE_SOFTCAP * jnp.tanh(gr / GATE_SOFTCAP)
    ridx = lax.broadcasted_iota(jnp.int32, gr.shape, 0)
    grow_ref[...] = jnp.where(ridx < A_HEADS, gr, _log_sigmoid(gr))
    gc = _dot(hb, wg_ref[...]) + brow_ref[...]
    gc = GATE_SOFTCAP * jnp.tanh(gc / GATE_SOFTCAP)
    cidx = lax.broadcasted_iota(jnp.int32, gc.shape, 1)
    gcol_ref[...] = jnp.where(cidx < A_HEADS, gc, _log_sigmoid(gc))
    finish()


def mlstm_in(x, moe, gf, nw, shift, scale, w_in, b_gates, bsz, seq):
    t, d = x.shape
    dqk, dv = d // 2, d
    has_moe = moe is not None
    tm = min(TM, seq)
    ns = seq // tm
    w_b = w_in.astype(BF16)
    w_qt = w_b[:, 0:dqk].T
    w_k = w_b[:, dqk:2 * dqk]
    w_vt = w_b[:, 2 * dqk:2 * dqk + dv].T
    w_o = w_b[:, 2 * dqk + dv:2 * dqk + 2 * dv]
    w_g = w_b[:, 2 * dqk + 2 * dv:]
    ng = 2 * A_HEADS
    ys, dest = moe if has_moe else (None, None)
    args = [x] + ([ys, gf] if has_moe else []) + [nw, shift, scale, w_qt, w_k, w_vt, w_o, w_g.T, w_g,
                                                   b_gates.reshape(ng, 1), b_gates.reshape(1, ng)]
    in_specs = _prologue_specs(has_moe, ns, d, tm) + [
        _full((dqk, d)), _full((d, dqk)), _full((dv, d)), _full((d, dv)), _full((ng, d)), _full((d, ng)),
        _full((ng, 1)), _full((1, ng))]
    row = lambda n: pl.BlockSpec((tm, n), lambda b, s, *_: (b * ns + s, 0))
    col = lambda n: pl.BlockSpec((n, tm), lambda b, s, *_: (0, b * ns + s))
    out_specs = [col(dqk), row(dqk), col(dv), row(dv),
                 pl.BlockSpec((None, ng, tm), lambda b, s, *_: (b, 0, s)),
                 pl.BlockSpec((None, tm, ng), lambda b, s, *_: (b, s, 0))]
    out_shape = [jax.ShapeDtypeStruct((dqk, t), BF16), jax.ShapeDtypeStruct((t, dqk), BF16),
                 jax.ShapeDtypeStruct((dv, t), BF16), jax.ShapeDtypeStruct((t, dv), BF16),
                 jax.ShapeDtypeStruct((bsz, ng, seq), F32), jax.ShapeDtypeStruct((bsz, seq, ng), F32)]
    if has_moe:
        out_specs = [row(d)] + out_specs
        out_shape = [jax.ShapeDtypeStruct((t, d), F32)] + out_shape
    res = _prologue_call(functools.partial(_mlstm_in_kernel, has_moe, dqk), has_moe, dest, args, in_specs, out_specs,
                         out_shape, bsz, ns, tm, d, "mlstm_in")
    if has_moe:
        return res[0], res[1:]
    return x, res


N_ROWS = 16


def _mlstm_cell_kernel(qt_ref, k_ref, vt_ref, grow_ref, gcol_ref, hs_ref, ct_ref, m_ref):
    c = pl.program_id(1)
    L = k_ref.shape[0]
    dk = k_ref.shape[1] // A_HEADS
    dvh = vt_ref.shape[0] // A_HEADS

    @pl.when(c == 0)
    def _():
        ct_ref[...] = jnp.zeros_like(ct_ref)
        m_ref[...] = jnp.zeros_like(m_ref)

    r = lax.broadcasted_iota(jnp.int32, (L, L), 0)
    cc = lax.broadcasted_iota(jnp.int32, (L, L), 1)
    visible = r <= cc
    upper = visible.astype(F32)
    lower = (cc <= r).astype(F32)
    grow = grow_ref[...]
    gcol = gcol_ref[...]
    b_row_all = _dot(grow[A_HEADS:, :], upper, HIGHEST)
    b_col_all = _dot(lower, gcol[:, A_HEADS:], HIGHEST)
    x_col_all = gcol[:, :A_HEADS] - b_col_all
    ones_rows = jnp.ones((N_ROWS, L), BF16)

    for h in range(A_HEADS):
        qt = qt_ref[h * dk:(h + 1) * dk, :]
        kh = k_ref[:, h * dk:(h + 1) * dk]
        v_aug = jnp.concatenate([vt_ref[h * dvh:(h + 1) * dvh, :], ones_rows], axis=0)
        br = b_row_all[h:h + 1, :]
        igr = grow[h:h + 1, :]
        xc = x_col_all[:, h:h + 1]
        m_prev = m_ref[h:h + 1, 0:1]
        ct = ct_ref[h]

        d_intra = jnp.where(visible, br + xc, -jnp.inf)
        d_inter = br + m_prev
        m_t = jnp.maximum(d_inter, jnp.max(d_intra, axis=0, keepdims=True))
        w = jnp.exp(d_intra - m_t) * _dot(kh, qt)
        a = jnp.exp(d_inter - m_t)
        tot = _dot(v_aug, w.astype(BF16)) + a * _dot(ct.astype(BF16), qt)
        num = tot[:dvh, :]
        den = tot[dvh:dvh + 1, :]
        hh = num / jnp.maximum(jnp.abs(den), jnp.exp(-m_t))
        hh = hh * lax.rsqrt(jnp.mean(hh * hh, axis=0, keepdims=True) + EPS)
        hs_ref[:, h * dvh:(h + 1) * dvh] = hh.T.astype(hs_ref.dtype)

        b_last = br[:, L - 1:L]
        g = b_last - br + igr
        m_new = jnp.maximum(b_last + m_prev, jnp.max(g, axis=1, keepdims=True))
        wk = jnp.exp(g - m_new)
        decay = jnp.exp(b_last + m_prev - m_new)
        ct_ref[h] = decay * ct + _dot((v_aug.astype(F32) * wk).astype(BF16), kh)
        m_ref[h:h + 1, :] = jnp.broadcast_to(m_new, (1, m_ref.shape[1]))


def mlstm_cell(qt, k, vt, grow, gcol, bsz, seq):
    t, dqk = k.shape
    dv = vt.shape[0]
    L = min(A_L, seq)
    nc = seq // L
    ng = grow.shape[1]
    dk = dqk // A_HEADS
    dvh = dv // A_HEADS
    return pl.pallas_call(
        _mlstm_cell_kernel,
        grid=(bsz, nc),
        in_specs=[pl.BlockSpec((dqk, L), lambda b, c: (0, b * nc + c)),
                  pl.BlockSpec((L, dqk), lambda b, c: (b * nc + c, 0)),
                  pl.BlockSpec((dv, L), lambda b, c: (0, b * nc + c)),
                  pl.BlockSpec((None, ng, L), lambda b, c: (b, 0, c)),
                  pl.BlockSpec((None, L, ng), lambda b, c: (b, c, 0))],
        out_specs=pl.BlockSpec((L, dv), lambda b, c: (b * nc + c, 0)),
        out_shape=jax.ShapeDtypeStruct((t, dv), BF16),
        scratch_shapes=[pltpu.VMEM((A_HEADS, dvh + N_ROWS, dk), F32), pltpu.VMEM((A_HEADS, 128), F32)],
        compiler_params=_cp(("arbitrary", "arbitrary")), name="mlstm_cell",
    )(qt, k, vt, grow, gcol)


def _fox_q_kernel(has_moe, qscale, *refs):
    xin, h, rest, issue, finish = _prologue(has_moe, refs)
    w_ref = rest[0]
    outs = rest[1:]
    if has_moe:
        outs[0][...] = xin
        outs = outs[1:]
    hb = h.astype(BF16)
    d = w_ref.shape[1]
    n = d // PIECE
    for g in range(n):
        issue(g, n)
        cols = slice(g * PIECE, (g + 1) * PIECE)
        outs[0][:, cols] = (_dot(hb, w_ref[:, cols]) * qscale).astype(BF16)
    finish()


def fox_q(x, moe, gf, nw, shift, scale, w_q, bsz, seq):
    t, d = x.shape
    has_moe = moe is not None
    tm = min(TM, seq)
    ns = seq // tm
    ys, dest = moe if has_moe else (None, None)
    args = [x] + ([ys, gf] if has_moe else []) + [nw, shift, scale, w_q.astype(BF16)]
    in_specs = _prologue_specs(has_moe, ns, d, tm) + [_full((d, d))]
    row = pl.BlockSpec((tm, d), lambda b, s, *_: (b * ns + s, 0))
    out_specs = [row]
    out_shape = [jax.ShapeDtypeStruct((t, d), BF16)]
    if has_moe:
        out_specs = [row] + out_specs
        out_shape = [jax.ShapeDtypeStruct((t, d), F32)] + out_shape
    qscale = float(d // B_HEADS) ** -0.5 * LOG2E
    res = _prologue_call(functools.partial(_fox_q_kernel, has_moe, qscale), has_moe, dest, args, in_specs, out_specs,
                         out_shape, bsz, ns, tm, d, "fox_q")
    if has_moe:
        return res[0], res[1]
    return x, res[0]


N_BIAS_PIECES = 3


def _bias_select_matrices(d, n_heads):
    import numpy as np
    dh = d // n_heads
    pq = np.zeros((N_BIAS_PIECES, n_heads, d), np.float32)
    pk = np.zeros((N_BIAS_PIECES, n_heads, d), np.float32)
    oq = np.zeros((1, d), np.float32)
    ok = np.zeros((1, d), np.float32)
    for h in range(n_heads):
        base = (h // 2) * 2 * dh + (dh if h % 2 == 0 else 0)
        for p in range(N_BIAS_PIECES):
            pq[p, h, base + p] = 1.0
            ok[0, base + p] = 1.0
            oq[0, base + N_BIAS_PIECES + p] = 1.0
            pk[p, h, base + N_BIAS_PIECES + p] = -1.0
    return pq, pk, oq, ok


def _split3(f):
    a = f.astype(BF16)
    r1 = f - a.astype(F32)
    b = r1.astype(BF16)
    c = (r1 - b.astype(F32)).astype(BF16)
    return a, b, c


def _shared_kv_kernel(x_ref, nw_ref, sh_ref, sc_ref, w_ref, wvt_ref, wf_ref, brow_ref, lo_ref, pq_ref, pk_ref, oq_ref,
                      ok_ref, ke_ref, ko_ref, vt_ref, eq_ref, cc_ref):
    s = pl.program_id(1)
    d = x_ref.shape[1]
    dh = d // B_HEADS

    @pl.when(s == 0)
    def _():
        cc_ref[...] = jnp.zeros_like(cc_ref)

    h = _norm_mod(x_ref[...], nw_ref[...], sh_ref[...], sc_ref[...])
    hb = h.astype(BF16)
    lc = _log_sigmoid(_dot(hb, wf_ref[...]) + brow_ref[...])
    fcum = _dot(lo_ref[...], lc, HIGHEST) + cc_ref[...]
    cc_ref[...] += jnp.sum(lc, axis=0, keepdims=True)
    pieces = jnp.concatenate(_split3(fcum * LOG2E), axis=1)
    eq = oq_ref[...] + _dot(pieces, pq_ref[...])
    ek = ok_ref[...] + _dot(pieces, pk_ref[...])
    eq_ref[...] = eq.astype(BF16)
    even = (lax.broadcasted_iota(jnp.int32, (1, d), 1) % (2 * dh)) < dh
    k = _dot(hb, w_ref[...])
    ke_ref[...] = jnp.where(even, k, ek).astype(BF16)
    ko_ref[...] = jnp.where(even, ek, k).astype(BF16)
    vt_ref[...] = _dot_nt(wvt_ref[...], hb).astype(BF16)


def shared_kv(x, nw, shift, scale, w_kv, b_fgate, bsz, seq):
    t, d = x.shape
    tm = min(TM, seq)
    ns = seq // tm
    nh = B_HEADS
    w_k = w_kv[:, :d].astype(BF16)
    w_vt = w_kv[:, d:2 * d].astype(BF16).T
    w_f = w_kv[:, 2 * d:].astype(BF16)
    idx = jnp.arange(tm)
    lower = (idx[None, :] <= idx[:, None]).astype(F32)
    pq, pk, oq, ok = _bias_select_matrices(d, nh)
    row = pl.BlockSpec((tm, d), lambda b, s: (b * ns + s, 0))
    vec = pl.BlockSpec((None, 1, d), lambda b, s: (b, 0, 0))
    sel = _full((N_BIAS_PIECES * nh, d))
    return pl.pallas_call(
        _shared_kv_kernel,
        grid=(bsz, ns),
        in_specs=[row, _full((1, d)), vec, vec, _full((d, d)), _full((d, d)), _full((d, nh)), _full((1, nh)),
                  _full((tm, tm)), sel, sel, _full((1, d)), _full((1, d))],
        out_specs=[row, row, pl.BlockSpec((d, tm), lambda b, s: (0, b * ns + s)), row],
        out_shape=[jax.ShapeDtypeStruct((t, d), BF16), jax.ShapeDtypeStruct((t, d), BF16),
                   jax.ShapeDtypeStruct((d, t), BF16), jax.ShapeDtypeStruct((t, d), BF16)],
        scratch_shapes=[pltpu.VMEM((1, nh), F32)],
        compiler_params=_cp(("arbitrary", "arbitrary")), name="shared_kv",
    )(x, nw, shift, scale, w_k, w_vt, w_f, b_fgate.reshape(1, nh), lower,
      jnp.asarray(pq.reshape(-1, d), BF16), jnp.asarray(pk.reshape(-1, d), BF16), jnp.asarray(oq), jnp.asarray(ok))


def _fox_attn_kernel(q_ref, eq_ref, ke_ref, ko_ref, vt_ref, o_ref, st_ref):
    i = pl.program_id(2)
    tq = q_ref.shape[0]
    tk = tq
    ck = min(ATT_CHUNK, tk)
    nchunk = tk // ck
    dh2 = q_ref.shape[1]
    dh = dh2 // 2
    lo_mask = lax.broadcasted_iota(jnp.int32, (1, dh2), 1) < dh
    q = q_ref[...]
    eq = eq_ref[...]
    qs = (jnp.where(lo_mask, q, eq), jnp.where(lo_mask, eq, q))
    k_refs = (ke_ref, ko_ref)
    ones_rows = jnp.ones((N_ROWS, ck), BF16)

    def tail(x, q0, new):
        return new if q0 == 0 else jnp.concatenate([x[:, :q0], new], axis=1)

    def qk_chunk(e, kt, c, mx, diag):
        off = pl.multiple_of(kt * tk + c * ck, ck)
        q0 = c * ck if diag else 0
        st = _dot_nt(k_refs[e][pl.ds(off, ck), :], qs[e][q0:, :])
        if diag:
            visible = (lax.broadcasted_iota(jnp.int32, st.shape, 0) <= lax.broadcasted_iota(jnp.int32, st.shape, 1))
            st = jnp.where(visible, st, -jnp.inf)
        st_ref[e, c * ck:(c + 1) * ck, q0:] = st
        cm = jnp.max(st, axis=0, keepdims=True)
        return cm if mx is None else tail(mx, q0, jnp.maximum(mx[:, q0:], cm))

    def pv_chunk(e, kt, c, state, diag):
        m, acc = state
        off = pl.multiple_of(kt * tk + c * ck, ck)
        q0 = c * ck if diag else 0
        pt = jnp.exp2(st_ref[e, c * ck:(c + 1) * ck, q0:] - m[:, q0:])
        v_aug = jnp.concatenate([vt_ref[e * dh:(e + 1) * dh, pl.ds(off, ck)], ones_rows], axis=0)
        return m, tail(acc, q0, acc[:, q0:] + _dot(v_aug, pt.astype(BF16)))

    def phase(cur, state, tile_max, nxt):
        m, acc = state
        m_new = jnp.maximum(m, tile_max)
        state = (m_new, jnp.exp2(m - m_new) * acc)
        mx = None
        for c in range(nchunk):
            mx = qk_chunk(nxt[0], nxt[1], c, mx, nxt[2])
            state = pv_chunk(cur[0], cur[1], c, state, cur[2])
        return state, mx

    init = (jnp.full((1, tq), -1e30, F32), jnp.zeros((dh + N_ROWS, tq), F32))
    mx0 = None
    for c in range(nchunk):
        mx0 = qk_chunk(0, i, c, mx0, True)
    s0, mx1 = phase((0, i, True), init, mx0, (1, i, True))
    s1, mx0 = phase((1, i, True), init, mx1, (0, 0, False))

    def trip(j, carry):
        s0, s1, mx0 = carry
        s0, mx1 = phase((0, j, False), s0, mx0, (1, j, False))
        s1, mx0 = phase((1, j, False), s1, mx1, (0, j + 1, False))
        return s0, s1, mx0

    s0, s1, _ = lax.fori_loop(0, i, trip, (s0, s1, mx0))
    ot = jnp.concatenate([acc[:dh] / acc[dh:dh + 1] for _, acc in (s0, s1)], axis=0)
    o_ref[...] = ot.T.astype(o_ref.dtype)


def fox_attn(q, eq, ke, ko, vt, bsz, seq):
    t, d = q.shape
    nh = B_HEADS
    dh2 = 2 * (d // nh)
    tq = min(TQ, seq)
    nq = seq // tq
    qspec = pl.BlockSpec((tq, dh2), lambda b, hp, i: (b * nq + i, hp))
    kspec = pl.BlockSpec((seq, dh2), lambda b, hp, i: (b, hp))
    vspec = pl.BlockSpec((dh2, seq), lambda b, hp, i: (hp, b))
    return pl.pallas_call(
        _fox_attn_kernel,
        grid=(bsz, nh // 2, nq),
        in_specs=[qspec, qspec, kspec, kspec, vspec],
        out_specs=qspec,
        out_shape=jax.ShapeDtypeStruct((t, d), BF16),
        scratch_shapes=[pltpu.VMEM((2, tq, tq), F32)],
        compiler_params=_cp(("arbitrary", "arbitrary", "arbitrary")), name="fox_attn",
    )(q, eq, ke, ko, vt)


def _route(logits, bias_col):
    aff = _sigmoid(logits)
    sel = aff + bias_col
    s = [sel[e:e + 1, :] for e in range(N_EXPERTS)]
    gs = []
    for g in range(N_GROUPS):
        v = s[EPG * g:EPG * (g + 1)]
        best = v[0] + v[1]
        for i in range(EPG):
            for j in range(i + 1, EPG):
                if (i, j) != (0, 1):
                    best = jnp.maximum(best, v[i] + v[j])
        gs.append(best)
    grp = jnp.zeros_like(gs[0], dtype=jnp.int32)
    best = gs[0]
    for g in range(1, N_GROUPS):
        upd = gs[g] > best
        grp = jnp.where(upd, g, grp)
        best = jnp.where(upd, gs[g], best)

    def pick(arrs, j):
        out = arrs[(N_GROUPS - 1) * EPG + j]
        for g in range(N_GROUPS - 2, -1, -1):
            out = jnp.where(grp == g, arrs[g * EPG + j], out)
        return out

    v = [pick(s, j) for j in range(EPG)]
    i1 = jnp.zeros_like(grp)
    b1 = v[0]
    for j in range(1, EPG):
        upd = v[j] > b1
        i1 = jnp.where(upd, j, i1)
        b1 = jnp.where(upd, v[j], b1)
    i2 = jnp.full_like(grp, -1)
    b2 = jnp.full_like(b1, -jnp.inf)
    for j in range(EPG):
        upd = (i1 != j) & ((i2 < 0) | (v[j] > b2))
        i2 = jnp.where(upd, j, i2)
        b2 = jnp.where(upd, v[j], b2)
    lo = jnp.minimum(i1, i2)
    hi = jnp.maximum(i1, i2)
    base = jnp.where(lo == 0, 0, jnp.where(lo == 1, 3, 5))
    return grp * N_PAIRS + base + (hi - lo - 1)


def _post_mix_kernel(is_mlstm, *refs):
    if is_mlstm:
        hs_ref, og_ref, mh_ref = refs[:3]
        refs = refs[3:]
        og = og_ref[...].astype(F32)
        mix = hs_ref[...].astype(F32) * mh_ref[...] * _sigmoid(og)
    else:
        mix = refs[0][...]
        refs = refs[1:]
    (w_ref, x_ref, gm_ref, nw_ref, sh_ref, sc_ref, wrt_ref, rb_ref, us_ref,
     xnew_ref, h2_ref, cls_ref, rank_ref, cnt_ref) = refs
    first = (pl.program_id(0) == 0) & (pl.program_id(1) == 0)

    @pl.when(first)
    def _():
        cnt_ref[...] = jnp.zeros_like(cnt_ref)

    y = _dot(mix.astype(BF16), w_ref[...])
    xnew = x_ref[...] + gm_ref[...] * y
    xnew_ref[...] = xnew
    h2 = _norm_mod(xnew, nw_ref[...], sh_ref[...], sc_ref[...])
    h2_ref[...] = h2.reshape(h2_ref.shape)
    h_hi = h2.astype(BF16)
    h_lo = (h2 - h_hi.astype(F32)).astype(BF16)
    ne = wrt_ref.shape[0] // 2
    part = _dot_nt(wrt_ref[...], h_hi)
    logits = part[:ne] + part[ne:] + _dot_nt(wrt_ref[:ne, :], h_lo)
    cls = _route(logits, rb_ref[...])
    cls_ref[...] = cls
    tm = cls.shape[1]
    onehot = (lax.broadcasted_iota(jnp.int32, (CLS_PAD, tm), 0) == cls).astype(F32)
    prefix = _dot(onehot.astype(BF16), us_ref[...])
    carry = cnt_ref[:, 0:1]
    rank = jnp.sum(onehot * (prefix + carry), axis=0, keepdims=True)
    rank_ref[...] = rank.astype(jnp.int32)
    cnt_ref[...] += jnp.sum(onehot, axis=1, keepdims=True)


def post_mix(mix_args, is_mlstm, w_o, x, gm, nw, shift, scale, w_router, router_bias, bsz, seq):
    t, d = x.shape
    tm = min(TM, seq)
    ns = seq // tm
    ne = N_EXPERTS
    idx = jnp.arange(tm)
    upper_strict = (idx[:, None] < idx[None, :]).astype(BF16)
    row = pl.BlockSpec((tm, d), lambda b, s: (b * ns + s, 0))
    vec = pl.BlockSpec((None, 1, d), lambda b, s: (b, 0, 0))
    lane_row = pl.BlockSpec((None, 1, tm), lambda b, s: (b, 0, s))
    if is_mlstm:
        mix_specs = [row, row, _full((1, d))]
    else:
        mix_specs = [row]
    in_specs = mix_specs + [_full((d, d)), row, vec, _full((1, d)), vec, vec, _full((2 * ne, d)), _full((ne, 1)),
                            _full((tm, tm))]
    wr = w_router.T
    wr_hi = wr.astype(BF16)
    wr_split = jnp.concatenate([wr_hi, (wr - wr_hi.astype(F32)).astype(BF16)], axis=0)
    row3 = pl.BlockSpec((tm, d // LANES, LANES), lambda b, s: (b * ns + s, 0, 0))
    out_specs = [row, row3, lane_row, lane_row, _full((CLS_PAD, 128))]
    out_shape = [jax.ShapeDtypeStruct((t, d), F32), jax.ShapeDtypeStruct((t, d // LANES, LANES), F32),
                 jax.ShapeDtypeStruct((bsz, 1, seq), jnp.int32), jax.ShapeDtypeStruct((bsz, 1, seq), jnp.int32),
                 jax.ShapeDtypeStruct((CLS_PAD, 128), F32)]
    return pl.pallas_call(
        functools.partial(_post_mix_kernel, is_mlstm),
        grid=(bsz, ns), in_specs=in_specs, out_specs=out_specs, out_shape=out_shape,
        compiler_params=_cp(("arbitrary", "arbitrary")), name="post_mix",
    )(*mix_args, w_o.astype(BF16), x, gm, nw, shift, scale, wr_split, router_bias.reshape(ne, 1), upper_strict)


_FILL_PIECES = tuple(1 << k for k in range(MB.bit_length() - 2, -1, -1))


def _row_scatter_kernel(rb, dest_ref, fill_start_ref, fill_n_ref, src_ref, dst_ref, zero_ref, sem, zsem):
    base = pl.program_id(0) * rb

    def fill(do):
        for c in range(N_CLASSES):
            n = fill_n_ref[c]
            for p in _FILL_PIECES:
                row = fill_start_ref[c] + (n & ~(2 * p - 1))

                @pl.when((n & p) != 0)
                def _():
                    do(pltpu.make_async_copy(zero_ref.at[pl.ds(0, p)], dst_ref.at[pl.ds(row, p)], zsem))
        for k in range(N_CLASSES):
            row = pl.multiple_of(fill_start_ref[N_CLASSES] + k * MB, MB)

            @pl.when(k < fill_n_ref[N_CLASSES])
            def _():
                do(pltpu.make_async_copy(zero_ref, dst_ref.at[pl.ds(row, MB)], zsem))

    @pl.when(pl.program_id(0) == 0)
    def _():
        zero_ref[...] = jnp.zeros_like(zero_ref)
        fill(lambda cp: cp.start())

    def start(g, carry):
        for u in range(DMA_GROUP):
            rr = g * DMA_GROUP + u
            pltpu.make_async_copy(src_ref.at[rr], dst_ref.at[dest_ref[base + rr]], sem).start(priority=u % DMA_QUEUES)
        return carry

    def wait(rr, carry):
        pltpu.make_async_copy(src_ref.at[0], dst_ref.at[0], sem).wait()
        return carry

    lax.fori_loop(0, rb // DMA_GROUP, start, 0)
    lax.fori_loop(0, rb, wait, 0, unroll=8)

    @pl.when(pl.program_id(0) == 0)
    def _():
        fill(lambda cp: cp.wait())


def row_scatter(dest, fill_start, fill_n, src, n_rows):
    t = src.shape[0]
    rb = min(RB, t)
    return pl.pallas_call(
        functools.partial(_row_scatter_kernel, rb),
        grid_spec=pltpu.PrefetchScalarGridSpec(
            num_scalar_prefetch=3, grid=(t // rb,),
            in_specs=[pl.BlockSpec((rb,) + src.shape[1:], lambda i, *_: (i, 0, 0))],
            out_specs=pl.BlockSpec(memory_space=pl.ANY),
            scratch_shapes=[pltpu.VMEM((MB,) + src.shape[1:], src.dtype),
                            pltpu.SemaphoreType.DMA, pltpu.SemaphoreType.DMA]),
        out_shape=jax.ShapeDtypeStruct((n_rows,) + src.shape[1:], src.dtype),
        compiler_params=_cp(("arbitrary",)), name="row_scatter",
    )(dest, fill_start, fill_n, src)


def _experts_kernel(elo_ref, ehi_ref, nused_ref, x_ref, wrt_ref,
                    wg_lo, wu_lo, wd_lo, wg_hi, wu_hi, wd_hi, y_ref):
    i = pl.program_id(0)

    @pl.when(i < nused_ref[0])
    def _():
        x = x_ref[...].reshape(x_ref.shape[0], -1)
        xb = x.astype(BF16)
        aff_lo = _sigmoid(jnp.sum(x * wrt_ref[pl.ds(elo_ref[i], 1), :], axis=1, keepdims=True))
        aff_hi = _sigmoid(jnp.sum(x * wrt_ref[pl.ds(ehi_ref[i], 1), :], axis=1, keepdims=True))
        tot = aff_lo + aff_hi
        acc = None
        for gate, wg, wu, wd in ((aff_lo / tot, wg_lo, wu_lo, wd_lo), (aff_hi / tot, wg_hi, wu_hi, wd_hi)):
            hg = _dot(xb, wg[...])
            hu = _dot(xb, wu[...])
            act = hg * _sigmoid(hg) * hu * gate
            part = _dot(act.astype(BF16), wd[...])
            acc = part if acc is None else acc + part
        y_ref[...] = acc.reshape(y_ref.shape)

    @pl.when(i >= nused_ref[0])
    def _():
        y_ref[...] = jnp.zeros_like(y_ref)


def experts(xs, w_router_t, blk_lo, blk_hi, nused, w_gate, w_up, w_down):
    p = xs.shape[0]
    d = xs.shape[1] * xs.shape[2]
    nblk = p // MB
    de = w_gate.shape[2]
    xrow = pl.BlockSpec((MB,) + xs.shape[1:], lambda i, lo, hi, nu: (i, 0, 0))
    xrow_in = pl.BlockSpec((MB,) + xs.shape[1:], lambda i, lo, hi, nu: (jnp.minimum(i, nu[0] - 1), 0, 0))
    wrt = pl.BlockSpec(w_router_t.shape, lambda i, lo, hi, nu: (0, 0))
    w_in_lo = pl.BlockSpec((None, d, de), lambda i, lo, hi, nu: (lo[i], 0, 0))
    w_in_hi = pl.BlockSpec((None, d, de), lambda i, lo, hi, nu: (hi[i], 0, 0))
    w_out_lo = pl.BlockSpec((None, de, d), lambda i, lo, hi, nu: (lo[i], 0, 0))
    w_out_hi = pl.BlockSpec((None, de, d), lambda i, lo, hi, nu: (hi[i], 0, 0))
    return pl.pallas_call(
        _experts_kernel,
        grid_spec=pltpu.PrefetchScalarGridSpec(
            num_scalar_prefetch=3, grid=(nblk,),
            in_specs=[xrow_in, wrt, w_in_lo, w_in_lo, w_out_lo, w_in_hi, w_in_hi, w_out_hi],
            out_specs=xrow),
        out_shape=jax.ShapeDtypeStruct(xs.shape, F32),
        compiler_params=_cp(("arbitrary",)), name="experts",
    )(blk_lo, blk_hi, nused, xs, w_router_t, w_gate, w_up, w_down, w_gate, w_up, w_down)


_PAIR_LO = (0, 0, 0, 1, 1, 2)
_PAIR_HI = (1, 2, 3, 2, 3, 3)


def moe(h2, cls, rank, counts, w_router_t, w_gate, w_up, w_down):
    t = h2.shape[0]
    p = t + N_CLASSES * MB
    nblk = p // MB
    cls = cls.reshape(t)
    counts = counts[:N_CLASSES, 0].astype(jnp.int32)
    padded = (counts + MB - 1) // MB * MB
    pad_end = jnp.cumsum(padded)
    pad_start = pad_end - padded
    dest = (pad_start[cls] + rank.reshape(t)).astype(jnp.int32)
    blk_row = jnp.arange(nblk, dtype=jnp.int32) * MB
    blk_cls = jnp.minimum(jnp.sum((pad_end[None, :] <= blk_row[:, None]).astype(jnp.int32), axis=1), N_CLASSES - 1)
    grp = blk_cls // N_PAIRS
    pr = blk_cls % N_PAIRS
    blk_lo = grp * EPG + jnp.asarray(_PAIR_LO, jnp.int32)[pr]
    blk_hi = grp * EPG + jnp.asarray(_PAIR_HI, jnp.int32)[pr]
    nused = (pad_end[-1:] // MB).astype(jnp.int32)
    fill_start = jnp.concatenate([pad_start + counts, pad_end[-1:]]).astype(jnp.int32)
    fill_n = jnp.concatenate([padded - counts, (p - pad_end[-1:]) // MB]).astype(jnp.int32)
    xs = row_scatter(dest, fill_start, fill_n, h2, p)
    ys = experts(xs, w_router_t, blk_lo, blk_hi, nused, w_gate, w_up, w_down)
    return ys, dest


def _final_kernel(*refs):
    _, h, rest, issue, finish = _prologue(True, refs)
    n = 8
    rows = h.shape[0] // n
    for g in range(n):
        issue(g, n)
        rest[0][g * rows:(g + 1) * rows, :] = h[g * rows:(g + 1) * rows, :]
    finish()


def final_norm(x, moe_out, gf, nw, shift, scale, bsz, seq):
    t, d = x.shape
    tm = min(TM, seq)
    ns = seq // tm
    ys, dest = moe_out
    row = pl.BlockSpec((tm, d), lambda b, s, *_: (b * ns + s, 0))
    return _prologue_call(_final_kernel, True, dest, [x, ys, gf, nw, shift, scale], _prologue_specs(True, ns, d, tm),
                          row, jax.ShapeDtypeStruct((t, d), F32), bsz, ns, tm, d, "final_norm")


def kernel(x, c, a_w_in, a_b_gates, a_mh_norm, a_w_out, kv_norm, w_ada_kv, b_ada_kv, w_kv, b_fgate, b_w_q, b_w_o,
           norm_mix, norm_ffn, w_ada, b_ada, w_router, router_bias, w_gate, w_up, w_down, norm_final, w_ada_final,
           b_ada_final):
    bsz, seq, d = x.shape
    depth = w_ada.shape[0]
    n_a = a_w_in.shape[0]
    t = bsz * seq
    xf = x.reshape(t, d)

    mods = ada(c, w_ada, b_ada)
    kv_mod = ada(c, w_ada_kv[None], b_ada_kv[None])[0]
    fin = ada(c, w_ada_final[None], b_ada_final[None])[0]

    def vecs(m, n):
        return [m[:, None, i * d:(i + 1) * d] for i in range(n)]

    wg_b = w_gate.astype(BF16)
    wu_b = w_up.astype(BF16)
    wd_b = w_down.astype(BF16)

    moe_out = None
    gf_prev = None
    kv = None
    for layer in range(depth):
        sh_m, sc_m, g_m, sh_f, sc_f, g_f = vecs(mods[layer], 6)
        nm = norm_mix[layer].reshape(1, d)
        if layer < n_a:
            xf, (q, k, v, o, grow, gcol) = mlstm_in(xf, moe_out, gf_prev, nm, sh_m, sc_m, a_w_in[layer],
                                                     a_b_gates[layer], bsz, seq)
            hs = mlstm_cell(q, k, v, grow, gcol, bsz, seq)
            mix_args = (hs, o, a_mh_norm[layer].reshape(1, d))
            w_o = a_w_out[layer]
        else:
            j = layer - n_a
            xf, q = fox_q(xf, moe_out, gf_prev, nm, sh_m, sc_m, b_w_q[j], bsz, seq)
            if layer == n_a:
                kv_sh, kv_sc = vecs(kv_mod, 2)
                kv = shared_kv(xf, kv_norm.reshape(1, d), kv_sh, kv_sc, w_kv, b_fgate, bsz, seq)
            ke, ko, vt, eq = kv
            att = fox_attn(q, eq, ke, ko, vt, bsz, seq)
            mix_args = (att,)
            w_o = b_w_o[j]
        xf, h2, cls, rank, counts = post_mix(
            mix_args, layer < n_a, w_o, xf, g_m, norm_ffn[layer].reshape(1, d), sh_f, sc_f, w_router, router_bias,
            bsz, seq)
        moe_out = moe(h2, cls, rank, counts, w_router.T, wg_b[layer], wu_b[layer], wd_b[layer])
        gf_prev = g_f
    fin_sh, fin_sc = vecs(fin, 2)
    out = final_norm(xf, moe_out, gf_prev, norm_final.reshape(1, d), fin_sh, fin_sc, bsz, seq)
    return out.reshape(bsz, seq, d)
```

```python
import functools

import jax
import jax.numpy as jnp
from jax import lax
from jax.experimental import pallas as pl
from jax.experimental.pallas import tpu as pltpu

F32 = jnp.float32
BF16 = jnp.bfloat16
EPS = 1e-6
GATE_SOFTCAP = 15.0

A_HEADS = 8
B_HEADS = 16
N_EXPERTS = 16
N_GROUPS = 4
EPG = N_EXPERTS // N_GROUPS
N_PAIRS = 6
N_CLASSES = N_GROUPS * N_PAIRS
CLS_PAD = 32

VMEM_LIMIT = 56 * 1024 * 1024

TM = 512
A_L = 256
TQ = 1024
ATT_CHUNK = 256
MB = 256
RB = 2048
DMA_GROUP = 8
DMA_QUEUES = 2
PIECE = 256
LANES = 128
LOG2E = 1.4426950408889634

HIGHEST = lax.Precision.HIGHEST


def _cp(sem):
    return pltpu.CompilerParams(dimension_semantics=sem, vmem_limit_bytes=VMEM_LIMIT)


def _dot(a, b, precision=None):
    return jnp.dot(a, b, preferred_element_type=F32, precision=precision)


def _dot_nt(a, b, precision=None):
    return lax.dot_general(a, b, (((1,), (1,)), ((), ())), preferred_element_type=F32, precision=precision)


def _dot_tn(a, b, precision=None):
    return lax.dot_general(a, b, (((0,), (0,)), ((), ())), preferred_element_type=F32, precision=precision)


def _norm_mod(xin, nw, shift, scale):
    ms = jnp.mean(xin * xin, axis=-1, keepdims=True)
    y = xin * lax.rsqrt(ms + EPS)
    return (y * nw) * (1.0 + scale) + shift


def _log_sigmoid(x):
    return jnp.minimum(x, 0.0) - jnp.log1p(jnp.exp(-jnp.abs(x)))


def _sigmoid(x):
    return 1.0 / (1.0 + jnp.exp(-x))


def _ada_kernel(c_ref, w_ref, b_ref, o_ref):
    c = c_ref[...]
    ca = c * _sigmoid(c)
    o_ref[...] = _dot(ca, w_ref[...], HIGHEST) + b_ref[...]


def ada(c, w, b):
    nl, d, n = w.shape
    bsz = c.shape[0]
    tn = 512
    return pl.pallas_call(
        _ada_kernel,
        grid=(nl, n // tn),
        in_specs=[
            pl.BlockSpec((bsz, d), lambda l, j: (0, 0)),
            pl.BlockSpec((None, d, tn), lambda l, j: (l, 0, j)),
            pl.BlockSpec((None, 1, tn), lambda l, j: (l, 0, j)),
        ],
        out_specs=pl.BlockSpec((None, bsz, tn), lambda l, j: (l, 0, j)),
        out_shape=jax.ShapeDtypeStruct((nl, bsz, n), F32),
        compiler_params=_cp(("arbitrary", "arbitrary")),
        name="ada",
    )(c, w, b.reshape(nl, 1, n))


def _prologue(has_moe, refs):
    if not has_moe:
        x_ref, nw_ref, sh_ref, sc_ref = refs[:4]
        xin = x_ref[...]
        h = _norm_mod(xin, nw_ref[...], sh_ref[...], sc_ref[...])
        return xin, h, refs[4:], (lambda g, n: None), (lambda: None)
    dest_ref, x_ref, ys_ref, gf_ref, nw_ref, sh_ref, sc_ref = refs[:7]
    gbuf, gsem = refs[-2:]
    tm = x_ref.shape[0]
    lin = pl.program_id(0) * pl.num_programs(1) + pl.program_id(1)
    last = pl.num_programs(0) * pl.num_programs(1) - 1
    slot = lin % 2

    def start_rows(tile, slot_, lo, hi):
        base = tile * tm
        for r in range(lo, hi):
            pltpu.make_async_copy(ys_ref.at[dest_ref[base + r]], gbuf.at[slot_, r], gsem.at[slot_]).start(
                priority=r % DMA_QUEUES)

    def wait_rows(slot_):
        def body(r, carry):
            pltpu.make_async_copy(ys_ref.at[0], gbuf.at[slot_, 0], gsem.at[slot_]).wait()
            return carry
        lax.fori_loop(0, tm, body, 0, unroll=8)

    @pl.when(lin == 0)
    def _():
        start_rows(0, 0, 0, tm)

    wait_rows(slot)
    xin = x_ref[...] + gf_ref[...] * gbuf[slot].reshape(x_ref.shape)
    h = _norm_mod(xin, nw_ref[...], sh_ref[...], sc_ref[...])
    nxt = jnp.minimum(lin + 1, last)

    def issue(g, n):
        start_rows(nxt, 1 - slot, g * tm // n, (g + 1) * tm // n)

    def finish():
        @pl.when(lin == last)
        def _():
            wait_rows(1 - slot)

    return xin, h, refs[7:-2], issue, finish


def _prologue_specs(has_moe, ns, d, tm):
    row = pl.BlockSpec((tm, d), lambda b, s, *_: (b * ns + s, 0))
    vec = pl.BlockSpec((None, 1, d), lambda b, s, *_: (b, 0, 0))
    one = pl.BlockSpec((1, d), lambda b, s, *_: (0, 0))
    if has_moe:
        return [row, pl.BlockSpec(memory_space=pl.ANY), vec, one, vec, vec]
    return [row, one, vec, vec]


def _prologue_call(kernel_fn, has_moe, dest, args, in_specs, out_specs, out_shape, bsz, ns, tm, d, name):
    if not has_moe:
        return pl.pallas_call(
            kernel_fn, grid=(bsz, ns), in_specs=in_specs, out_specs=out_specs, out_shape=out_shape,
            compiler_params=_cp(("arbitrary", "arbitrary")), name=name)(*args)
    return pl.pallas_call(
        kernel_fn,
        grid_spec=pltpu.PrefetchScalarGridSpec(
            num_scalar_prefetch=1, grid=(bsz, ns), in_specs=in_specs, out_specs=out_specs,
            scratch_shapes=[pltpu.VMEM((2, tm, d // LANES, LANES), F32), pltpu.SemaphoreType.DMA((2,))]),
        out_shape=out_shape, compiler_params=_cp(("arbitrary", "arbitrary")), name=name)(dest, *args)


def _full(shape):
    nd = len(shape)
    return pl.BlockSpec(shape, lambda b, s, *_: (0,) * nd)


def _mlstm_in_kernel(has_moe, dqk, *refs):
    xin, h, rest, issue, finish = _prologue(has_moe, refs)
    wqt_ref, wk_ref, wvt_ref, wo_ref, wgt_ref, wg_ref, bcol_ref, brow_ref = rest[:8]
    outs = rest[8:]
    if has_moe:
        xnew_ref = outs[0]
        outs = outs[1:]
        xnew_ref[...] = xin
    qt_ref, k_ref, vt_ref, o_ref, grow_ref, gcol_ref = outs
    hb = h.astype(BF16)
    scale = float(dqk // A_HEADS) ** -0.5
    pieces = []
    for kind, w_ref, out_ref, sc in (("t", wqt_ref, qt_ref, scale), ("n", wk_ref, k_ref, None),
                                     ("t", wvt_ref, vt_ref, None), ("n", wo_ref, o_ref, None)):
        nf = w_ref.shape[0] if kind == "t" else w_ref.shape[1]
        for f0 in range(0, nf, PIECE):
            pieces.append((kind, w_ref, out_ref, slice(f0, f0 + PIECE), sc))
    for g, (kind, w_ref, out_ref, fs, sc) in enumerate(pieces):
        issue(g, len(pieces))
        if kind == "t":
            res = _dot_nt(w_ref[fs, :], hb)
            out_ref[fs, :] = (res if sc is None else res * sc).astype(BF16)
        else:
            out_ref[:, fs] = _dot(hb, w_ref[:, fs]).astype(BF16)
    gr = _dot_nt(wgt_ref[...], hb) + bcol_ref[...]
    gr = GATE_SOFTCAP * jnp.tanh(gr / GATE_SOFTCAP)
    ridx = lax.broadcasted_iota(jnp.int32, gr.shape, 0)
    grow_ref[...] = jnp.where(ridx < A_HEADS, gr, _log_sigmoid(gr))
    gc = _dot(hb, wg_ref[...]) + brow_ref[...]
    gc = GATE_SOFTCAP * jnp.tanh(gc / GATE_SOFTCAP)
    cidx = lax.broadcasted_iota(jnp.int32, gc.shape, 1)
    gcol_ref[...] = jnp.where(cidx < A_HEADS, gc, _log_sigmoid(gc))
    finish()


def mlstm_in(x, moe, gf, nw, shift, scale, w_in, b_gates, bsz, seq):
    t, d = x.shape
    dqk, dv = d // 2, d
    has_moe = moe is not None
    tm = min(TM, seq)
    ns = seq // tm
    w_b = w_in.astype(BF16)
    w_qt = w_b[:, 0:dqk].T
    w_k = w_b[:, dqk:2 * dqk]
    w_vt = w_b[:, 2 * dqk:2 * dqk + dv].T
    w_o = w_b[:, 2 * dqk + dv:2 * dqk + 2 * dv]
    w_g = w_b[:, 2 * dqk + 2 * dv:]
    ng = 2 * A_HEADS
    ys, dest = moe if has_moe else (None, None)
    args = [x] + ([ys, gf] if has_moe else []) + [nw, shift, scale, w_qt, w_k, w_vt, w_o, w_g.T, w_g,
                                                   b_gates.reshape(ng, 1), b_gates.reshape(1, ng)]
    in_specs = _prologue_specs(has_moe, ns, d, tm) + [
        _full((dqk, d)), _full((d, dqk)), _full((dv, d)), _full((d, dv)), _full((ng, d)), _full((d, ng)),
        _full((ng, 1)), _full((1, ng))]
    row = lambda n: pl.BlockSpec((tm, n), lambda b, s, *_: (b * ns + s, 0))
    col = lambda n: pl.BlockSpec((n, tm), lambda b, s, *_: (0, b * ns + s))
    out_specs = [col(dqk), row(dqk), col(dv), row(dv),
                 pl.BlockSpec((None, ng, tm), lambda b, s, *_: (b, 0, s)),
                 pl.BlockSpec((None, tm, ng), lambda b, s, *_: (b, s, 0))]
    out_shape = [jax.ShapeDtypeStruct((dqk, t), BF16), jax.ShapeDtypeStruct((t, dqk), BF16),
                 jax.ShapeDtypeStruct((dv, t), BF16), jax.ShapeDtypeStruct((t, dv), BF16),
                 jax.ShapeDtypeStruct((bsz, ng, seq), F32), jax.ShapeDtypeStruct((bsz, seq, ng), F32)]
    if has_moe:
        out_specs = [row(d)] + out_specs
        out_shape = [jax.ShapeDtypeStruct((t, d), F32)] + out_shape
    res = _prologue_call(functools.partial(_mlstm_in_kernel, has_moe, dqk), has_moe, dest, args, in_specs, out_specs,
                         out_shape, bsz, ns, tm, d, "mlstm_in")
    if has_moe:
        return res[0], res[1:]
    return x, res


N_ROWS = 16


def _mlstm_cell_kernel(qt_ref, k_ref, vt_ref, grow_ref, gcol_ref, hs_ref, ct_ref, m_ref):
    c = pl.program_id(1)
    L = k_ref.shape[0]
    dk = k_ref.shape[1] // A_HEADS
    dvh = vt_ref.shape[0] // A_HEADS

    @pl.when(c == 0)
    def _():
        ct_ref[...] = jnp.zeros_like(ct_ref)
        m_ref[...] = jnp.zeros_like(m_ref)

    r = lax.broadcasted_iota(jnp.int32, (L, L), 0)
    cc = lax.broadcasted_iota(jnp.int32, (L, L), 1)
    visible = r <= cc
    upper = visible.astype(F32)
    lower = (cc <= r).astype(F32)
    grow = grow_ref[...]
    gcol = gcol_ref[...]
    b_row_all = _dot(grow[A_HEADS:, :], upper, HIGHEST)
    b_col_all = _dot(lower, gcol[:, A_HEADS:], HIGHEST)
    x_col_all = gcol[:, :A_HEADS] - b_col_all
    ones_rows = jnp.ones((N_ROWS, L), BF16)

    heads = range(A_HEADS)
    qts = [qt_ref[h * dk:(h + 1) * dk, :] for h in heads]
    khs = [k_ref[:, h * dk:(h + 1) * dk] for h in heads]
    brs = [b_row_all[h:h + 1, :] for h in heads]
    m_prevs = [m_ref[h:h + 1, 0:1] for h in heads]
    kqs = [_dot(khs[h], qts[h]) for h in heads]
    ws, scs = [], []
    for h in heads:
        d_intra = jnp.where(visible, brs[h] + x_col_all[:, h:h + 1], -jnp.inf)
        d_inter = brs[h] + m_prevs[h]
        m_t = jnp.maximum(d_inter, jnp.max(d_intra, axis=0, keepdims=True))
        ws.append((jnp.exp(d_intra - m_t) * kqs[h]).astype(BF16))
        scs.append((jnp.exp(d_inter - m_t), jnp.exp(-m_t)))
    v_augs = [jnp.concatenate([vt_ref[h * dvh:(h + 1) * dvh, :], ones_rows], axis=0) for h in heads]
    cts = [ct_ref[h] for h in heads]
    tots = [_dot(v_augs[h], ws[h]) + scs[h][0] * _dot(cts[h].astype(BF16), qts[h]) for h in heads]
    for h in heads:
        num = tots[h][:dvh, :]
        den = tots[h][dvh:dvh + 1, :]
        hh = num / jnp.maximum(jnp.abs(den), scs[h][1])
        hh = hh * lax.rsqrt(jnp.mean(hh * hh, axis=0, keepdims=True) + EPS)
        hs_ref[:, h * dvh:(h + 1) * dvh] = hh.T.astype(hs_ref.dtype)
    for h in heads:
        b_last = brs[h][:, L - 1:L]
        g = b_last - brs[h] + grow[h:h + 1, :]
        m_new = jnp.maximum(b_last + m_prevs[h], jnp.max(g, axis=1, keepdims=True))
        wk = jnp.exp(g - m_new)
        decay = jnp.exp(b_last + m_prevs[h] - m_new)
        ct_ref[h] = decay * cts[h] + _dot((v_augs[h].astype(F32) * wk).astype(BF16), khs[h])
        m_ref[h:h + 1, :] = jnp.broadcast_to(m_new, (1, m_ref.shape[1]))


def mlstm_cell(qt, k, vt, grow, gcol, bsz, seq):
    t, dqk = k.shape
    dv = vt.shape[0]
    L = min(A_L, seq)
    nc = seq // L
    ng = grow.shape[1]
    dk = dqk // A_HEADS
    dvh = dv // A_HEADS
    return pl.pallas_call(
        _mlstm_cell_kernel,
        grid=(bsz, nc),
        in_specs=[pl.BlockSpec((dqk, L), lambda b, c: (0, b * nc + c)),
                  pl.BlockSpec((L, dqk), lambda b, c: (b * nc + c, 0)),
                  pl.BlockSpec((dv, L), lambda b, c: (0, b * nc + c)),
                  pl.BlockSpec((None, ng, L), lambda b, c: (b, 0, c)),
                  pl.BlockSpec((None, L, ng), lambda b, c: (b, c, 0))],
        out_specs=pl.BlockSpec((L, dv), lambda b, c: (b * nc + c, 0)),
        out_shape=jax.ShapeDtypeStruct((t, dv), BF16),
        scratch_shapes=[pltpu.VMEM((A_HEADS, dvh + N_ROWS, dk), F32), pltpu.VMEM((A_HEADS, 128), F32)],
        compiler_params=_cp(("arbitrary", "arbitrary")), name="mlstm_cell",
    )(qt, k, vt, grow, gcol)


def _fox_q_kernel(has_moe, qscale, *refs):
    xin, h, rest, issue, finish = _prologue(has_moe, refs)
    w_ref = rest[0]
    outs = rest[1:]
    if has_moe:
        outs[0][...] = xin
        outs = outs[1:]
    hb = h.astype(BF16)
    d = w_ref.shape[1]
    n = d // PIECE
    for g in range(n):
        issue(g, n)
        cols = slice(g * PIECE, (g + 1) * PIECE)
        outs[0][:, cols] = (_dot(hb, w_ref[:, cols]) * qscale).astype(BF16)
    finish()


def fox_q(x, moe, gf, nw, shift, scale, w_q, bsz, seq):
    t, d = x.shape
    has_moe = moe is not None
    tm = min(TM, seq)
    ns = seq // tm
    ys, dest = moe if has_moe else (None, None)
    args = [x] + ([ys, gf] if has_moe else []) + [nw, shift, scale, w_q.astype(BF16)]
    in_specs = _prologue_specs(has_moe, ns, d, tm) + [_full((d, d))]
    row = pl.BlockSpec((tm, d), lambda b, s, *_: (b * ns + s, 0))
    out_specs = [row]
    out_shape = [jax.ShapeDtypeStruct((t, d), BF16)]
    if has_moe:
        out_specs = [row] + out_specs
        out_shape = [jax.ShapeDtypeStruct((t, d), F32)] + out_shape
    qscale = float(d // B_HEADS) ** -0.5 * LOG2E
    res = _prologue_call(functools.partial(_fox_q_kernel, has_moe, qscale), has_moe, dest, args, in_specs, out_specs,
                         out_shape, bsz, ns, tm, d, "fox_q")
    if has_moe:
        return res[0], res[1]
    return x, res[0]


N_BIAS_PIECES = 3


def _bias_select_matrices(d, n_heads):
    import numpy as np
    dh = d // n_heads
    pq = np.zeros((N_BIAS_PIECES, n_heads, d), np.float32)
    pk = np.zeros((N_BIAS_PIECES, n_heads, d), np.float32)
    oq = np.zeros((1, d), np.float32)
    ok = np.zeros((1, d), np.float32)
    for h in range(n_heads):
        base = (h // 2) * 2 * dh + (dh if h % 2 == 0 else 0)
        for p in range(N_BIAS_PIECES):
            pq[p, h, base + p] = 1.0
            ok[0, base + p] = 1.0
            oq[0, base + N_BIAS_PIECES + p] = 1.0
            pk[p, h, base + N_BIAS_PIECES + p] = -1.0
    return pq, pk, oq, ok


def _split3(f):
    a = f.astype(BF16)
    r1 = f - a.astype(F32)
    b = r1.astype(BF16)
    c = (r1 - b.astype(F32)).astype(BF16)
    return a, b, c


def _shared_kv_kernel(x_ref, nw_ref, sh_ref, sc_ref, w_ref, wvt_ref, wf_ref, brow_ref, lo_ref, pq_ref, pk_ref, oq_ref,
                      ok_ref, ke_ref, ko_ref, vt_ref, eq_ref, cc_ref):
    s = pl.program_id(1)
    d = x_ref.shape[1]
    dh = d // B_HEADS

    @pl.when(s == 0)
    def _():
        cc_ref[...] = jnp.zeros_like(cc_ref)

    h = _norm_mod(x_ref[...], nw_ref[...], sh_ref[...], sc_ref[...])
    hb = h.astype(BF16)
    lc = _log_sigmoid(_dot(hb, wf_ref[...]) + brow_ref[...])
    fcum = _dot(lo_ref[...], lc, HIGHEST) + cc_ref[...]
    cc_ref[...] += jnp.sum(lc, axis=0, keepdims=True)
    pieces = jnp.concatenate(_split3(fcum * LOG2E), axis=1)
    eq = oq_ref[...] + _dot(pieces, pq_ref[...])
    ek = ok_ref[...] + _dot(pieces, pk_ref[...])
    eq_ref[...] = eq.astype(BF16)
    even = (lax.broadcasted_iota(jnp.int32, (1, d), 1) % (2 * dh)) < dh
    k = _dot(hb, w_ref[...])
    ke_ref[...] = jnp.where(even, k, ek).astype(BF16)
    ko_ref[...] = jnp.where(even, ek, k).astype(BF16)
    vt_ref[...] = _dot_nt(wvt_ref[...], hb).astype(BF16)


def shared_kv(x, nw, shift, scale, w_kv, b_fgate, bsz, seq):
    t, d = x.shape
    tm = min(TM, seq)
    ns = seq // tm
    nh = B_HEADS
    w_k = w_kv[:, :d].astype(BF16)
    w_vt = w_kv[:, d:2 * d].astype(BF16).T
    w_f = w_kv[:, 2 * d:].astype(BF16)
    idx = jnp.arange(tm)
    lower = (idx[None, :] <= idx[:, None]).astype(F32)
    pq, pk, oq, ok = _bias_select_matrices(d, nh)
    row = pl.BlockSpec((tm, d), lambda b, s: (b * ns + s, 0))
    vec = pl.BlockSpec((None, 1, d), lambda b, s: (b, 0, 0))
    sel = _full((N_BIAS_PIECES * nh, d))
    return pl.pallas_call(
        _shared_kv_kernel,
        grid=(bsz, ns),
        in_specs=[row, _full((1, d)), vec, vec, _full((d, d)), _full((d, d)), _full((d, nh)), _full((1, nh)),
                  _full((tm, tm)), sel, sel, _full((1, d)), _full((1, d))],
        out_specs=[row, row, pl.BlockSpec((d, tm), lambda b, s: (0, b * ns + s)), row],
        out_shape=[jax.ShapeDtypeStruct((t, d), BF16), jax.ShapeDtypeStruct((t, d), BF16),
                   jax.ShapeDtypeStruct((d, t), BF16), jax.ShapeDtypeStruct((t, d), BF16)],
        scratch_shapes=[pltpu.VMEM((1, nh), F32)],
        compiler_params=_cp(("arbitrary", "arbitrary")), name="shared_kv",
    )(x, nw, shift, scale, w_k, w_vt, w_f, b_fgate.reshape(1, nh), lower,
      jnp.asarray(pq.reshape(-1, d), BF16), jnp.asarray(pk.reshape(-1, d), BF16), jnp.asarray(oq), jnp.asarray(ok))


def _fox_attn_kernel(q_ref, eq_ref, ke_ref, ko_ref, vt_ref, o_ref, st_ref):
    i = pl.program_id(2)
    tq = q_ref.shape[0]
    tk = tq
    ck = min(ATT_CHUNK, tk)
    nchunk = tk // ck
    dh2 = q_ref.shape[1]
    dh = dh2 // 2
    lo_mask = lax.broadcasted_iota(jnp.int32, (1, dh2), 1) < dh
    q = q_ref[...]
    eq = eq_ref[...]
    qs = (jnp.where(lo_mask, q, eq), jnp.where(lo_mask, eq, q))
    k_refs = (ke_ref, ko_ref)
    ones_rows = jnp.ones((N_ROWS, ck), BF16)

    def tail(x, q0, new):
        return new if q0 == 0 else jnp.concatenate([x[:, :q0], new], axis=1)

    def qk_chunk(e, kt, c, mx, diag):
        off = pl.multiple_of(kt * tk + c * ck, ck)
        q0 = c * ck if diag else 0
        st = _dot_nt(k_refs[e][pl.ds(off, ck), :], qs[e][q0:, :])
        if diag:
            visible = (lax.broadcasted_iota(jnp.int32, st.shape, 0) <= lax.broadcasted_iota(jnp.int32, st.shape, 1))
            st = jnp.where(visible, st, -jnp.inf)
        st_ref[e, c * ck:(c + 1) * ck, q0:] = st
        cm = jnp.max(st, axis=0, keepdims=True)
        return cm if mx is None else tail(mx, q0, jnp.maximum(mx[:, q0:], cm))

    def pv_chunk(e, kt, c, state, diag):
        m, acc = state
        off = pl.multiple_of(kt * tk + c * ck, ck)
        q0 = c * ck if diag else 0
        pt = jnp.exp2(st_ref[e, c * ck:(c + 1) * ck, q0:] - m[:, q0:])
        v_aug = jnp.concatenate([vt_ref[e * dh:(e + 1) * dh, pl.ds(off, ck)], ones_rows], axis=0)
        return m, tail(acc, q0, acc[:, q0:] + _dot(v_aug, pt.astype(BF16)))

    def phase(cur, state, tile_max, nxt):
        m, acc = state
        m_new = jnp.maximum(m, tile_max)
        state = (m_new, jnp.exp2(m - m_new) * acc)
        mx = None
        for c in range(nchunk):
            mx = qk_chunk(nxt[0], nxt[1], c, mx, nxt[2])
            state = pv_chunk(cur[0], cur[1], c, state, cur[2])
        return state, mx

    init = (jnp.full((1, tq), -1e30, F32), jnp.zeros((dh + N_ROWS, tq), F32))
    mx0 = None
    for c in range(nchunk):
        mx0 = qk_chunk(0, i, c, mx0, True)
    s0, mx1 = phase((0, i, True), init, mx0, (1, i, True))
    s1, mx0 = phase((1, i, True), init, mx1, (0, 0, False))

    def trip(j, carry):
        s0, s1, mx0 = carry
        s0, mx1 = phase((0, j, False), s0, mx0, (1, j, False))
        s1, mx0 = phase((1, j, False), s1, mx1, (0, j + 1, False))
        return s0, s1, mx0

    s0, s1, _ = lax.fori_loop(0, i, trip, (s0, s1, mx0))
    ot = jnp.concatenate([acc[:dh] / acc[dh:dh + 1] for _, acc in (s0, s1)], axis=0)
    o_ref[...] = ot.T.astype(o_ref.dtype)


def fox_attn(q, eq, ke, ko, vt, bsz, seq):
    t, d = q.shape
    nh = B_HEADS
    dh2 = 2 * (d // nh)
    tq = min(TQ, seq)
    nq = seq // tq
    qspec = pl.BlockSpec((tq, dh2), lambda b, hp, i: (b * nq + i, hp))
    kspec = pl.BlockSpec((seq, dh2), lambda b, hp, i: (b, hp))
    vspec = pl.BlockSpec((dh2, seq), lambda b, hp, i: (hp, b))
    return pl.pallas_call(
        _fox_attn_kernel,
        grid=(bsz, nh // 2, nq),
        in_specs=[qspec, qspec, kspec, kspec, vspec],
        out_specs=qspec,
        out_shape=jax.ShapeDtypeStruct((t, d), BF16),
        scratch_shapes=[pltpu.VMEM((2, tq, tq), F32)],
        compiler_params=_cp(("arbitrary", "arbitrary", "arbitrary")), name="fox_attn",
    )(q, eq, ke, ko, vt)


def _route(logits, bias_col):
    aff = _sigmoid(logits)
    sel = aff + bias_col
    s = [sel[e:e + 1, :] for e in range(N_EXPERTS)]
    gs = []
    for g in range(N_GROUPS):
        v = s[EPG * g:EPG * (g + 1)]
        best = v[0] + v[1]
        for i in range(EPG):
            for j in range(i + 1, EPG):
                if (i, j) != (0, 1):
                    best = jnp.maximum(best, v[i] + v[j])
        gs.append(best)
    grp = jnp.zeros_like(gs[0], dtype=jnp.int32)
    best = gs[0]
    for g in range(1, N_GROUPS):
        upd = gs[g] > best
        grp = jnp.where(upd, g, grp)
        best = jnp.where(upd, gs[g], best)

    def pick(arrs, j):
        out = arrs[(N_GROUPS - 1) * EPG + j]
        for g in range(N_GROUPS - 2, -1, -1):
            out = jnp.where(grp == g, arrs[g * EPG + j], out)
        return out

    v = [pick(s, j) for j in range(EPG)]
    i1 = jnp.zeros_like(grp)
    b1 = v[0]
    for j in range(1, EPG):
        upd = v[j] > b1
        i1 = jnp.where(upd, j, i1)
        b1 = jnp.where(upd, v[j], b1)
    i2 = jnp.full_like(grp, -1)
    b2 = jnp.full_like(b1, -jnp.inf)
    for j in range(EPG):
        upd = (i1 != j) & ((i2 < 0) | (v[j] > b2))
        i2 = jnp.where(upd, j, i2)
        b2 = jnp.where(upd, v[j], b2)
    lo = jnp.minimum(i1, i2)
    hi = jnp.maximum(i1, i2)
    base = jnp.where(lo == 0, 0, jnp.where(lo == 1, 3, 5))
    return grp * N_PAIRS + base + (hi - lo - 1)


def _post_mix_kernel(is_mlstm, *refs):
    if is_mlstm:
        hs_ref, og_ref, mh_ref = refs[:3]
        refs = refs[3:]
        og = og_ref[...].astype(F32)
        mix = hs_ref[...].astype(F32) * mh_ref[...] * _sigmoid(og)
    else:
        mix = refs[0][...]
        refs = refs[1:]
    (w_ref, x_ref, gm_ref, nw_ref, sh_ref, sc_ref, wrt_ref, rb_ref, us_ref,
     xnew_ref, h2_ref, cls_ref, rank_ref, cnt_ref) = refs
    first = (pl.program_id(0) == 0) & (pl.program_id(1) == 0)

    @pl.when(first)
    def _():
        cnt_ref[...] = jnp.zeros_like(cnt_ref)

    y = _dot(mix.astype(BF16), w_ref[...])
    xnew = x_ref[...] + gm_ref[...] * y
    xnew_ref[...] = xnew
    h2 = _norm_mod(xnew, nw_ref[...], sh_ref[...], sc_ref[...])
    h2_ref[...] = h2.reshape(h2_ref.shape)
    h_hi = h2.astype(BF16)
    h_lo = (h2 - h_hi.astype(F32)).astype(BF16)
    ne = wrt_ref.shape[0] // 2
    part = _dot_nt(wrt_ref[...], h_hi)
    logits = part[:ne] + part[ne:] + _dot_nt(wrt_ref[:ne, :], h_lo)
    cls = _route(logits, rb_ref[...])
    cls_ref[...] = cls
    tm = cls.shape[1]
    onehot = (lax.broadcasted_iota(jnp.int32, (CLS_PAD, tm), 0) == cls).astype(F32)
    prefix = _dot(onehot.astype(BF16), us_ref[...])
    carry = cnt_ref[:, 0:1]
    rank = jnp.sum(onehot * (prefix + carry), axis=0, keepdims=True)
    rank_ref[...] = rank.astype(jnp.int32)
    cnt_ref[...] += jnp.sum(onehot, axis=1, keepdims=True)


def post_mix(mix_args, is_mlstm, w_o, x, gm, nw, shift, scale, w_router, router_bias, bsz, seq):
    t, d = x.shape
    tm = min(TM, seq)
    ns = seq // tm
    ne = N_EXPERTS
    idx = jnp.arange(tm)
    upper_strict = (idx[:, None] < idx[None, :]).astype(BF16)
    row = pl.BlockSpec((tm, d), lambda b, s: (b * ns + s, 0))
    vec = pl.BlockSpec((None, 1, d), lambda b, s: (b, 0, 0))
    lane_row = pl.BlockSpec((None, 1, tm), lambda b, s: (b, 0, s))
    if is_mlstm:
        mix_specs = [row, row, _full((1, d))]
    else:
        mix_specs = [row]
    in_specs = mix_specs + [_full((d, d)), row, vec, _full((1, d)), vec, vec, _full((2 * ne, d)), _full((ne, 1)),
                            _full((tm, tm))]
    wr = w_router.T
    wr_hi = wr.astype(BF16)
    wr_split = jnp.concatenate([wr_hi, (wr - wr_hi.astype(F32)).astype(BF16)], axis=0)
    row3 = pl.BlockSpec((tm, d // LANES, LANES), lambda b, s: (b * ns + s, 0, 0))
    out_specs = [row, row3, lane_row, lane_row, _full((CLS_PAD, 128))]
    out_shape = [jax.ShapeDtypeStruct((t, d), F32), jax.ShapeDtypeStruct((t, d // LANES, LANES), F32),
                 jax.ShapeDtypeStruct((bsz, 1, seq), jnp.int32), jax.ShapeDtypeStruct((bsz, 1, seq), jnp.int32),
                 jax.ShapeDtypeStruct((CLS_PAD, 128), F32)]
    return pl.pallas_call(
        functools.partial(_post_mix_kernel, is_mlstm),
        grid=(bsz, ns), in_specs=in_specs, out_specs=out_specs, out_shape=out_shape,
        compiler_params=_cp(("arbitrary", "arbitrary")), name="post_mix",
    )(*mix_args, w_o.astype(BF16), x, gm, nw, shift, scale, wr_split, router_bias.reshape(ne, 1), upper_strict)


_FILL_PIECES = tuple(1 << k for k in range(MB.bit_length() - 2, -1, -1))


def _row_scatter_kernel(rb, dest_ref, fill_start_ref, fill_n_ref, src_ref, dst_ref, zero_ref, sem, zsem):
    base = pl.program_id(0) * rb

    def fill(do):
        for c in range(N_CLASSES):
            n = fill_n_ref[c]
            for p in _FILL_PIECES:
                row = fill_start_ref[c] + (n & ~(2 * p - 1))

                @pl.when((n & p) != 0)
                def _():
                    do(pltpu.make_async_copy(zero_ref.at[pl.ds(0, p)], dst_ref.at[pl.ds(row, p)], zsem))
        for k in range(N_CLASSES):
            row = pl.multiple_of(fill_start_ref[N_CLASSES] + k * MB, MB)

            @pl.when(k < fill_n_ref[N_CLASSES])
            def _():
                do(pltpu.make_async_copy(zero_ref, dst_ref.at[pl.ds(row, MB)], zsem))

    @pl.when(pl.program_id(0) == 0)
    def _():
        zero_ref[...] = jnp.zeros_like(zero_ref)
        fill(lambda cp: cp.start())

    def start(g, carry):
        for u in range(DMA_GROUP):
            rr = g * DMA_GROUP + u
            pltpu.make_async_copy(src_ref.at[rr], dst_ref.at[dest_ref[base + rr]], sem).start(priority=u % DMA_QUEUES)
        return carry

    def wait(rr, carry):
        pltpu.make_async_copy(src_ref.at[0], dst_ref.at[0], sem).wait()
        return carry

    lax.fori_loop(0, rb // DMA_GROUP, start, 0)
    lax.fori_loop(0, rb, wait, 0, unroll=8)

    @pl.when(pl.program_id(0) == 0)
    def _():
        fill(lambda cp: cp.wait())


def row_scatter(dest, fill_start, fill_n, src, n_rows):
    t = src.shape[0]
    rb = min(RB, t)
    return pl.pallas_call(
        functools.partial(_row_scatter_kernel, rb),
        grid_spec=pltpu.PrefetchScalarGridSpec(
            num_scalar_prefetch=3, grid=(t // rb,),
            in_specs=[pl.BlockSpec((rb,) + src.shape[1:], lambda i, *_: (i, 0, 0))],
            out_specs=pl.BlockSpec(memory_space=pl.ANY),
            scratch_shapes=[pltpu.VMEM((MB,) + src.shape[1:], src.dtype),
                            pltpu.SemaphoreType.DMA, pltpu.SemaphoreType.DMA]),
        out_shape=jax.ShapeDtypeStruct((n_rows,) + src.shape[1:], src.dtype),
        compiler_params=_cp(("arbitrary",)), name="row_scatter",
    )(dest, fill_start, fill_n, src)


def _experts_kernel(elo_ref, ehi_ref, nused_ref, x_ref, wrt_ref,
                    wg_lo, wu_lo, wd_lo, wg_hi, wu_hi, wd_hi, y_ref):
    i = pl.program_id(0)

    @pl.when(i < nused_ref[0])
    def _():
        x = x_ref[...].reshape(x_ref.shape[0], -1)
        xb = x.astype(BF16)
        aff_lo = _sigmoid(jnp.sum(x * wrt_ref[pl.ds(elo_ref[i], 1), :], axis=1, keepdims=True))
        aff_hi = _sigmoid(jnp.sum(x * wrt_ref[pl.ds(ehi_ref[i], 1), :], axis=1, keepdims=True))
        tot = aff_lo + aff_hi
        acc = None
        for gate, wg, wu, wd in ((aff_lo / tot, wg_lo, wu_lo, wd_lo), (aff_hi / tot, wg_hi, wu_hi, wd_hi)):
            hg = _dot(xb, wg[...])
            hu = _dot(xb, wu[...])
            act = hg * _sigmoid(hg) * hu * gate
            part = _dot(act.astype(BF16), wd[...])
            acc = part if acc is None else acc + part
        y_ref[...] = acc.reshape(y_ref.shape)

    @pl.when(i >= nused_ref[0])
    def _():
        y_ref[...] = jnp.zeros_like(y_ref)


def experts(xs, w_router_t, blk_lo, blk_hi, nused, w_gate, w_up, w_down):
    p = xs.shape[0]
    d = xs.shape[1] * xs.shape[2]
    nblk = p // MB
    de = w_gate.shape[2]
    xrow = pl.BlockSpec((MB,) + xs.shape[1:], lambda i, lo, hi, nu: (i, 0, 0))
    xrow_in = pl.BlockSpec((MB,) + xs.shape[1:], lambda i, lo, hi, nu: (jnp.minimum(i, nu[0] - 1), 0, 0))
    wrt = pl.BlockSpec(w_router_t.shape, lambda i, lo, hi, nu: (0, 0))
    w_in_lo = pl.BlockSpec((None, d, de), lambda i, lo, hi, nu: (lo[i], 0, 0))
    w_in_hi = pl.BlockSpec((None, d, de), lambda i, lo, hi, nu: (hi[i], 0, 0))
    w_out_lo = pl.BlockSpec((None, de, d), lambda i, lo, hi, nu: (lo[i], 0, 0))
    w_out_hi = pl.BlockSpec((None, de, d), lambda i, lo, hi, nu: (hi[i], 0, 0))
    return pl.pallas_call(
        _experts_kernel,
        grid_spec=pltpu.PrefetchScalarGridSpec(
            num_scalar_prefetch=3, grid=(nblk,),
            in_specs=[xrow_in, wrt, w_in_lo, w_in_lo, w_out_lo, w_in_hi, w_in_hi, w_out_hi],
            out_specs=xrow),
        out_shape=jax.ShapeDtypeStruct(xs.shape, F32),
        compiler_params=_cp(("arbitrary",)), name="experts",
    )(blk_lo, blk_hi, nused, xs, w_router_t, w_gate, w_up, w_down, w_gate, w_up, w_down)


_PAIR_LO = (0, 0, 0, 1, 1, 2)
_PAIR_HI = (1, 2, 3, 2, 3, 3)


def moe(h2, cls, rank, counts, w_router_t, w_gate, w_up, w_down):
    t = h2.shape[0]
    p = t + N_CLASSES * MB
    nblk = p // MB
    cls = cls.reshape(t)
    counts = counts[:N_CLASSES, 0].astype(jnp.int32)
    padded = (counts + MB - 1) // MB * MB
    pad_end = jnp.cumsum(padded)
    pad_start = pad_end - padded
    dest = (pad_start[cls] + rank.reshape(t)).astype(jnp.int32)
    blk_row = jnp.arange(nblk, dtype=jnp.int32) * MB
    blk_cls = jnp.minimum(jnp.sum((pad_end[None, :] <= blk_row[:, None]).astype(jnp.int32), axis=1), N_CLASSES - 1)
    grp = blk_cls // N_PAIRS
    pr = blk_cls % N_PAIRS
    blk_lo = grp * EPG + jnp.asarray(_PAIR_LO, jnp.int32)[pr]
    blk_hi = grp * EPG + jnp.asarray(_PAIR_HI, jnp.int32)[pr]
    nused = (pad_end[-1:] // MB).astype(jnp.int32)
    fill_start = jnp.concatenate([pad_start + counts, pad_end[-1:]]).astype(jnp.int32)
    fill_n = jnp.concatenate([padded - counts, (p - pad_end[-1:]) // MB]).astype(jnp.int32)
    xs = row_scatter(dest, fill_start, fill_n, h2, p)
    ys = experts(xs, w_router_t, blk_lo, blk_hi, nused, w_gate, w_up, w_down)
    return ys, dest


def _final_kernel(*refs):
    _, h, rest, issue, finish = _prologue(True, refs)
    n = 8
    rows = h.shape[0] // n
    for g in range(n):
        issue(g, n)
        rest[0][g * rows:(g + 1) * rows, :] = h[g * rows:(g + 1) * rows, :]
    finish()


def final_norm(x, moe_out, gf, nw, shift, scale, bsz, seq):
    t, d = x.shape
    tm = min(TM, seq)
    ns = seq // tm
    ys, dest = moe_out
    row = pl.BlockSpec((tm, d), lambda b, s, *_: (b * ns + s, 0))
    return _prologue_call(_final_kernel, True, dest, [x, ys, gf, nw, shift, scale], _prologue_specs(True, ns, d, tm),
                          row, jax.ShapeDtypeStruct((t, d), F32), bsz, ns, tm, d, "final_norm")


def kernel(x, c, a_w_in, a_b_gates, a_mh_norm, a_w_out, kv_norm, w_ada_kv, b_ada_kv, w_kv, b_fgate, b_w_q, b_w_o,
           norm_mix, norm_ffn, w_ada, b_ada, w_router, router_bias, w_gate, w_up, w_down, norm_final, w_ada_final,
           b_ada_final):
    bsz, seq, d = x.shape
    depth = w_ada.shape[0]
    n_a = a_w_in.shape[0]
    t = bsz * seq
    xf = x.reshape(t, d)

    mods = ada(c, w_ada, b_ada)
    kv_mod = ada(c, w_ada_kv[None], b_ada_kv[None])[0]
    fin = ada(c, w_ada_final[None], b_ada_final[None])[0]

    def vecs(m, n):
        return [m[:, None, i * d:(i + 1) * d] for i in range(n)]

    wg_b = w_gate.astype(BF16)
    wu_b = w_up.astype(BF16)
    wd_b = w_down.astype(BF16)

    moe_out = None
    gf_prev = None
    kv = None
    for layer in range(depth):
        sh_m, sc_m, g_m, sh_f, sc_f, g_f = vecs(mods[layer], 6)
        nm = norm_mix[layer].reshape(1, d)
        if layer < n_a:
            xf, (q, k, v, o, grow, gcol) = mlstm_in(xf, moe_out, gf_prev, nm, sh_m, sc_m, a_w_in[layer],
                                                     a_b_gates[layer], bsz, seq)
            hs = mlstm_cell(q, k, v, grow, gcol, bsz, seq)
            mix_args = (hs, o, a_mh_norm[layer].reshape(1, d))
            w_o = a_w_out[layer]
        else:
            j = layer - n_a
            xf, q = fox_q(xf, moe_out, gf_prev, nm, sh_m, sc_m, b_w_q[j], bsz, seq)
            if layer == n_a:
                kv_sh, kv_sc = vecs(kv_mod, 2)
                kv = shared_kv(xf, kv_norm.reshape(1, d), kv_sh, kv_sc, w_kv, b_fgate, bsz, seq)
            ke, ko, vt, eq = kv
            att = fox_attn(q, eq, ke, ko, vt, bsz, seq)
            mix_args = (att,)
            w_o = b_w_o[j]
        xf, h2, cls, rank, counts = post_mix(
            mix_args, layer < n_a, w_o, xf, g_m, norm_ffn[layer].reshape(1, d), sh_f, sc_f, w_router, router_bias,
            bsz, seq)
        moe_out = moe(h2, cls, rank, counts, w_router.T, wg_b[layer], wu_b[layer], wd_b[layer])
        gf_prev = g_f
    fin_sh, fin_sc = vecs(fin, 2)
    out = final_norm(xf, moe_out, gf_prev, norm_final.reshape(1, d), fin_sh, fin_sc, bsz, seq)
    return out.reshape(bsz, seq, d)
```

```python
import functools

import jax
import jax.numpy as jnp
from jax import lax
from jax.experimental import pallas as pl
from jax.experimental.pallas import tpu as pltpu

F32 = jnp.float32
BF16 = jnp.bfloat16
EPS = 1e-6
GATE_SOFTCAP = 15.0

A_HEADS = 8
B_HEADS = 16
N_EXPERTS = 16
N_GROUPS = 4
EPG = N_EXPERTS // N_GROUPS
N_PAIRS = 6
N_CLASSES = N_GROUPS * N_PAIRS
CLS_PAD = 32

VMEM_LIMIT = 56 * 1024 * 1024

TM = 512
A_L = 256
TQ = 1024
ATT_CHUNK = 256
MB = 256
RB = 4096
DMA_GROUP = 8
DMA_QUEUES = 2
PIECE = 256
LANES = 128
LOG2E = 1.4426950408889634

HIGHEST = lax.Precision.HIGHEST


def _cp(sem):
    return pltpu.CompilerParams(dimension_semantics=sem, vmem_limit_bytes=VMEM_LIMIT)


def _dot(a, b, precision=None):
    return jnp.dot(a, b, preferred_element_type=F32, precision=precision)


def _dot_nt(a, b, precision=None):
    return lax.dot_general(a, b, (((1,), (1,)), ((), ())), preferred_element_type=F32, precision=precision)


def _dot_tn(a, b, precision=None):
    return lax.dot_general(a, b, (((0,), (0,)), ((), ())), preferred_element_type=F32, precision=precision)


def _norm_mod(xin, nw, shift, scale):
    ms = jnp.mean(xin * xin, axis=-1, keepdims=True)
    y = xin * lax.rsqrt(ms + EPS)
    return (y * nw) * (1.0 + scale) + shift


def _log_sigmoid(x):
    return jnp.minimum(x, 0.0) - jnp.log1p(jnp.exp(-jnp.abs(x)))


def _sigmoid(x):
    return 1.0 / (1.0 + jnp.exp(-x))


def _ada_kernel(c_ref, w_ref, b_ref, o_ref):
    c = c_ref[...]
    ca = c * _sigmoid(c)
    o_ref[...] = _dot(ca, w_ref[...], HIGHEST) + b_ref[...]


def ada(c, w, b):
    nl, d, n = w.shape
    bsz = c.shape[0]
    tn = 512
    return pl.pallas_call(
        _ada_kernel,
        grid=(nl, n // tn),
        in_specs=[
            pl.BlockSpec((bsz, d), lambda l, j: (0, 0)),
            pl.BlockSpec((None, d, tn), lambda l, j: (l, 0, j)),
            pl.BlockSpec((None, 1, tn), lambda l, j: (l, 0, j)),
        ],
        out_specs=pl.BlockSpec((None, bsz, tn), lambda l, j: (l, 0, j)),
        out_shape=jax.ShapeDtypeStruct((nl, bsz, n), F32),
        compiler_params=_cp(("arbitrary", "arbitrary")),
        name="ada",
    )(c, w, b.reshape(nl, 1, n))


def _prologue(has_moe, refs):
    if not has_moe:
        x_ref, nw_ref, sh_ref, sc_ref = refs[:4]
        xin = x_ref[...]
        h = _norm_mod(xin, nw_ref[...], sh_ref[...], sc_ref[...])
        return xin, h, refs[4:], (lambda g, n: None), (lambda: None)
    dest_ref, x_ref, ys_ref, gf_ref, nw_ref, sh_ref, sc_ref = refs[:7]
    gbuf, gsem = refs[-2:]
    tm = x_ref.shape[0]
    lin = pl.program_id(0) * pl.num_programs(1) + pl.program_id(1)
    last = pl.num_programs(0) * pl.num_programs(1) - 1
    slot = lin % 2

    def start_rows(tile, slot_, lo, hi):
        base = tile * tm
        for r in range(lo, hi):
            pltpu.make_async_copy(ys_ref.at[dest_ref[base + r]], gbuf.at[slot_, r], gsem.at[slot_]).start(
                priority=r % DMA_QUEUES)

    def wait_rows(slot_):
        def body(r, carry):
            pltpu.make_async_copy(ys_ref.at[0], gbuf.at[slot_, 0], gsem.at[slot_]).wait()
            return carry
        lax.fori_loop(0, tm, body, 0, unroll=8)

    @pl.when(lin == 0)
    def _():
        start_rows(0, 0, 0, tm)

    wait_rows(slot)
    xin = x_ref[...] + gf_ref[...] * gbuf[slot].reshape(x_ref.shape)
    h = _norm_mod(xin, nw_ref[...], sh_ref[...], sc_ref[...])
    nxt = jnp.minimum(lin + 1, last)

    def issue(g, n):
        start_rows(nxt, 1 - slot, g * tm // n, (g + 1) * tm // n)

    def finish():
        @pl.when(lin == last)
        def _():
            wait_rows(1 - slot)

    return xin, h, refs[7:-2], issue, finish


def _prologue_specs(has_moe, ns, d, tm):
    row = pl.BlockSpec((tm, d), lambda b, s, *_: (b * ns + s, 0))
    vec = pl.BlockSpec((None, 1, d), lambda b, s, *_: (b, 0, 0))
    one = pl.BlockSpec((1, d), lambda b, s, *_: (0, 0))
    if has_moe:
        return [row, pl.BlockSpec(memory_space=pl.ANY), vec, one, vec, vec]
    return [row, one, vec, vec]


def _prologue_call(kernel_fn, has_moe, dest, args, in_specs, out_specs, out_shape, bsz, ns, tm, d, name):
    if not has_moe:
        return pl.pallas_call(
            kernel_fn, grid=(bsz, ns), in_specs=in_specs, out_specs=out_specs, out_shape=out_shape,
            compiler_params=_cp(("arbitrary", "arbitrary")), name=name)(*args)
    return pl.pallas_call(
        kernel_fn,
        grid_spec=pltpu.PrefetchScalarGridSpec(
            num_scalar_prefetch=1, grid=(bsz, ns), in_specs=in_specs, out_specs=out_specs,
            scratch_shapes=[pltpu.VMEM((2, tm, d // LANES, LANES), F32), pltpu.SemaphoreType.DMA((2,))]),
        out_shape=out_shape, compiler_params=_cp(("arbitrary", "arbitrary")), name=name)(dest, *args)


def _full(shape):
    nd = len(shape)
    return pl.BlockSpec(shape, lambda b, s, *_: (0,) * nd)


def _mlstm_in_kernel(has_moe, dqk, *refs):
    xin, h, rest, issue, finish = _prologue(has_moe, refs)
    wqt_ref, wk_ref, wvt_ref, wo_ref, wgt_ref, wg_ref, bcol_ref, brow_ref = rest[:8]
    outs = rest[8:]
    if has_moe:
        xnew_ref = outs[0]
        outs = outs[1:]
        xnew_ref[...] = xin
    qt_ref, k_ref, vt_ref, o_ref, grow_ref, gcol_ref = outs
    hb = h.astype(BF16)
    scale = float(dqk // A_HEADS) ** -0.5
    pieces = []
    for kind, w_ref, out_ref, sc in (("t", wqt_ref, qt_ref, scale), ("n", wk_ref, k_ref, None),
                                     ("t", wvt_ref, vt_ref, None), ("n", wo_ref, o_ref, None)):
        nf = w_ref.shape[0] if kind == "t" else w_ref.shape[1]
        for f0 in range(0, nf, PIECE):
            pieces.append((kind, w_ref, out_ref, slice(f0, f0 + PIECE), sc))
    for g, (kind, w_ref, out_ref, fs, sc) in enumerate(pieces):
        issue(g, len(pieces))
        if kind == "t":
            res = _dot_nt(w_ref[fs, :], hb)
            out_ref[fs, :] = (res if sc is None else res * sc).astype(BF16)
        else:
            out_ref[:, fs] = _dot(hb, w_ref[:, fs]).astype(BF16)
    gr = _dot_nt(wgt_ref[...], hb) + bcol_ref[...]
    gr = GATE_SOFTCAP * jnp.tanh(gr / GATE_SOFTCAP)
    ridx = lax.broadcasted_iota(jnp.int32, gr.shape, 0)
    grow_ref[...] = jnp.where(ridx < A_HEADS, gr, _log_sigmoid(gr))
    gc = _dot(hb, wg_ref[...]) + brow_ref[...]
    gc = GATE_SOFTCAP * jnp.tanh(gc / GATE_SOFTCAP)
    cidx = lax.broadcasted_iota(jnp.int32, gc.shape, 1)
    gcol_ref[...] = jnp.where(cidx < A_HEADS, gc, _log_sigmoid(gc))
    finish()


def mlstm_in(x, moe, gf, nw, shift, scale, w_in, b_gates, bsz, seq):
    t, d = x.shape
    dqk, dv = d // 2, d
    has_moe = moe is not None
    tm = min(TM, seq)
    ns = seq // tm
    w_b = w_in.astype(BF16)
    w_qt = w_b[:, 0:dqk].T
    w_k = w_b[:, dqk:2 * dqk]
    w_vt = w_b[:, 2 * dqk:2 * dqk + dv].T
    w_o = w_b[:, 2 * dqk + dv:2 * dqk + 2 * dv]
    w_g = w_b[:, 2 * dqk + 2 * dv:]
    ng = 2 * A_HEADS
    ys, dest = moe if has_moe else (None, None)
    args = [x] + ([ys, gf] if has_moe else []) + [nw, shift, scale, w_qt, w_k, w_vt, w_o, w_g.T, w_g,
                                                   b_gates.reshape(ng, 1), b_gates.reshape(1, ng)]
    in_specs = _prologue_specs(has_moe, ns, d, tm) + [
        _full((dqk, d)), _full((d, dqk)), _full((dv, d)), _full((d, dv)), _full((ng, d)), _full((d, ng)),
        _full((ng, 1)), _full((1, ng))]
    row = lambda n: pl.BlockSpec((tm, n), lambda b, s, *_: (b * ns + s, 0))
    col = lambda n: pl.BlockSpec((n, tm), lambda b, s, *_: (0, b * ns + s))
    out_specs = [col(dqk), row(dqk), col(dv), row(dv),
                 pl.BlockSpec((None, ng, tm), lambda b, s, *_: (b, 0, s)),
                 pl.BlockSpec((None, tm, ng), lambda b, s, *_: (b, s, 0))]
    out_shape = [jax.ShapeDtypeStruct((dqk, t), BF16), jax.ShapeDtypeStruct((t, dqk), BF16),
                 jax.ShapeDtypeStruct((dv, t), BF16), jax.ShapeDtypeStruct((t, dv), BF16),
                 jax.ShapeDtypeStruct((bsz, ng, seq), F32), jax.ShapeDtypeStruct((bsz, seq, ng), F32)]
    if has_moe:
        out_specs = [row(d)] + out_specs
        out_shape = [jax.ShapeDtypeStruct((t, d), F32)] + out_shape
    res = _prologue_call(functools.partial(_mlstm_in_kernel, has_moe, dqk), has_moe, dest, args, in_specs, out_specs,
                         out_shape, bsz, ns, tm, d, "mlstm_in")
    if has_moe:
        return res[0], res[1:]
    return x, res


N_ROWS = 16


def _mlstm_cell_kernel(qt_ref, k_ref, vt_ref, grow_ref, gcol_ref, hs_ref, ct_ref, m_ref):
    c = pl.program_id(1)
    L = k_ref.shape[0]
    dk = k_ref.shape[1] // A_HEADS
    dvh = vt_ref.shape[0] // A_HEADS

    @pl.when(c == 0)
    def _():
        ct_ref[...] = jnp.zeros_like(ct_ref)
        m_ref[...] = jnp.zeros_like(m_ref)

    r = lax.broadcasted_iota(jnp.int32, (L, L), 0)
    cc = lax.broadcasted_iota(jnp.int32, (L, L), 1)
    visible = r <= cc
    upper = visible.astype(BF16)
    lower = (cc <= r).astype(BF16)
    grow = grow_ref[...]
    gcol = gcol_ref[...]
    b_row_all = sum(_dot(piece, upper) for piece in _split3(grow[A_HEADS:, :]))
    b_col_all = sum(_dot(lower, piece) for piece in _split3(gcol[:, A_HEADS:]))
    x_col_all = gcol[:, :A_HEADS] - b_col_all
    ones_rows = jnp.ones((N_ROWS, L), BF16)

    heads = range(A_HEADS)
    qts = [qt_ref[h * dk:(h + 1) * dk, :] for h in heads]
    khs = [k_ref[:, h * dk:(h + 1) * dk] for h in heads]
    brs = [b_row_all[h:h + 1, :] for h in heads]
    m_prevs = [m_ref[h:h + 1, 0:1] for h in heads]
    kqs = [_dot(khs[h], qts[h]) for h in heads]
    ws, scs = [], []
    for h in heads:
        d_intra = jnp.where(visible, brs[h] + x_col_all[:, h:h + 1], -jnp.inf)
        d_inter = brs[h] + m_prevs[h]
        m_t = jnp.maximum(d_inter, jnp.max(d_intra, axis=0, keepdims=True))
        ws.append((jnp.exp(d_intra - m_t) * kqs[h]).astype(BF16))
        scs.append((jnp.exp(d_inter - m_t), jnp.exp(-m_t)))
    v_augs = [jnp.concatenate([vt_ref[h * dvh:(h + 1) * dvh, :], ones_rows], axis=0) for h in heads]
    cts = [ct_ref[h] for h in heads]
    tots = [_dot(v_augs[h], ws[h]) + scs[h][0] * _dot(cts[h].astype(BF16), qts[h]) for h in heads]
    for h in heads:
        num = tots[h][:dvh, :]
        den = tots[h][dvh:dvh + 1, :]
        hh = num / jnp.maximum(jnp.abs(den), scs[h][1])
        hh = hh * lax.rsqrt(jnp.mean(hh * hh, axis=0, keepdims=True) + EPS)
        hs_ref[:, h * dvh:(h + 1) * dvh] = hh.T.astype(hs_ref.dtype)
    for h in heads:
        b_last = brs[h][:, L - 1:L]
        g = b_last - brs[h] + grow[h:h + 1, :]
        m_new = jnp.maximum(b_last + m_prevs[h], jnp.max(g, axis=1, keepdims=True))
        wk = jnp.exp(g - m_new)
        decay = jnp.exp(b_last + m_prevs[h] - m_new)
        ct_ref[h] = decay * cts[h] + _dot((v_augs[h].astype(F32) * wk).astype(BF16), khs[h])
        m_ref[h:h + 1, :] = jnp.broadcast_to(m_new, (1, m_ref.shape[1]))


def mlstm_cell(qt, k, vt, grow, gcol, bsz, seq):
    t, dqk = k.shape
    dv = vt.shape[0]
    L = min(A_L, seq)
    nc = seq // L
    ng = grow.shape[1]
    dk = dqk // A_HEADS
    dvh = dv // A_HEADS
    return pl.pallas_call(
        _mlstm_cell_kernel,
        grid=(bsz, nc),
        in_specs=[pl.BlockSpec((dqk, L), lambda b, c: (0, b * nc + c)),
                  pl.BlockSpec((L, dqk), lambda b, c: (b * nc + c, 0)),
                  pl.BlockSpec((dv, L), lambda b, c: (0, b * nc + c)),
                  pl.BlockSpec((None, ng, L), lambda b, c: (b, 0, c)),
                  pl.BlockSpec((None, L, ng), lambda b, c: (b, c, 0))],
        out_specs=pl.BlockSpec((L, dv), lambda b, c: (b * nc + c, 0)),
        out_shape=jax.ShapeDtypeStruct((t, dv), BF16),
        scratch_shapes=[pltpu.VMEM((A_HEADS, dvh + N_ROWS, dk), F32), pltpu.VMEM((A_HEADS, 128), F32)],
        compiler_params=_cp(("arbitrary", "arbitrary")), name="mlstm_cell",
    )(qt, k, vt, grow, gcol)


def _fox_q_kernel(has_moe, qscale, *refs):
    xin, h, rest, issue, finish = _prologue(has_moe, refs)
    w_ref = rest[0]
    outs = rest[1:]
    if has_moe:
        outs[0][...] = xin
        outs = outs[1:]
    hb = h.astype(BF16)
    d = w_ref.shape[1]
    n = d // PIECE
    for g in range(n):
        issue(g, n)
        cols = slice(g * PIECE, (g + 1) * PIECE)
        outs[0][:, cols] = (_dot(hb, w_ref[:, cols]) * qscale).astype(BF16)
    finish()


def fox_q(x, moe, gf, nw, shift, scale, w_q, bsz, seq):
    t, d = x.shape
    has_moe = moe is not None
    tm = min(TM, seq)
    ns = seq // tm
    ys, dest = moe if has_moe else (None, None)
    args = [x] + ([ys, gf] if has_moe else []) + [nw, shift, scale, w_q.astype(BF16)]
    in_specs = _prologue_specs(has_moe, ns, d, tm) + [_full((d, d))]
    row = pl.BlockSpec((tm, d), lambda b, s, *_: (b * ns + s, 0))
    out_specs = [row]
    out_shape = [jax.ShapeDtypeStruct((t, d), BF16)]
    if has_moe:
        out_specs = [row] + out_specs
        out_shape = [jax.ShapeDtypeStruct((t, d), F32)] + out_shape
    qscale = float(d // B_HEADS) ** -0.5 * LOG2E
    res = _prologue_call(functools.partial(_fox_q_kernel, has_moe, qscale), has_moe, dest, args, in_specs, out_specs,
                         out_shape, bsz, ns, tm, d, "fox_q")
    if has_moe:
        return res[0], res[1]
    return x, res[0]


N_BIAS_PIECES = 3


def _bias_select_matrices(d, n_heads):
    import numpy as np
    dh = d // n_heads
    pq = np.zeros((N_BIAS_PIECES, n_heads, d), np.float32)
    pk = np.zeros((N_BIAS_PIECES, n_heads, d), np.float32)
    oq = np.zeros((1, d), np.float32)
    ok = np.zeros((1, d), np.float32)
    for h in range(n_heads):
        base = (h // 2) * 2 * dh + (dh if h % 2 == 0 else 0)
        for p in range(N_BIAS_PIECES):
            pq[p, h, base + p] = 1.0
            ok[0, base + p] = 1.0
            oq[0, base + N_BIAS_PIECES + p] = 1.0
            pk[p, h, base + N_BIAS_PIECES + p] = -1.0
    return pq, pk, oq, ok


def _split3(f):
    a = f.astype(BF16)
    r1 = f - a.astype(F32)
    b = r1.astype(BF16)
    c = (r1 - b.astype(F32)).astype(BF16)
    return a, b, c


def _shared_kv_kernel(x_ref, nw_ref, sh_ref, sc_ref, w_ref, wvt_ref, wf_ref, brow_ref, lo_ref, pq_ref, pk_ref, oq_ref,
                      ok_ref, ke_ref, ko_ref, vt_ref, eq_ref, cc_ref):
    s = pl.program_id(1)
    d = x_ref.shape[1]
    dh = d // B_HEADS

    @pl.when(s == 0)
    def _():
        cc_ref[...] = jnp.zeros_like(cc_ref)

    h = _norm_mod(x_ref[...], nw_ref[...], sh_ref[...], sc_ref[...])
    hb = h.astype(BF16)
    lc = _log_sigmoid(_dot(hb, wf_ref[...]) + brow_ref[...])
    fcum = sum(_dot(lo_ref[...], piece) for piece in _split3(lc)) + cc_ref[...]
    cc_ref[...] += jnp.sum(lc, axis=0, keepdims=True)
    pieces = jnp.concatenate(_split3(fcum * LOG2E), axis=1)
    eq = oq_ref[...] + _dot(pieces, pq_ref[...])
    ek = ok_ref[...] + _dot(pieces, pk_ref[...])
    eq_ref[...] = eq.astype(BF16)
    even = (lax.broadcasted_iota(jnp.int32, (1, d), 1) % (2 * dh)) < dh
    k = _dot(hb, w_ref[...])
    ke_ref[...] = jnp.where(even, k, ek).astype(BF16)
    ko_ref[...] = jnp.where(even, ek, k).astype(BF16)
    vt_ref[...] = _dot_nt(wvt_ref[...], hb).astype(BF16)


def shared_kv(x, nw, shift, scale, w_kv, b_fgate, bsz, seq):
    t, d = x.shape
    tm = min(TM, seq)
    ns = seq // tm
    nh = B_HEADS
    w_k = w_kv[:, :d].astype(BF16)
    w_vt = w_kv[:, d:2 * d].astype(BF16).T
    w_f = w_kv[:, 2 * d:].astype(BF16)
    idx = jnp.arange(tm)
    lower = (idx[None, :] <= idx[:, None]).astype(BF16)
    pq, pk, oq, ok = _bias_select_matrices(d, nh)
    row = pl.BlockSpec((tm, d), lambda b, s: (b * ns + s, 0))
    vec = pl.BlockSpec((None, 1, d), lambda b, s: (b, 0, 0))
    sel = _full((N_BIAS_PIECES * nh, d))
    return pl.pallas_call(
        _shared_kv_kernel,
        grid=(bsz, ns),
        in_specs=[row, _full((1, d)), vec, vec, _full((d, d)), _full((d, d)), _full((d, nh)), _full((1, nh)),
                  _full((tm, tm)), sel, sel, _full((1, d)), _full((1, d))],
        out_specs=[row, row, pl.BlockSpec((d, tm), lambda b, s: (0, b * ns + s)), row],
        out_shape=[jax.ShapeDtypeStruct((t, d), BF16), jax.ShapeDtypeStruct((t, d), BF16),
                   jax.ShapeDtypeStruct((d, t), BF16), jax.ShapeDtypeStruct((t, d), BF16)],
        scratch_shapes=[pltpu.VMEM((1, nh), F32)],
        compiler_params=_cp(("arbitrary", "arbitrary")), name="shared_kv",
    )(x, nw, shift, scale, w_k, w_vt, w_f, b_fgate.reshape(1, nh), lower,
      jnp.asarray(pq.reshape(-1, d), BF16), jnp.asarray(pk.reshape(-1, d), BF16), jnp.asarray(oq), jnp.asarray(ok))


def _fox_attn_kernel(q_ref, eq_ref, ke_ref, ko_ref, vt_ref, o_ref, st_ref):
    i = pl.program_id(2)
    tq = q_ref.shape[0]
    tk = tq
    ck = min(ATT_CHUNK, tk)
    nchunk = tk // ck
    dh2 = q_ref.shape[1]
    dh = dh2 // 2
    lo_mask = lax.broadcasted_iota(jnp.int32, (1, dh2), 1) < dh
    q = q_ref[...]
    eq = eq_ref[...]
    qs = (jnp.where(lo_mask, q, eq), jnp.where(lo_mask, eq, q))
    k_refs = (ke_ref, ko_ref)
    ones_rows = jnp.ones((N_ROWS, ck), BF16)

    def tail(x, q0, new):
        return new if q0 == 0 else jnp.concatenate([x[:, :q0], new], axis=1)

    def qk_chunk(e, kt, c, mx, diag):
        off = pl.multiple_of(kt * tk + c * ck, ck)
        q0 = c * ck if diag else 0
        st = _dot_nt(k_refs[e][pl.ds(off, ck), :], qs[e][q0:, :])
        if diag:
            visible = (lax.broadcasted_iota(jnp.int32, st.shape, 0) <= lax.broadcasted_iota(jnp.int32, st.shape, 1))
            st = jnp.where(visible, st, -jnp.inf)
        st_ref[e, c * ck:(c + 1) * ck, q0:] = st
        cm = jnp.max(st, axis=0, keepdims=True)
        return cm if mx is None else tail(mx, q0, jnp.maximum(mx[:, q0:], cm))

    def pv_chunk(e, kt, c, state, diag):
        m, acc = state
        off = pl.multiple_of(kt * tk + c * ck, ck)
        q0 = c * ck if diag else 0
        pt = jnp.exp2(st_ref[e, c * ck:(c + 1) * ck, q0:] - m[:, q0:])
        v_aug = jnp.concatenate([vt_ref[e * dh:(e + 1) * dh, pl.ds(off, ck)], ones_rows], axis=0)
        return m, tail(acc, q0, acc[:, q0:] + _dot(v_aug, pt.astype(BF16)))

    def phase(cur, state, tile_max, nxt):
        m, acc = state
        m_new = jnp.maximum(m, tile_max)
        state = (m_new, jnp.exp2(m - m_new) * acc)
        mx = None
        for c in range(nchunk):
            mx = qk_chunk(nxt[0], nxt[1], c, mx, nxt[2])
            state = pv_chunk(cur[0], cur[1], c, state, cur[2])
        return state, mx

    init = (jnp.full((1, tq), -1e30, F32), jnp.zeros((dh + N_ROWS, tq), F32))
    mx0 = None
    for c in range(nchunk):
        mx0 = qk_chunk(0, i, c, mx0, True)
    s0, mx1 = phase((0, i, True), init, mx0, (1, i, True))
    s1, mx0 = phase((1, i, True), init, mx1, (0, 0, False))

    def trip(j, carry):
        s0, s1, mx0 = carry
        s0, mx1 = phase((0, j, False), s0, mx0, (1, j, False))
        s1, mx0 = phase((1, j, False), s1, mx1, (0, j + 1, False))
        return s0, s1, mx0

    s0, s1, _ = lax.fori_loop(0, i, trip, (s0, s1, mx0))
    ot = jnp.concatenate([acc[:dh] / acc[dh:dh + 1] for _, acc in (s0, s1)], axis=0)
    o_ref[...] = ot.T.astype(o_ref.dtype)


def fox_attn(q, eq, ke, ko, vt, bsz, seq):
    t, d = q.shape
    nh = B_HEADS
    dh2 = 2 * (d // nh)
    tq = min(TQ, seq)
    nq = seq // tq
    qspec = pl.BlockSpec((tq, dh2), lambda b, hp, i: (b * nq + i, hp))
    kspec = pl.BlockSpec((seq, dh2), lambda b, hp, i: (b, hp))
    vspec = pl.BlockSpec((dh2, seq), lambda b, hp, i: (hp, b))
    return pl.pallas_call(
        _fox_attn_kernel,
        grid=(bsz, nh // 2, nq),
        in_specs=[qspec, qspec, kspec, kspec, vspec],
        out_specs=qspec,
        out_shape=jax.ShapeDtypeStruct((t, d), BF16),
        scratch_shapes=[pltpu.VMEM((2, tq, tq), F32)],
        compiler_params=_cp(("arbitrary", "arbitrary", "arbitrary")), name="fox_attn",
    )(q, eq, ke, ko, vt)


def _route(logits, bias_col):
    aff = _sigmoid(logits)
    sel = aff + bias_col
    s = [sel[e:e + 1, :] for e in range(N_EXPERTS)]
    gs = []
    for g in range(N_GROUPS):
        v = s[EPG * g:EPG * (g + 1)]
        best = v[0] + v[1]
        for i in range(EPG):
            for j in range(i + 1, EPG):
                if (i, j) != (0, 1):
                    best = jnp.maximum(best, v[i] + v[j])
        gs.append(best)
    grp = jnp.zeros_like(gs[0], dtype=jnp.int32)
    best = gs[0]
    for g in range(1, N_GROUPS):
        upd = gs[g] > best
        grp = jnp.where(upd, g, grp)
        best = jnp.where(upd, gs[g], best)

    def pick(arrs, j):
        out = arrs[(N_GROUPS - 1) * EPG + j]
        for g in range(N_GROUPS - 2, -1, -1):
            out = jnp.where(grp == g, arrs[g * EPG + j], out)
        return out

    v = [pick(s, j) for j in range(EPG)]
    i1 = jnp.zeros_like(grp)
    b1 = v[0]
    for j in range(1, EPG):
        upd = v[j] > b1
        i1 = jnp.where(upd, j, i1)
        b1 = jnp.where(upd, v[j], b1)
    i2 = jnp.full_like(grp, -1)
    b2 = jnp.full_like(b1, -jnp.inf)
    for j in range(EPG):
        upd = (i1 != j) & ((i2 < 0) | (v[j] > b2))
        i2 = jnp.where(upd, j, i2)
        b2 = jnp.where(upd, v[j], b2)
    lo = jnp.minimum(i1, i2)
    hi = jnp.maximum(i1, i2)
    base = jnp.where(lo == 0, 0, jnp.where(lo == 1, 3, 5))
    return grp * N_PAIRS + base + (hi - lo - 1)


def _post_mix_kernel(is_mlstm, *refs):
    if is_mlstm:
        hs_ref, og_ref, mh_ref = refs[:3]
        refs = refs[3:]
        og = og_ref[...].astype(F32)
        mix = hs_ref[...].astype(F32) * mh_ref[...] * _sigmoid(og)
    else:
        mix = refs[0][...]
        refs = refs[1:]
    (w_ref, x_ref, gm_ref, nw_ref, sh_ref, sc_ref, wrt_ref, rb_ref, us_ref,
     xnew_ref, h2_ref, cls_ref, rank_ref, cnt_ref) = refs
    first = (pl.program_id(0) == 0) & (pl.program_id(1) == 0)

    @pl.when(first)
    def _():
        cnt_ref[...] = jnp.zeros_like(cnt_ref)

    y = _dot(mix.astype(BF16), w_ref[...])
    xnew = x_ref[...] + gm_ref[...] * y
    xnew_ref[...] = xnew
    h2 = _norm_mod(xnew, nw_ref[...], sh_ref[...], sc_ref[...])
    h2_ref[...] = h2.reshape(h2_ref.shape)
    h_hi = h2.astype(BF16)
    h_lo = (h2 - h_hi.astype(F32)).astype(BF16)
    ne = wrt_ref.shape[0] // 2
    part = _dot_nt(wrt_ref[...], h_hi)
    logits = part[:ne] + part[ne:] + _dot_nt(wrt_ref[:ne, :], h_lo)
    cls = _route(logits, rb_ref[...])
    cls_ref[...] = cls
    tm = cls.shape[1]
    onehot = (lax.broadcasted_iota(jnp.int32, (CLS_PAD, tm), 0) == cls).astype(F32)
    prefix = _dot(onehot.astype(BF16), us_ref[...])
    carry = cnt_ref[:, 0:1]
    rank = jnp.sum(onehot * (prefix + carry), axis=0, keepdims=True)
    rank_ref[...] = rank.astype(jnp.int32)
    cnt_ref[...] += jnp.sum(onehot, axis=1, keepdims=True)


def post_mix(mix_args, is_mlstm, w_o, x, gm, nw, shift, scale, w_router, router_bias, bsz, seq):
    t, d = x.shape
    tm = min(TM, seq)
    ns = seq // tm
    ne = N_EXPERTS
    idx = jnp.arange(tm)
    upper_strict = (idx[:, None] < idx[None, :]).astype(BF16)
    row = pl.BlockSpec((tm, d), lambda b, s: (b * ns + s, 0))
    vec = pl.BlockSpec((None, 1, d), lambda b, s: (b, 0, 0))
    lane_row = pl.BlockSpec((None, 1, tm), lambda b, s: (b, 0, s))
    if is_mlstm:
        mix_specs = [row, row, _full((1, d))]
    else:
        mix_specs = [row]
    in_specs = mix_specs + [_full((d, d)), row, vec, _full((1, d)), vec, vec, _full((2 * ne, d)), _full((ne, 1)),
                            _full((tm, tm))]
    wr = w_router.T
    wr_hi = wr.astype(BF16)
    wr_split = jnp.concatenate([wr_hi, (wr - wr_hi.astype(F32)).astype(BF16)], axis=0)
    row3 = pl.BlockSpec((tm, d // LANES, LANES), lambda b, s: (b * ns + s, 0, 0))
    out_specs = [row, row3, lane_row, lane_row, _full((CLS_PAD, 128))]
    out_shape = [jax.ShapeDtypeStruct((t, d), F32), jax.ShapeDtypeStruct((t, d // LANES, LANES), F32),
                 jax.ShapeDtypeStruct((bsz, 1, seq), jnp.int32), jax.ShapeDtypeStruct((bsz, 1, seq), jnp.int32),
                 jax.ShapeDtypeStruct((CLS_PAD, 128), F32)]
    return pl.pallas_call(
        functools.partial(_post_mix_kernel, is_mlstm),
        grid=(bsz, ns), in_specs=in_specs, out_specs=out_specs, out_shape=out_shape,
        compiler_params=_cp(("arbitrary", "arbitrary")), name="post_mix",
    )(*mix_args, w_o.astype(BF16), x, gm, nw, shift, scale, wr_split, router_bias.reshape(ne, 1), upper_strict)


_FILL_PIECES = tuple(1 << k for k in range(MB.bit_length() - 2, -1, -1))


def _row_scatter_kernel(rb, dest_ref, fill_start_ref, fill_n_ref, src_ref, dst_ref, zero_ref, sem, zsem):
    base = pl.program_id(0) * rb

    def fill(do):
        for c in range(N_CLASSES):
            n = fill_n_ref[c]
            for p in _FILL_PIECES:
                row = fill_start_ref[c] + (n & ~(2 * p - 1))

                @pl.when((n & p) != 0)
                def _():
                    do(pltpu.make_async_copy(zero_ref.at[pl.ds(0, p)], dst_ref.at[pl.ds(row, p)], zsem))
        for k in range(N_CLASSES):
            row = pl.multiple_of(fill_start_ref[N_CLASSES] + k * MB, MB)

            @pl.when(k < fill_n_ref[N_CLASSES])
            def _():
                do(pltpu.make_async_copy(zero_ref, dst_ref.at[pl.ds(row, MB)], zsem))

    @pl.when(pl.program_id(0) == 0)
    def _():
        zero_ref[...] = jnp.zeros_like(zero_ref)
        fill(lambda cp: cp.start())

    def start(g, carry):
        for u in range(DMA_GROUP):
            rr = g * DMA_GROUP + u
            pltpu.make_async_copy(src_ref.at[rr], dst_ref.at[dest_ref[base + rr]], sem).start(priority=u % DMA_QUEUES)
        return carry

    def wait(rr, carry):
        pltpu.make_async_copy(src_ref.at[0], dst_ref.at[0], sem).wait()
        return carry

    lax.fori_loop(0, rb // DMA_GROUP, start, 0)
    lax.fori_loop(0, rb, wait, 0, unroll=8)

    @pl.when(pl.program_id(0) == 0)
    def _():
        fill(lambda cp: cp.wait())


def row_scatter(dest, fill_start, fill_n, src, n_rows):
    t = src.shape[0]
    rb = min(RB, t)
    return pl.pallas_call(
        functools.partial(_row_scatter_kernel, rb),
        grid_spec=pltpu.PrefetchScalarGridSpec(
            num_scalar_prefetch=3, grid=(t // rb,),
            in_specs=[pl.BlockSpec((rb,) + src.shape[1:], lambda i, *_: (i, 0, 0))],
            out_specs=pl.BlockSpec(memory_space=pl.ANY),
            scratch_shapes=[pltpu.VMEM((MB,) + src.shape[1:], src.dtype),
                            pltpu.SemaphoreType.DMA, pltpu.SemaphoreType.DMA]),
        out_shape=jax.ShapeDtypeStruct((n_rows,) + src.shape[1:], src.dtype),
        compiler_params=_cp(("arbitrary",)), name="row_scatter",
    )(dest, fill_start, fill_n, src)


def _experts_kernel(elo_ref, ehi_ref, nused_ref, x_ref, wrt_ref,
                    wg_lo, wu_lo, wd_lo, wg_hi, wu_hi, wd_hi, y_ref):
    i = pl.program_id(0)

    @pl.when(i < nused_ref[0])
    def _():
        x = x_ref[...].reshape(x_ref.shape[0], -1)
        xb = x.astype(BF16)
        aff_lo = _sigmoid(jnp.sum(x * wrt_ref[pl.ds(elo_ref[i], 1), :], axis=1, keepdims=True))
        aff_hi = _sigmoid(jnp.sum(x * wrt_ref[pl.ds(ehi_ref[i], 1), :], axis=1, keepdims=True))
        tot = aff_lo + aff_hi
        acc = None
        for gate, wg, wu, wd in ((aff_lo / tot, wg_lo, wu_lo, wd_lo), (aff_hi / tot, wg_hi, wu_hi, wd_hi)):
            hg = _dot(xb, wg[...])
            hu = _dot(xb, wu[...])
            act = hg * _sigmoid(hg) * hu * gate
            part = _dot(act.astype(BF16), wd[...])
            acc = part if acc is None else acc + part
        y_ref[...] = acc.reshape(y_ref.shape)

    @pl.when(i >= nused_ref[0])
    def _():
        y_ref[...] = jnp.zeros_like(y_ref)


def experts(xs, w_router_t, blk_lo, blk_hi, nused, w_gate, w_up, w_down):
    p = xs.shape[0]
    d = xs.shape[1] * xs.shape[2]
    nblk = p // MB
    de = w_gate.shape[2]
    xrow = pl.BlockSpec((MB,) + xs.shape[1:], lambda i, lo, hi, nu: (i, 0, 0))
    xrow_in = pl.BlockSpec((MB,) + xs.shape[1:], lambda i, lo, hi, nu: (jnp.minimum(i, nu[0] - 1), 0, 0))
    wrt = pl.BlockSpec(w_router_t.shape, lambda i, lo, hi, nu: (0, 0))
    w_in_lo = pl.BlockSpec((None, d, de), lambda i, lo, hi, nu: (lo[i], 0, 0))
    w_in_hi = pl.BlockSpec((None, d, de), lambda i, lo, hi, nu: (hi[i], 0, 0))
    w_out_lo = pl.BlockSpec((None, de, d), lambda i, lo, hi, nu: (lo[i], 0, 0))
    w_out_hi = pl.BlockSpec((None, de, d), lambda i, lo, hi, nu: (hi[i], 0, 0))
    return pl.pallas_call(
        _experts_kernel,
        grid_spec=pltpu.PrefetchScalarGridSpec(
            num_scalar_prefetch=3, grid=(nblk,),
            in_specs=[xrow_in, wrt, w_in_lo, w_in_lo, w_out_lo, w_in_hi, w_in_hi, w_out_hi],
            out_specs=xrow),
        out_shape=jax.ShapeDtypeStruct(xs.shape, F32),
        compiler_params=_cp(("arbitrary",)), name="experts",
    )(blk_lo, blk_hi, nused, xs, w_router_t, w_gate, w_up, w_down, w_gate, w_up, w_down)


_PAIR_LO = (0, 0, 0, 1, 1, 2)
_PAIR_HI = (1, 2, 3, 2, 3, 3)


def moe(h2, cls, rank, counts, w_router_t, w_gate, w_up, w_down):
    t = h2.shape[0]
    p = t + N_CLASSES * MB
    nblk = p // MB
    cls = cls.reshape(t)
    counts = counts[:N_CLASSES, 0].astype(jnp.int32)
    padded = (counts + MB - 1) // MB * MB
    pad_end = jnp.cumsum(padded)
    pad_start = pad_end - padded
    dest = (pad_start[cls] + rank.reshape(t)).astype(jnp.int32)
    blk_row = jnp.arange(nblk, dtype=jnp.int32) * MB
    blk_cls = jnp.minimum(jnp.sum((pad_end[None, :] <= blk_row[:, None]).astype(jnp.int32), axis=1), N_CLASSES - 1)
    grp = blk_cls // N_PAIRS
    pr = blk_cls % N_PAIRS
    blk_lo = grp * EPG + jnp.asarray(_PAIR_LO, jnp.int32)[pr]
    blk_hi = grp * EPG + jnp.asarray(_PAIR_HI, jnp.int32)[pr]
    nused = (pad_end[-1:] // MB).astype(jnp.int32)
    fill_start = jnp.concatenate([pad_start + counts, pad_end[-1:]]).astype(jnp.int32)
    fill_n = jnp.concatenate([padded - counts, (p - pad_end[-1:]) // MB]).astype(jnp.int32)
    xs = row_scatter(dest, fill_start, fill_n, h2, p)
    ys = experts(xs, w_router_t, blk_lo, blk_hi, nused, w_gate, w_up, w_down)
    return ys, dest


def _final_kernel(*refs):
    _, h, rest, issue, finish = _prologue(True, refs)
    n = 8
    rows = h.shape[0] // n
    for g in range(n):
        issue(g, n)
        rest[0][g * rows:(g + 1) * rows, :] = h[g * rows:(g + 1) * rows, :]
    finish()


def final_norm(x, moe_out, gf, nw, shift, scale, bsz, seq):
    t, d = x.shape
    tm = min(TM, seq)
    ns = seq // tm
    ys, dest = moe_out
    row = pl.BlockSpec((tm, d), lambda b, s, *_: (b * ns + s, 0))
    return _prologue_call(_final_kernel, True, dest, [x, ys, gf, nw, shift, scale], _prologue_specs(True, ns, d, tm),
                          row, jax.ShapeDtypeStruct((t, d), F32), bsz, ns, tm, d, "final_norm")


def kernel(x, c, a_w_in, a_b_gates, a_mh_norm, a_w_out, kv_norm, w_ada_kv, b_ada_kv, w_kv, b_fgate, b_w_q, b_w_o,
           norm_mix, norm_ffn, w_ada, b_ada, w_router, router_bias, w_gate, w_up, w_down, norm_final, w_ada_final,
           b_ada_final):
    bsz, seq, d = x.shape
    depth = w_ada.shape[0]
    n_a = a_w_in.shape[0]
    t = bsz * seq
    xf = x.reshape(t, d)

    mods = ada(c, w_ada, b_ada)
    kv_mod = ada(c, w_ada_kv[None], b_ada_kv[None])[0]
    fin = ada(c, w_ada_final[None], b_ada_final[None])[0]

    def vecs(m, n):
        return [m[:, None, i * d:(i + 1) * d] for i in range(n)]

    wg_b = w_gate.astype(BF16)
    wu_b = w_up.astype(BF16)
    wd_b = w_down.astype(BF16)

    moe_out = None
    gf_prev = None
    kv = None
    for layer in range(depth):
        sh_m, sc_m, g_m, sh_f, sc_f, g_f = vecs(mods[layer], 6)
        nm = norm_mix[layer].reshape(1, d)
        if layer < n_a:
            xf, (q, k, v, o, grow, gcol) = mlstm_in(xf, moe_out, gf_prev, nm, sh_m, sc_m, a_w_in[layer],
                                                     a_b_gates[layer], bsz, seq)
            hs = mlstm_cell(q, k, v, grow, gcol, bsz, seq)
            mix_args = (hs, o, a_mh_norm[layer].reshape(1, d))
            w_o = a_w_out[layer]
        else:
            j = layer - n_a
            xf, q = fox_q(xf, moe_out, gf_prev, nm, sh_m, sc_m, b_w_q[j], bsz, seq)
            if layer == n_a:
                kv_sh, kv_sc = vecs(kv_mod, 2)
                kv = shared_kv(xf, kv_norm.reshape(1, d), kv_sh, kv_sc, w_kv, b_fgate, bsz, seq)
            ke, ko, vt, eq = kv
            att = fox_attn(q, eq, ke, ko, vt, bsz, seq)
            mix_args = (att,)
            w_o = b_w_o[j]
        xf, h2, cls, rank, counts = post_mix(
            mix_args, layer < n_a, w_o, xf, g_m, norm_ffn[layer].reshape(1, d), sh_f, sc_f, w_router, router_bias,
            bsz, seq)
        moe_out = moe(h2, cls, rank, counts, w_router.T, wg_b[layer], wu_b[layer], wd_b[layer])
        gf_prev = g_f
    fin_sh, fin_sc = vecs(fin, 2)
    out = final_norm(xf, moe_out, gf_prev, norm_final.reshape(1, d), fin_sh, fin_sc, bsz, seq)
    return out.reshape(bsz, seq, d)
```

```python
import functools

import jax
import jax.numpy as jnp
from jax import lax
from jax.experimental import pallas as pl
from jax.experimental.pallas import tpu as pltpu

F32 = jnp.float32
BF16 = jnp.bfloat16
EPS = 1e-6
GATE_SOFTCAP = 15.0

A_HEADS = 8
B_HEADS = 16
N_EXPERTS = 16
N_GROUPS = 4
EPG = N_EXPERTS // N_GROUPS
N_PAIRS = 6
N_CLASSES = N_GROUPS * N_PAIRS
CLS_PAD = 32

VMEM_LIMIT = 56 * 1024 * 1024

TM = 512
A_L = 256
TQ = 1024
ATT_CHUNK = 256
MB = 256
RB = 2048
DMA_GROUP = 8
DMA_QUEUES = 2
PIECE = 256
LANES = 128
LOG2E = 1.4426950408889634

HIGHEST = lax.Precision.HIGHEST


def _cp(sem):
    return pltpu.CompilerParams(dimension_semantics=sem, vmem_limit_bytes=VMEM_LIMIT)


def _dot(a, b, precision=None):
    return jnp.dot(a, b, preferred_element_type=F32, precision=precision)


def _dot_nt(a, b, precision=None):
    return lax.dot_general(a, b, (((1,), (1,)), ((), ())), preferred_element_type=F32, precision=precision)


def _dot_tn(a, b, precision=None):
    return lax.dot_general(a, b, (((0,), (0,)), ((), ())), preferred_element_type=F32, precision=precision)


def _norm_mod(xin, nw, shift, scale):
    ms = jnp.mean(xin * xin, axis=-1, keepdims=True)
    y = xin * lax.rsqrt(ms + EPS)
    return (y * nw) * (1.0 + scale) + shift


def _log_sigmoid(x):
    return jnp.minimum(x, 0.0) - jnp.log1p(jnp.exp(-jnp.abs(x)))


def _sigmoid(x):
    return 1.0 / (1.0 + jnp.exp(-x))


def _ada_kernel(c_ref, w_ref, b_ref, o_ref):
    c = c_ref[...]
    ca = c * _sigmoid(c)
    o_ref[...] = _dot(ca, w_ref[...], HIGHEST) + b_ref[...]


def ada(c, w, b):
    nl, d, n = w.shape
    bsz = c.shape[0]
    tn = 512
    return pl.pallas_call(
        _ada_kernel,
        grid=(nl, n // tn),
        in_specs=[
            pl.BlockSpec((bsz, d), lambda l, j: (0, 0)),
            pl.BlockSpec((None, d, tn), lambda l, j: (l, 0, j)),
            pl.BlockSpec((None, 1, tn), lambda l, j: (l, 0, j)),
        ],
        out_specs=pl.BlockSpec((None, bsz, tn), lambda l, j: (l, 0, j)),
        out_shape=jax.ShapeDtypeStruct((nl, bsz, n), F32),
        compiler_params=_cp(("arbitrary", "arbitrary")),
        name="ada",
    )(c, w, b.reshape(nl, 1, n))


def _prologue(has_moe, refs):
    if not has_moe:
        x_ref, nw_ref, sh_ref, sc_ref = refs[:4]
        xin = x_ref[...]
        h = _norm_mod(xin, nw_ref[...], sh_ref[...], sc_ref[...])
        return xin, h, refs[4:], (lambda g, n: None), (lambda: None)
    dest_ref, x_ref, ys_ref, gf_ref, nw_ref, sh_ref, sc_ref = refs[:7]
    gbuf, gsem = refs[-2:]
    tm = x_ref.shape[0]
    lin = pl.program_id(0) * pl.num_programs(1) + pl.program_id(1)
    last = pl.num_programs(0) * pl.num_programs(1) - 1
    slot = lin % 2

    def start_rows(tile, slot_, lo, hi):
        base = tile * tm
        for r in range(lo, hi):
            pltpu.make_async_copy(ys_ref.at[dest_ref[base + r]], gbuf.at[slot_, r], gsem.at[slot_]).start(
                priority=r % DMA_QUEUES)

    def wait_rows(slot_):
        def body(r, carry):
            pltpu.make_async_copy(ys_ref.at[0], gbuf.at[slot_, 0], gsem.at[slot_]).wait()
            return carry
        lax.fori_loop(0, tm, body, 0, unroll=8)

    @pl.when(lin == 0)
    def _():
        start_rows(0, 0, 0, tm)

    wait_rows(slot)
    xin = x_ref[...] + gf_ref[...] * gbuf[slot].reshape(x_ref.shape)
    h = _norm_mod(xin, nw_ref[...], sh_ref[...], sc_ref[...])
    nxt = jnp.minimum(lin + 1, last)

    def issue(g, n):
        start_rows(nxt, 1 - slot, g * tm // n, (g + 1) * tm // n)

    def finish():
        @pl.when(lin == last)
        def _():
            wait_rows(1 - slot)

    return xin, h, refs[7:-2], issue, finish


def _prologue_specs(has_moe, ns, d, tm):
    row = pl.BlockSpec((tm, d), lambda b, s, *_: (b * ns + s, 0))
    vec = pl.BlockSpec((None, 1, d), lambda b, s, *_: (b, 0, 0))
    one = pl.BlockSpec((1, d), lambda b, s, *_: (0, 0))
    if has_moe:
        return [row, pl.BlockSpec(memory_space=pl.ANY), vec, one, vec, vec]
    return [row, one, vec, vec]


def _prologue_call(kernel_fn, has_moe, dest, args, in_specs, out_specs, out_shape, bsz, ns, tm, d, name):
    if not has_moe:
        return pl.pallas_call(
            kernel_fn, grid=(bsz, ns), in_specs=in_specs, out_specs=out_specs, out_shape=out_shape,
            compiler_params=_cp(("arbitrary", "arbitrary")), name=name)(*args)
    return pl.pallas_call(
        kernel_fn,
        grid_spec=pltpu.PrefetchScalarGridSpec(
            num_scalar_prefetch=1, grid=(bsz, ns), in_specs=in_specs, out_specs=out_specs,
            scratch_shapes=[pltpu.VMEM((2, tm, d // LANES, LANES), F32), pltpu.SemaphoreType.DMA((2,))]),
        out_shape=out_shape, compiler_params=_cp(("arbitrary", "arbitrary")), name=name)(dest, *args)


def _full(shape):
    nd = len(shape)
    return pl.BlockSpec(shape, lambda b, s, *_: (0,) * nd)


def _mlstm_in_kernel(has_moe, dqk, *refs):
    xin, h, rest, issue, finish = _prologue(has_moe, refs)
    wqt_ref, wk_ref, wvt_ref, wo_ref, wgt_ref, wg_ref, bcol_ref, brow_ref = rest[:8]
    outs = rest[8:]
    if has_moe:
        xnew_ref = outs[0]
        outs = outs[1:]
        xnew_ref[...] = xin
    qt_ref, k_ref, vt_ref, o_ref, grow_ref, gcol_ref = outs
    hb = h.astype(BF16)
    scale = float(dqk // A_HEADS) ** -0.5
    pieces = []
    for kind, w_ref, out_ref, sc in (("t", wqt_ref, qt_ref, scale), ("n", wk_ref, k_ref, None),
                                     ("t", wvt_ref, vt_ref, None), ("n", wo_ref, o_ref, None)):
        nf = w_ref.shape[0] if kind == "t" else w_ref.shape[1]
        for f0 in range(0, nf, PIECE):
            pieces.append((kind, w_ref, out_ref, slice(f0, f0 + PIECE), sc))
    for g, (kind, w_ref, out_ref, fs, sc) in enumerate(pieces):
        issue(g, len(pieces))
        if kind == "t":
            res = _dot_nt(w_ref[fs, :], hb)
            out_ref[fs, :] = (res if sc is None else res * sc).astype(BF16)
        else:
            out_ref[:, fs] = _dot(hb, w_ref[:, fs]).astype(BF16)
    gr = _dot_nt(wgt_ref[...], hb) + bcol_ref[...]
    gr = GATE_SOFTCAP * jnp.tanh(gr / GATE_SOFTCAP)
    ridx = lax.broadcasted_iota(jnp.int32, gr.shape, 0)
    grow_ref[...] = jnp.where(ridx < A_HEADS, gr, _log_sigmoid(gr))
    gc = _dot(hb, wg_ref[...]) + brow_ref[...]
    gc = GATE_SOFTCAP * jnp.tanh(gc / GATE_SOFTCAP)
    cidx = lax.broadcasted_iota(jnp.int32, gc.shape, 1)
    gcol_ref[...] = jnp.where(cidx < A_HEADS, gc, _log_sigmoid(gc))
    finish()


def mlstm_in(x, moe, gf, nw, shift, scale, w_in, b_gates, bsz, seq):
    t, d = x.shape
    dqk, dv = d // 2, d
    has_moe = moe is not None
    tm = min(TM, seq)
    ns = seq // tm
    w_b = w_in.astype(BF16)
    w_qt = w_b[:, 0:dqk].T
    w_k = w_b[:, dqk:2 * dqk]
    w_vt = w_b[:, 2 * dqk:2 * dqk + dv].T
    w_o = w_b[:, 2 * dqk + dv:2 * dqk + 2 * dv]
    w_g = w_b[:, 2 * dqk + 2 * dv:]
    ng = 2 * A_HEADS
    ys, dest = moe if has_moe else (None, None)
    args = [x] + ([ys, gf] if has_moe else []) + [nw, shift, scale, w_qt, w_k, w_vt, w_o, w_g.T, w_g,
                                                   b_gates.reshape(ng, 1), b_gates.reshape(1, ng)]
    in_specs = _prologue_specs(has_moe, ns, d, tm) + [
        _full((dqk, d)), _full((d, dqk)), _full((dv, d)), _full((d, dv)), _full((ng, d)), _full((d, ng)),
        _full((ng, 1)), _full((1, ng))]
    row = lambda n: pl.BlockSpec((tm, n), lambda b, s, *_: (b * ns + s, 0))
    col = lambda n: pl.BlockSpec((n, tm), lambda b, s, *_: (0, b * ns + s))
    out_specs = [col(dqk), row(dqk), col(dv), row(dv),
                 pl.BlockSpec((None, ng, tm), lambda b, s, *_: (b, 0, s)),
                 pl.BlockSpec((None, tm, ng), lambda b, s, *_: (b, s, 0))]
    out_shape = [jax.ShapeDtypeStruct((dqk, t), BF16), jax.ShapeDtypeStruct((t, dqk), BF16),
                 jax.ShapeDtypeStruct((dv, t), BF16), jax.ShapeDtypeStruct((t, dv), BF16),
                 jax.ShapeDtypeStruct((bsz, ng, seq), F32), jax.ShapeDtypeStruct((bsz, seq, ng), F32)]
    if has_moe:
        out_specs = [row(d)] + out_specs
        out_shape = [jax.ShapeDtypeStruct((t, d), F32)] + out_shape
    res = _prologue_call(functools.partial(_mlstm_in_kernel, has_moe, dqk), has_moe, dest, args, in_specs, out_specs,
                         out_shape, bsz, ns, tm, d, "mlstm_in")
    if has_moe:
        return res[0], res[1:]
    return x, res


N_ROWS = 16


def _mlstm_cell_kernel(qt_ref, k_ref, vt_ref, grow_ref, gcol_ref, hs_ref, ct_ref, m_ref):
    c = pl.program_id(1)
    L = k_ref.shape[0]
    dk = k_ref.shape[1] // A_HEADS
    dvh = vt_ref.shape[0] // A_HEADS

    @pl.when(c == 0)
    def _():
        ct_ref[...] = jnp.zeros_like(ct_ref)
        m_ref[...] = jnp.zeros_like(m_ref)

    r = lax.broadcasted_iota(jnp.int32, (L, L), 0)
    cc = lax.broadcasted_iota(jnp.int32, (L, L), 1)
    visible = r <= cc
    upper = visible.astype(BF16)
    lower = (cc <= r).astype(BF16)
    grow = grow_ref[...]
    gcol = gcol_ref[...]
    b_row_all = sum(_dot(piece, upper) for piece in _split3(grow[A_HEADS:, :]))
    b_col_all = sum(_dot(lower, piece) for piece in _split3(gcol[:, A_HEADS:]))
    x_col_all = gcol[:, :A_HEADS] - b_col_all
    ones_rows = jnp.ones((N_ROWS, L), BF16)

    heads = range(A_HEADS)
    qts = [qt_ref[h * dk:(h + 1) * dk, :] for h in heads]
    khs = [k_ref[:, h * dk:(h + 1) * dk] for h in heads]
    brs = [b_row_all[h:h + 1, :] for h in heads]
    m_prevs = [m_ref[h:h + 1, 0:1] for h in heads]
    kqs = [_dot(khs[h], qts[h]) for h in heads]
    ws, scs = [], []
    for h in heads:
        d_intra = jnp.where(visible, brs[h] + x_col_all[:, h:h + 1], -jnp.inf)
        d_inter = brs[h] + m_prevs[h]
        m_t = jnp.maximum(d_inter, jnp.max(d_intra, axis=0, keepdims=True))
        ws.append((jnp.exp(d_intra - m_t) * kqs[h]).astype(BF16))
        scs.append((jnp.exp(d_inter - m_t), jnp.exp(-m_t)))
    v_augs = [jnp.concatenate([vt_ref[h * dvh:(h + 1) * dvh, :], ones_rows], axis=0) for h in heads]
    cts = [ct_ref[h] for h in heads]
    tots = [_dot(v_augs[h], ws[h]) + scs[h][0] * _dot(cts[h].astype(BF16), qts[h]) for h in heads]
    for h in heads:
        num = tots[h][:dvh, :]
        den = tots[h][dvh:dvh + 1, :]
        hh = num / jnp.maximum(jnp.abs(den), scs[h][1])
        hh = hh * lax.rsqrt(jnp.mean(hh * hh, axis=0, keepdims=True) + EPS)
        hs_ref[:, h * dvh:(h + 1) * dvh] = hh.T.astype(hs_ref.dtype)
    for h in heads:
        b_last = brs[h][:, L - 1:L]
        g = b_last - brs[h] + grow[h:h + 1, :]
        m_new = jnp.maximum(b_last + m_prevs[h], jnp.max(g, axis=1, keepdims=True))
        wk = jnp.exp(g - m_new)
        decay = jnp.exp(b_last + m_prevs[h] - m_new)
        ct_ref[h] = decay * cts[h] + _dot((v_augs[h].astype(F32) * wk).astype(BF16), khs[h])
        m_ref[h:h + 1, :] = jnp.broadcast_to(m_new, (1, m_ref.shape[1]))


def mlstm_cell(qt, k, vt, grow, gcol, bsz, seq):
    t, dqk = k.shape
    dv = vt.shape[0]
    L = min(A_L, seq)
    nc = seq // L
    ng = grow.shape[1]
    dk = dqk // A_HEADS
    dvh = dv // A_HEADS
    return pl.pallas_call(
        _mlstm_cell_kernel,
        grid=(bsz, nc),
        in_specs=[pl.BlockSpec((dqk, L), lambda b, c: (0, b * nc + c)),
                  pl.BlockSpec((L, dqk), lambda b, c: (b * nc + c, 0)),
                  pl.BlockSpec((dv, L), lambda b, c: (0, b * nc + c)),
                  pl.BlockSpec((None, ng, L), lambda b, c: (b, 0, c)),
                  pl.BlockSpec((None, L, ng), lambda b, c: (b, c, 0))],
        out_specs=pl.BlockSpec((L, dv), lambda b, c: (b * nc + c, 0)),
        out_shape=jax.ShapeDtypeStruct((t, dv), BF16),
        scratch_shapes=[pltpu.VMEM((A_HEADS, dvh + N_ROWS, dk), F32), pltpu.VMEM((A_HEADS, 128), F32)],
        compiler_params=_cp(("arbitrary", "arbitrary")), name="mlstm_cell",
    )(qt, k, vt, grow, gcol)


def _fox_q_kernel(has_moe, qscale, *refs):
    xin, h, rest, issue, finish = _prologue(has_moe, refs)
    w_ref = rest[0]
    outs = rest[1:]
    if has_moe:
        outs[0][...] = xin
        outs = outs[1:]
    hb = h.astype(BF16)
    d = w_ref.shape[1]
    n = d // PIECE
    for g in range(n):
        issue(g, n)
        cols = slice(g * PIECE, (g + 1) * PIECE)
        outs[0][:, cols] = (_dot(hb, w_ref[:, cols]) * qscale).astype(BF16)
    finish()


def fox_q(x, moe, gf, nw, shift, scale, w_q, bsz, seq):
    t, d = x.shape
    has_moe = moe is not None
    tm = min(TM, seq)
    ns = seq // tm
    ys, dest = moe if has_moe else (None, None)
    args = [x] + ([ys, gf] if has_moe else []) + [nw, shift, scale, w_q.astype(BF16)]
    in_specs = _prologue_specs(has_moe, ns, d, tm) + [_full((d, d))]
    row = pl.BlockSpec((tm, d), lambda b, s, *_: (b * ns + s, 0))
    out_specs = [row]
    out_shape = [jax.ShapeDtypeStruct((t, d), BF16)]
    if has_moe:
        out_specs = [row] + out_specs
        out_shape = [jax.ShapeDtypeStruct((t, d), F32)] + out_shape
    qscale = float(d // B_HEADS) ** -0.5 * LOG2E
    res = _prologue_call(functools.partial(_fox_q_kernel, has_moe, qscale), has_moe, dest, args, in_specs, out_specs,
                         out_shape, bsz, ns, tm, d, "fox_q")
    if has_moe:
        return res[0], res[1]
    return x, res[0]


N_BIAS_PIECES = 3


def _bias_select_matrices(d, n_heads):
    import numpy as np
    dh = d // n_heads
    pq = np.zeros((N_BIAS_PIECES, n_heads, d), np.float32)
    pk = np.zeros((N_BIAS_PIECES, n_heads, d), np.float32)
    oq = np.zeros((1, d), np.float32)
    ok = np.zeros((1, d), np.float32)
    for h in range(n_heads):
        base = (h // 2) * 2 * dh + (dh if h % 2 == 0 else 0)
        for p in range(N_BIAS_PIECES):
            pq[p, h, base + p] = 1.0
            ok[0, base + p] = 1.0
            oq[0, base + N_BIAS_PIECES + p] = 1.0
            pk[p, h, base + N_BIAS_PIECES + p] = -1.0
    return pq, pk, oq, ok


def _split3(f):
    a = f.astype(BF16)
    r1 = f - a.astype(F32)
    b = r1.astype(BF16)
    c = (r1 - b.astype(F32)).astype(BF16)
    return a, b, c


def _shared_kv_kernel(x_ref, nw_ref, sh_ref, sc_ref, w_ref, wvt_ref, wf_ref, brow_ref, lo_ref, pq_ref, pk_ref, oq_ref,
                      ok_ref, ke_ref, ko_ref, vt_ref, eq_ref, cc_ref):
    s = pl.program_id(1)
    d = x_ref.shape[1]
    dh = d // B_HEADS

    @pl.when(s == 0)
    def _():
        cc_ref[...] = jnp.zeros_like(cc_ref)

    h = _norm_mod(x_ref[...], nw_ref[...], sh_ref[...], sc_ref[...])
    hb = h.astype(BF16)
    lc = _log_sigmoid(_dot(hb, wf_ref[...]) + brow_ref[...])
    fcum = sum(_dot(lo_ref[...], piece) for piece in _split3(lc)) + cc_ref[...]
    cc_ref[...] += jnp.sum(lc, axis=0, keepdims=True)
    pieces = jnp.concatenate(_split3(fcum * LOG2E), axis=1)
    eq = oq_ref[...] + _dot(pieces, pq_ref[...])
    ek = ok_ref[...] + _dot(pieces, pk_ref[...])
    eq_ref[...] = eq.astype(BF16)
    even = (lax.broadcasted_iota(jnp.int32, (1, d), 1) % (2 * dh)) < dh
    k = _dot(hb, w_ref[...])
    ke_ref[...] = jnp.where(even, k, ek).astype(BF16)
    ko_ref[...] = jnp.where(even, ek, k).astype(BF16)
    vt_ref[...] = _dot_nt(wvt_ref[...], hb).astype(BF16)


def shared_kv(x, nw, shift, scale, w_kv, b_fgate, bsz, seq):
    t, d = x.shape
    tm = min(TM, seq)
    ns = seq // tm
    nh = B_HEADS
    w_k = w_kv[:, :d].astype(BF16)
    w_vt = w_kv[:, d:2 * d].astype(BF16).T
    w_f = w_kv[:, 2 * d:].astype(BF16)
    idx = jnp.arange(tm)
    lower = (idx[None, :] <= idx[:, None]).astype(BF16)
    pq, pk, oq, ok = _bias_select_matrices(d, nh)
    row = pl.BlockSpec((tm, d), lambda b, s: (b * ns + s, 0))
    vec = pl.BlockSpec((None, 1, d), lambda b, s: (b, 0, 0))
    sel = _full((N_BIAS_PIECES * nh, d))
    return pl.pallas_call(
        _shared_kv_kernel,
        grid=(bsz, ns),
        in_specs=[row, _full((1, d)), vec, vec, _full((d, d)), _full((d, d)), _full((d, nh)), _full((1, nh)),
                  _full((tm, tm)), sel, sel, _full((1, d)), _full((1, d))],
        out_specs=[row, row, pl.BlockSpec((d, tm), lambda b, s: (0, b * ns + s)), row],
        out_shape=[jax.ShapeDtypeStruct((t, d), BF16), jax.ShapeDtypeStruct((t, d), BF16),
                   jax.ShapeDtypeStruct((d, t), BF16), jax.ShapeDtypeStruct((t, d), BF16)],
        scratch_shapes=[pltpu.VMEM((1, nh), F32)],
        compiler_params=_cp(("arbitrary", "arbitrary")), name="shared_kv",
    )(x, nw, shift, scale, w_k, w_vt, w_f, b_fgate.reshape(1, nh), lower,
      jnp.asarray(pq.reshape(-1, d), BF16), jnp.asarray(pk.reshape(-1, d), BF16), jnp.asarray(oq), jnp.asarray(ok))


def _fox_attn_kernel(q_ref, eq_ref, ke_ref, ko_ref, vt_ref, o_ref, st_ref):
    i = pl.program_id(2)
    tq = q_ref.shape[0]
    tk = tq
    ck = min(ATT_CHUNK, tk)
    nchunk = tk // ck
    dh2 = q_ref.shape[1]
    dh = dh2 // 2
    lo_mask = lax.broadcasted_iota(jnp.int32, (1, dh2), 1) < dh
    q = q_ref[...]
    eq = eq_ref[...]
    qs = (jnp.where(lo_mask, q, eq), jnp.where(lo_mask, eq, q))
    k_refs = (ke_ref, ko_ref)
    ones_rows = jnp.ones((N_ROWS, ck), BF16)

    def tail(x, q0, new):
        return new if q0 == 0 else jnp.concatenate([x[:, :q0], new], axis=1)

    def qk_chunk(e, kt, c, mx, diag):
        off = pl.multiple_of(kt * tk + c * ck, ck)
        q0 = c * ck if diag else 0
        st = _dot_nt(k_refs[e][pl.ds(off, ck), :], qs[e][q0:, :])
        if diag:
            visible = (lax.broadcasted_iota(jnp.int32, st.shape, 0) <= lax.broadcasted_iota(jnp.int32, st.shape, 1))
            st = jnp.where(visible, st, -jnp.inf)
        st_ref[e, c * ck:(c + 1) * ck, q0:] = st
        cm = jnp.max(st, axis=0, keepdims=True)
        return cm if mx is None else tail(mx, q0, jnp.maximum(mx[:, q0:], cm))

    def pv_chunk(e, kt, c, state, diag):
        m, acc = state
        off = pl.multiple_of(kt * tk + c * ck, ck)
        q0 = c * ck if diag else 0
        pt = jnp.exp2(st_ref[e, c * ck:(c + 1) * ck, q0:] - m[:, q0:])
        v_aug = jnp.concatenate([vt_ref[e * dh:(e + 1) * dh, pl.ds(off, ck)], ones_rows], axis=0)
        return m, tail(acc, q0, acc[:, q0:] + _dot(v_aug, pt.astype(BF16)))

    def phase(cur, state, tile_max, nxt):
        m, acc = state
        m_new = jnp.maximum(m, tile_max)
        state = (m_new, jnp.exp2(m - m_new) * acc)
        mx = None
        for c in range(nchunk):
            mx = qk_chunk(nxt[0], nxt[1], c, mx, nxt[2])
            state = pv_chunk(cur[0], cur[1], c, state, cur[2])
        return state, mx

    init = (jnp.full((1, tq), -1e30, F32), jnp.zeros((dh + N_ROWS, tq), F32))
    mx0 = None
    for c in range(nchunk):
        mx0 = qk_chunk(0, i, c, mx0, True)
    s0, mx1 = phase((0, i, True), init, mx0, (1, i, True))
    s1, mx0 = phase((1, i, True), init, mx1, (0, 0, False))

    def trip(j, carry):
        s0, s1, mx0 = carry
        s0, mx1 = phase((0, j, False), s0, mx0, (1, j, False))
        s1, mx0 = phase((1, j, False), s1, mx1, (0, j + 1, False))
        return s0, s1, mx0

    s0, s1, _ = lax.fori_loop(0, i, trip, (s0, s1, mx0))
    ot = jnp.concatenate([acc[:dh] / acc[dh:dh + 1] for _, acc in (s0, s1)], axis=0)
    o_ref[...] = ot.T.astype(o_ref.dtype)


def fox_attn(q, eq, ke, ko, vt, bsz, seq):
    t, d = q.shape
    nh = B_HEADS
    dh2 = 2 * (d // nh)
    tq = min(TQ, seq)
    nq = seq // tq
    qspec = pl.BlockSpec((tq, dh2), lambda b, hp, i: (b * nq + i, hp))
    kspec = pl.BlockSpec((seq, dh2), lambda b, hp, i: (b, hp))
    vspec = pl.BlockSpec((dh2, seq), lambda b, hp, i: (hp, b))
    return pl.pallas_call(
        _fox_attn_kernel,
        grid=(bsz, nh // 2, nq),
        in_specs=[qspec, qspec, kspec, kspec, vspec],
        out_specs=qspec,
        out_shape=jax.ShapeDtypeStruct((t, d), BF16),
        scratch_shapes=[pltpu.VMEM((2, tq, tq), F32)],
        compiler_params=_cp(("arbitrary", "arbitrary", "arbitrary")), name="fox_attn",
    )(q, eq, ke, ko, vt)


def _route(logits, bias_col):
    aff = _sigmoid(logits)
    sel = aff + bias_col
    s = [sel[e:e + 1, :] for e in range(N_EXPERTS)]
    gs = []
    for g in range(N_GROUPS):
        v = s[EPG * g:EPG * (g + 1)]
        best = v[0] + v[1]
        for i in range(EPG):
            for j in range(i + 1, EPG):
                if (i, j) != (0, 1):
                    best = jnp.maximum(best, v[i] + v[j])
        gs.append(best)
    grp = jnp.zeros_like(gs[0], dtype=jnp.int32)
    best = gs[0]
    for g in range(1, N_GROUPS):
        upd = gs[g] > best
        grp = jnp.where(upd, g, grp)
        best = jnp.where(upd, gs[g], best)

    def pick(arrs, j):
        out = arrs[(N_GROUPS - 1) * EPG + j]
        for g in range(N_GROUPS - 2, -1, -1):
            out = jnp.where(grp == g, arrs[g * EPG + j], out)
        return out

    v = [pick(s, j) for j in range(EPG)]
    i1 = jnp.zeros_like(grp)
    b1 = v[0]
    for j in range(1, EPG):
        upd = v[j] > b1
        i1 = jnp.where(upd, j, i1)
        b1 = jnp.where(upd, v[j], b1)
    i2 = jnp.full_like(grp, -1)
    b2 = jnp.full_like(b1, -jnp.inf)
    for j in range(EPG):
        upd = (i1 != j) & ((i2 < 0) | (v[j] > b2))
        i2 = jnp.where(upd, j, i2)
        b2 = jnp.where(upd, v[j], b2)
    lo = jnp.minimum(i1, i2)
    hi = jnp.maximum(i1, i2)
    base = jnp.where(lo == 0, 0, jnp.where(lo == 1, 3, 5))
    return grp * N_PAIRS + base + (hi - lo - 1)


def _post_mix_kernel(is_mlstm, *refs):
    if is_mlstm:
        hs_ref, og_ref, mh_ref = refs[:3]
        refs = refs[3:]
        og = og_ref[...].astype(F32)
        mix = hs_ref[...].astype(F32) * mh_ref[...] * _sigmoid(og)
    else:
        mix = refs[0][...]
        refs = refs[1:]
    (w_ref, x_ref, gm_ref, nw_ref, sh_ref, sc_ref, wrt_ref, rb_ref, us_ref,
     xnew_ref, h2_ref, cls_ref, rank_ref, cnt_ref) = refs
    first = (pl.program_id(0) == 0) & (pl.program_id(1) == 0)

    @pl.when(first)
    def _():
        cnt_ref[...] = jnp.zeros_like(cnt_ref)

    y = _dot(mix.astype(BF16), w_ref[...])
    xnew = x_ref[...] + gm_ref[...] * y
    xnew_ref[...] = xnew
    h2 = _norm_mod(xnew, nw_ref[...], sh_ref[...], sc_ref[...])
    h2_ref[...] = h2.reshape(h2_ref.shape)
    h_hi = h2.astype(BF16)
    h_lo = (h2 - h_hi.astype(F32)).astype(BF16)
    ne = wrt_ref.shape[0] // 2
    part = _dot_nt(wrt_ref[...], h_hi)
    logits = part[:ne] + part[ne:] + _dot_nt(wrt_ref[:ne, :], h_lo)
    cls = _route(logits, rb_ref[...])
    cls_ref[...] = cls
    tm = cls.shape[1]
    onehot = (lax.broadcasted_iota(jnp.int32, (CLS_PAD, tm), 0) == cls).astype(F32)
    prefix = _dot(onehot.astype(BF16), us_ref[...])
    carry = cnt_ref[:, 0:1]
    rank = jnp.sum(onehot * (prefix + carry), axis=0, keepdims=True)
    rank_ref[...] = rank.astype(jnp.int32)
    cnt_ref[...] += jnp.sum(onehot, axis=1, keepdims=True)


def post_mix(mix_args, is_mlstm, w_o, x, gm, nw, shift, scale, w_router, router_bias, bsz, seq):
    t, d = x.shape
    tm = min(TM, seq)
    ns = seq // tm
    ne = N_EXPERTS
    idx = jnp.arange(tm)
    upper_strict = (idx[:, None] < idx[None, :]).astype(BF16)
    row = pl.BlockSpec((tm, d), lambda b, s: (b * ns + s, 0))
    vec = pl.BlockSpec((None, 1, d), lambda b, s: (b, 0, 0))
    lane_row = pl.BlockSpec((None, 1, tm), lambda b, s: (b, 0, s))
    if is_mlstm:
        mix_specs = [row, row, _full((1, d))]
    else:
        mix_specs = [row]
    in_specs = mix_specs + [_full((d, d)), row, vec, _full((1, d)), vec, vec, _full((2 * ne, d)), _full((ne, 1)),
                            _full((tm, tm))]
    wr = w_router.T
    wr_hi = wr.astype(BF16)
    wr_split = jnp.concatenate([wr_hi, (wr - wr_hi.astype(F32)).astype(BF16)], axis=0)
    row3 = pl.BlockSpec((tm, d // LANES, LANES), lambda b, s: (b * ns + s, 0, 0))
    out_specs = [row, row3, lane_row, lane_row, _full((CLS_PAD, 128))]
    out_shape = [jax.ShapeDtypeStruct((t, d), F32), jax.ShapeDtypeStruct((t, d // LANES, LANES), F32),
                 jax.ShapeDtypeStruct((bsz, 1, seq), jnp.int32), jax.ShapeDtypeStruct((bsz, 1, seq), jnp.int32),
                 jax.ShapeDtypeStruct((CLS_PAD, 128), F32)]
    return pl.pallas_call(
        functools.partial(_post_mix_kernel, is_mlstm),
        grid=(bsz, ns), in_specs=in_specs, out_specs=out_specs, out_shape=out_shape,
        compiler_params=_cp(("arbitrary", "arbitrary")), name="post_mix",
    )(*mix_args, w_o.astype(BF16), x, gm, nw, shift, scale, wr_split, router_bias.reshape(ne, 1), upper_strict)


_FILL_PIECES = tuple(1 << k for k in range(MB.bit_length() - 2, -1, -1))


def _row_scatter_kernel(rb, dest_ref, fill_start_ref, fill_n_ref, src_ref, dst_ref, zero_ref, sem, zsem):
    base = pl.program_id(0) * rb

    def fill(do):
        for c in range(N_CLASSES):
            n = fill_n_ref[c]
            for p in _FILL_PIECES:
                row = fill_start_ref[c] + (n & ~(2 * p - 1))

                @pl.when((n & p) != 0)
                def _():
                    do(pltpu.make_async_copy(zero_ref.at[pl.ds(0, p)], dst_ref.at[pl.ds(row, p)], zsem))
        for k in range(N_CLASSES):
            row = pl.multiple_of(fill_start_ref[N_CLASSES] + k * MB, MB)

            @pl.when(k < fill_n_ref[N_CLASSES])
            def _():
                do(pltpu.make_async_copy(zero_ref, dst_ref.at[pl.ds(row, MB)], zsem))

    @pl.when(pl.program_id(0) == 0)
    def _():
        zero_ref[...] = jnp.zeros_like(zero_ref)
        fill(lambda cp: cp.start())

    def start(g, carry):
        for u in range(DMA_GROUP):
            rr = g * DMA_GROUP + u
            pltpu.make_async_copy(src_ref.at[rr], dst_ref.at[dest_ref[base + rr]], sem).start(priority=u % DMA_QUEUES)
        return carry

    def wait(rr, carry):
        pltpu.make_async_copy(src_ref.at[0], dst_ref.at[0], sem).wait()
        return carry

    lax.fori_loop(0, rb // DMA_GROUP, start, 0)
    lax.fori_loop(0, rb, wait, 0, unroll=8)

    @pl.when(pl.program_id(0) == 0)
    def _():
        fill(lambda cp: cp.wait())


def row_scatter(dest, fill_start, fill_n, src, n_rows):
    t = src.shape[0]
    rb = min(RB, t)
    return pl.pallas_call(
        functools.partial(_row_scatter_kernel, rb),
        grid_spec=pltpu.PrefetchScalarGridSpec(
            num_scalar_prefetch=3, grid=(t // rb,),
            in_specs=[pl.BlockSpec((rb,) + src.shape[1:], lambda i, *_: (i, 0, 0))],
            out_specs=pl.BlockSpec(memory_space=pl.ANY),
            scratch_shapes=[pltpu.VMEM((MB,) + src.shape[1:], src.dtype),
                            pltpu.SemaphoreType.DMA, pltpu.SemaphoreType.DMA]),
        out_shape=jax.ShapeDtypeStruct((n_rows,) + src.shape[1:], src.dtype),
        compiler_params=_cp(("arbitrary",)), name="row_scatter",
    )(dest, fill_start, fill_n, src)


def _experts_kernel(elo_ref, ehi_ref, nused_ref, x_ref, wrt_ref,
                    wg_lo, wu_lo, wd_lo, wg_hi, wu_hi, wd_hi, y_ref):
    i = pl.program_id(0)

    @pl.when(i < nused_ref[0])
    def _():
        x = x_ref[...].reshape(x_ref.shape[0], -1)
        xb = x.astype(BF16)
        aff_lo = _sigmoid(jnp.sum(x * wrt_ref[pl.ds(elo_ref[i], 1), :], axis=1, keepdims=True))
        aff_hi = _sigmoid(jnp.sum(x * wrt_ref[pl.ds(ehi_ref[i], 1), :], axis=1, keepdims=True))
        tot = aff_lo + aff_hi
        acc = None
        for gate, wg, wu, wd in ((aff_lo / tot, wg_lo, wu_lo, wd_lo), (aff_hi / tot, wg_hi, wu_hi, wd_hi)):
            hg = _dot(xb, wg[...])
            hu = _dot(xb, wu[...])
            act = hg * _sigmoid(hg) * hu * gate
            part = _dot(act.astype(BF16), wd[...])
            acc = part if acc is None else acc + part
        y_ref[...] = acc.reshape(y_ref.shape)

    @pl.when(i >= nused_ref[0])
    def _():
        y_ref[...] = jnp.zeros_like(y_ref)


def experts(xs, w_router_t, blk_lo, blk_hi, nused, w_gate, w_up, w_down):
    p = xs.shape[0]
    d = xs.shape[1] * xs.shape[2]
    nblk = p // MB
    de = w_gate.shape[2]
    xrow = pl.BlockSpec((MB,) + xs.shape[1:], lambda i, lo, hi, nu: (i, 0, 0))
    xrow_in = pl.BlockSpec((MB,) + xs.shape[1:], lambda i, lo, hi, nu: (jnp.minimum(i, nu[0] - 1), 0, 0))
    wrt = pl.BlockSpec(w_router_t.shape, lambda i, lo, hi, nu: (0, 0))
    w_in_lo = pl.BlockSpec((None, d, de), lambda i, lo, hi, nu: (lo[i], 0, 0))
    w_in_hi = pl.BlockSpec((None, d, de), lambda i, lo, hi, nu: (hi[i], 0, 0))
    w_out_lo = pl.BlockSpec((None, de, d), lambda i, lo, hi, nu: (lo[i], 0, 0))
    w_out_hi = pl.BlockSpec((None, de, d), lambda i, lo, hi, nu: (hi[i], 0, 0))
    return pl.pallas_call(
        _experts_kernel,
        grid_spec=pltpu.PrefetchScalarGridSpec(
            num_scalar_prefetch=3, grid=(nblk,),
            in_specs=[xrow_in, wrt, w_in_lo, w_in_lo, w_out_lo, w_in_hi, w_in_hi, w_out_hi],
            out_specs=xrow),
        out_shape=jax.ShapeDtypeStruct(xs.shape, F32),
        compiler_params=_cp(("arbitrary",)), name="experts",
    )(blk_lo, blk_hi, nused, xs, w_router_t, w_gate, w_up, w_down, w_gate, w_up, w_down)


_PAIR_LO = (0, 0, 0, 1, 1, 2)
_PAIR_HI = (1, 2, 3, 2, 3, 3)


def moe(h2, cls, rank, counts, w_router_t, w_gate, w_up, w_down):
    t = h2.shape[0]
    p = t + N_CLASSES * MB
    nblk = p // MB
    cls = cls.reshape(t)
    counts = counts[:N_CLASSES, 0].astype(jnp.int32)
    padded = (counts + MB - 1) // MB * MB
    pad_end = jnp.cumsum(padded)
    pad_start = pad_end - padded
    dest = (pad_start[cls] + rank.reshape(t)).astype(jnp.int32)
    blk_row = jnp.arange(nblk, dtype=jnp.int32) * MB
    blk_cls = jnp.minimum(jnp.sum((pad_end[None, :] <= blk_row[:, None]).astype(jnp.int32), axis=1), N_CLASSES - 1)
    grp = blk_cls // N_PAIRS
    pr = blk_cls % N_PAIRS
    blk_lo = grp * EPG + jnp.asarray(_PAIR_LO, jnp.int32)[pr]
    blk_hi = grp * EPG + jnp.asarray(_PAIR_HI, jnp.int32)[pr]
    nused = (pad_end[-1:] // MB).astype(jnp.int32)
    fill_start = jnp.concatenate([pad_start + counts, pad_end[-1:]]).astype(jnp.int32)
    fill_n = jnp.concatenate([padded - counts, (p - pad_end[-1:]) // MB]).astype(jnp.int32)
    xs = row_scatter(dest, fill_start, fill_n, h2, p)
    ys = experts(xs, w_router_t, blk_lo, blk_hi, nused, w_gate, w_up, w_down)
    return ys, dest


def _final_kernel(*refs):
    _, h, rest, issue, finish = _prologue(True, refs)
    n = 8
    rows = h.shape[0] // n
    for g in range(n):
        issue(g, n)
        rest[0][g * rows:(g + 1) * rows, :] = h[g * rows:(g + 1) * rows, :]
    finish()


def final_norm(x, moe_out, gf, nw, shift, scale, bsz, seq):
    t, d = x.shape
    tm = min(TM, seq)
    ns = seq // tm
    ys, dest = moe_out
    row = pl.BlockSpec((tm, d), lambda b, s, *_: (b * ns + s, 0))
    return _prologue_call(_final_kernel, True, dest, [x, ys, gf, nw, shift, scale], _prologue_specs(True, ns, d, tm),
                          row, jax.ShapeDtypeStruct((t, d), F32), bsz, ns, tm, d, "final_norm")


def kernel(x, c, a_w_in, a_b_gates, a_mh_norm, a_w_out, kv_norm, w_ada_kv, b_ada_kv, w_kv, b_fgate, b_w_q, b_w_o,
           norm_mix, norm_ffn, w_ada, b_ada, w_router, router_bias, w_gate, w_up, w_down, norm_final, w_ada_final,
           b_ada_final):
    bsz, seq, d = x.shape
    depth = w_ada.shape[0]
    n_a = a_w_in.shape[0]
    t = bsz * seq
    xf = x.reshape(t, d)

    mods = ada(c, w_ada, b_ada)
    kv_mod = ada(c, w_ada_kv[None], b_ada_kv[None])[0]
    fin = ada(c, w_ada_final[None], b_ada_final[None])[0]

    def vecs(m, n):
        return [m[:, None, i * d:(i + 1) * d] for i in range(n)]

    wg_b = w_gate.astype(BF16)
    wu_b = w_up.astype(BF16)
    wd_b = w_down.astype(BF16)

    moe_out = None
    gf_prev = None
    kv = None
    for layer in range(depth):
        sh_m, sc_m, g_m, sh_f, sc_f, g_f = vecs(mods[layer], 6)
        nm = norm_mix[layer].reshape(1, d)
        if layer < n_a:
            xf, (q, k, v, o, grow, gcol) = mlstm_in(xf, moe_out, gf_prev, nm, sh_m, sc_m, a_w_in[layer],
                                                     a_b_gates[layer], bsz, seq)
            hs = mlstm_cell(q, k, v, grow, gcol, bsz, seq)
            mix_args = (hs, o, a_mh_norm[layer].reshape(1, d))
            w_o = a_w_out[layer]
        else:
            j = layer - n_a
            xf, q = fox_q(xf, moe_out, gf_prev, nm, sh_m, sc_m, b_w_q[j], bsz, seq)
            if layer == n_a:
                kv_sh, kv_sc = vecs(kv_mod, 2)
                kv = shared_kv(xf, kv_norm.reshape(1, d), kv_sh, kv_sc, w_kv, b_fgate, bsz, seq)
            ke, ko, vt, eq = kv
            att = fox_attn(q, eq, ke, ko, vt, bsz, seq)
            mix_args = (att,)
            w_o = b_w_o[j]
        xf, h2, cls, rank, counts = post_mix(
            mix_args, layer < n_a, w_o, xf, g_m, norm_ffn[layer].reshape(1, d), sh_f, sc_f, w_router, router_bias,
            bsz, seq)
        moe_out = moe(h2, cls, rank, counts, w_router.T, wg_b[layer], wu_b[layer], wd_b[layer])
        gf_prev = g_f
    fin_sh, fin_sc = vecs(fin, 2)
    out = final_norm(xf, moe_out, gf_prev, norm_final.reshape(1, d), fin_sh, fin_sc, bsz, seq)
    return out.reshape(bsz, seq, d)
```
